```python
import math
import jax, jax.numpy as jnp
from jax import lax
import numpy as np

D_MODEL = 2048
BATCH = 4
SEQ = 2048
DEPTH = 4

CHUNK = 64
D_MIX = D_MODEL
CONV_W = 4
NORM_EPS = 1e-6
LRU_WIDTH = D_MIX // 4
LRU_BLOCKS = 8
LRU_BLOCK = LRU_WIDTH // LRU_BLOCKS
LRU_C = 8.0
GDN_HEAD = 128
GDN_HEADS = (3 * D_MIX // 8) // GDN_HEAD
GDN_WIDTH = GDN_HEADS * GDN_HEAD
RWKV_HEAD = 64
RWKV_WIDTH = D_MIX - LRU_WIDTH - GDN_WIDTH
RWKV_HEADS = RWKV_WIDTH // RWKV_HEAD
DECAY_LORA = 96
ICLR_LORA = 96
RWKV_SHIFT_WIDTH = 3 * RWKV_WIDTH + DECAY_LORA + ICLR_LORA
RWKV_GN_EPS = 64e-5
IN_SIZES = (LRU_WIDTH, LRU_WIDTH, 3 * GDN_WIDTH, GDN_WIDTH, GDN_HEADS, GDN_HEADS, RWKV_SHIFT_WIDTH, RWKV_WIDTH)
N_IN = 2 * LRU_WIDTH + 4 * GDN_WIDTH + 2 * GDN_HEADS + RWKV_SHIFT_WIDTH + RWKV_WIDTH

kernel_name = 'hybrid_rglru_gdn_rwkv7_trunk'


def rmsnorm(x, g):
    xf = x.astype(jnp.float32)
    y = xf * lax.rsqrt(jnp.mean(xf * xf, axis=-1, keepdims=True) + NORM_EPS) * g.astype(jnp.float32)
    return y.astype(x.dtype)


def split_cols(t, sizes):
    idx = []
    s = 0
    for n in sizes[:-1]:
        s += n
        idx.append(s)
    return jnp.split(t, idx, axis=-1)


def causal_dwconv(x, w):
    c = x.shape[-1]
    k = w.shape[0]
    return lax.conv_general_dilated(x, w[:, None, :], window_strides=(1,), padding=[(k - 1, 0)],
                                    dimension_numbers=('NWC', 'WIO', 'NWC'), feature_group_count=c)


def l2norm(t):
    return t * lax.rsqrt(jnp.sum(t * t, axis=-1, keepdims=True) + 1e-6)


def token_shift(x, mu):
    x_prev = jnp.pad(x, ((0, 0), (1, 0), (0, 0)))[:, :-1]
    return x + (x_prev - x) * mu


def rg_lru(xa, conv_w, conv_b, w_x, b_x, w_a, b_a, lam):
    f32 = jnp.float32
    bsz, seq, _ = xa.shape
    x = causal_dwconv(xa.astype(f32), conv_w.astype(f32)) + conv_b.astype(f32)
    xb = x.reshape(bsz, seq, LRU_BLOCKS, LRU_BLOCK)
    gate_x = jax.nn.sigmoid(jnp.einsum('bsni,nij->bsnj', xb, w_x.astype(f32)).reshape(bsz, seq, LRU_WIDTH) + b_x.astype(f32))
    gate_a = jax.nn.sigmoid(jnp.einsum('bsni,nij->bsnj', xb, w_a.astype(f32)).reshape(bsz, seq, LRU_WIDTH) + b_a.astype(f32))
    log_a = -LRU_C * gate_a * jax.nn.softplus(-lam.astype(f32))
    a = jnp.exp(log_a)
    mult = jnp.sqrt(jnp.maximum(-jnp.expm1(2.0 * log_a), 0.0))
    u = mult * (gate_x * x)

    def combine(left, right):
        a1, b1 = left
        a2, b2 = right
        return a1 * a2, a2 * b1 + b2

    _, h = lax.associative_scan(combine, (a, u), axis=1)
    return h


def chunk_gated_delta_rule(q, k, v, beta, g):
    bsz, seq, h, d = q.shape
    n = seq // CHUNK
    c = CHUNK

    def to_chunks(t):
        return t.reshape(bsz, n, c, h, -1).transpose(0, 3, 1, 2, 4)

    q, k, v = to_chunks(q), to_chunks(k), to_chunks(v)
    beta = beta.reshape(bsz, n, c, h).transpose(0, 3, 1, 2)
    g = g.reshape(bsz, n, c, h).transpose(0, 3, 1, 2)
    decay = jnp.cumsum(g, axis=-1)
    pos = jnp.arange(c)
    incl = pos[:, None] >= pos[None, :]
    strict = pos[:, None] > pos[None, :]
    diff = decay[..., :, None] - decay[..., None, :]
    lmask = jnp.exp(jnp.where(incl, diff, -jnp.inf))
    kb = k * beta[..., None]
    a_mat = jnp.where(strict, jnp.einsum('bhncd,bhnmd->bhncm', kb, k) * lmask, 0.0)
    eye = jnp.broadcast_to(jnp.eye(c, dtype=a_mat.dtype), a_mat.shape)
    t_mat = lax.linalg.triangular_solve(a_mat, eye, left_side=True, lower=True, unit_diagonal=True)
    u = jnp.einsum('bhncm,bhnme->bhnce', t_mat, v * beta[..., None])
    w = jnp.einsum('bhncm,bhnmd->bhncd', t_mat, kb * jnp.exp(decay)[..., None])
    qk = jnp.where(incl, jnp.einsum('bhncd,bhnmd->bhncm', q, k) * lmask, 0.0)
    q_dec = q * jnp.exp(decay)[..., None]
    k_dec = k * jnp.exp(decay[..., -1:] - decay)[..., None]
    g_last = jnp.exp(decay[..., -1])

    def step(state, inp):
        u_i, w_i, qk_i, qd_i, kd_i, gl_i = inp
        v_new = u_i - jnp.einsum('bhcd,bhde->bhce', w_i, state)
        o = jnp.einsum('bhcd,bhde->bhce', qd_i, state) + jnp.einsum('bhcm,bhme->bhce', qk_i, v_new)
        state = state * gl_i[..., None, None] + jnp.einsum('bhcd,bhce->bhde', kd_i, v_new)
        return state, o

    xs = tuple(jnp.moveaxis(t, 2, 0) for t in (u, w, qk, q_dec, k_dec, g_last))
    s0 = jnp.zeros((bsz, h, d, v.shape[-1]), jnp.float32)
    _, o = lax.scan(step, s0, xs)
    return o.transpose(1, 0, 3, 2, 4).reshape(bsz, seq, h, -1)


def gated_deltanet(qkv, beta_logit, a_logit, conv_w, a_log, dt_bias, norm_g):
    f32 = jnp.float32
    bsz, seq, _ = qkv.shape
    qkv = jax.nn.silu(causal_dwconv(qkv.astype(f32), conv_w.astype(f32)))
    q, k, v = jnp.split(qkv, 3, axis=-1)
    q = l2norm(q.reshape(bsz, seq, GDN_HEADS, GDN_HEAD)) * (GDN_HEAD ** -0.5)
    k = l2norm(k.reshape(bsz, seq, GDN_HEADS, GDN_HEAD))
    v = v.reshape(bsz, seq, GDN_HEADS, GDN_HEAD)
    beta = jax.nn.sigmoid(beta_logit.astype(f32))
    g = -jnp.exp(a_log.astype(f32)) * jax.nn.softplus(a_logit.astype(f32) + dt_bias.astype(f32))
    o = chunk_gated_delta_rule(q, k, v, beta, g)
    o = o * lax.rsqrt(jnp.mean(o * o, axis=-1, keepdims=True) + NORM_EPS) * norm_g.astype(f32)
    return o.reshape(bsz, seq, GDN_WIDTH)


def rwkv7_time_mix(c_in, mu, w0, w_up, a0, a_up, k_k, k_a, r_k, gn_w, gn_b):
    f32 = jnp.float32
    bsz, seq, _ = c_in.shape
    xs = token_shift(c_in.astype(f32), mu.astype(f32))
    r, k, v, wl, al = split_cols(xs, (RWKV_WIDTH, RWKV_WIDTH, RWKV_WIDTH, DECAY_LORA, ICLR_LORA))
    w_log = -jax.nn.softplus(-(w0.astype(f32) + jnp.tanh(wl) @ w_up.astype(f32))) - 0.5
    decay = jnp.exp(-jnp.exp(w_log))
    a = jax.nn.sigmoid(a0.astype(f32) + al @ a_up.astype(f32))
    hs = lambda t: t.reshape(bsz, seq, RWKV_HEADS, RWKV_HEAD)
    kk = l2norm(hs(k * k_k.astype(f32)))
    k = k * (1.0 + (a - 1.0) * k_a.astype(f32))
    r, k, v, decay, a = hs(r), hs(k), hs(v), hs(decay), hs(a)

    def step(state, inp):
        r_t, w_t, k_t, v_t, kk_t, a_t = inp
        sa = jnp.einsum('bhij,bhj->bhi', state, -kk_t)
        state = (state * w_t[:, :, None, :] + sa[..., None] * (kk_t * a_t)[:, :, None, :]
                 + v_t[..., None] * k_t[:, :, None, :])
        return state, jnp.einsum('bhij,bhj->bhi', state, r_t)

    seq_in = tuple(jnp.moveaxis(t, 1, 0) for t in (r, decay, k, v, kk, a))
    s0 = jnp.zeros((bsz, RWKV_HEADS, RWKV_HEAD, RWKV_HEAD), jnp.float32)
    _, y = lax.scan(step, s0, seq_in)
    y = jnp.moveaxis(y, 0, 1)
    mean = jnp.mean(y, axis=-1, keepdims=True)
    var = jnp.mean(jnp.square(y - mean), axis=-1, keepdims=True)
    y = ((y - mean) * lax.rsqrt(var + RWKV_GN_EPS)).reshape(bsz, seq, RWKV_WIDTH) * gn_w.astype(f32) + gn_b.astype(f32)
    bonus = jnp.sum(r * k * r_k.astype(f32), axis=-1, keepdims=True) * v
    return y + bonus.reshape(bsz, seq, RWKV_WIDTH)


def setup_inputs(seed: int = 0) -> dict:
    key = jax.random.key(seed)
    ks = iter(jax.random.split(key, 40))
    f32 = jnp.float32
    nrm = lambda shape, s: jax.random.normal(next(ks), shape, f32) * s
    uni = lambda shape, lo, hi: jax.random.uniform(next(ks), shape, f32, lo, hi)
    L = DEPTH
    x = jax.random.normal(next(ks), (BATCH, SEQ, D_MODEL), f32)
    norm_g = 1.0 + nrm((L, D_MODEL), 0.02)
    w_in = nrm((L, D_MODEL, N_IN), D_MODEL ** -0.5)
    w_out = nrm((L, D_MIX, D_MODEL), 0.5 * D_MIX ** -0.5)
    lru_conv_w = nrm((L, CONV_W, LRU_WIDTH), 0.5)
    lru_conv_b = nrm((L, LRU_WIDTH), 0.02)
    lru_wx = nrm((L, LRU_BLOCKS, LRU_BLOCK, LRU_BLOCK), LRU_BLOCK ** -0.5)
    lru_bx = nrm((L, LRU_WIDTH), 0.02)
    lru_wa = nrm((L, LRU_BLOCKS, LRU_BLOCK, LRU_BLOCK), LRU_BLOCK ** -0.5)
    lru_ba = nrm((L, LRU_WIDTH), 0.02)
    a_target = uni((L, LRU_WIDTH), 0.9, 0.999)
    sig = a_target ** (1.0 / LRU_C)
    lru_lambda = jnp.log(sig) - jnp.log1p(-sig)
    gdn_conv_w = nrm((L, CONV_W, 3 * GDN_WIDTH), 0.5)
    gdn_a_log = jnp.log(uni((L, GDN_HEADS), 1.0, 16.0))
    dt = jnp.exp(uni((L, GDN_HEADS), math.log(1e-3), math.log(1e-1)))
    gdn_dt_bias = dt + jnp.log(-jnp.expm1(-dt))
    gdn_norm_g = 1.0 + nrm((L, GDN_HEAD), 0.02)
    rwkv_mu = uni((L, RWKV_SHIFT_WIDTH), 0.1, 0.9)
    rwkv_w0 = uni((L, RWKV_WIDTH), -6.5, -1.5)
    rwkv_w_up = nrm((L, DECAY_LORA, RWKV_WIDTH), 0.1 * DECAY_LORA ** -0.5)
    rwkv_a0 = nrm((L, RWKV_WIDTH), 0.1)
    rwkv_a_up = nrm((L, ICLR_LORA, RWKV_WIDTH), 0.1 * ICLR_LORA ** -0.5)
    rwkv_k_k = 0.85 + nrm((L, RWKV_WIDTH), 0.02)
    rwkv_k_a = 1.0 + nrm((L, RWKV_WIDTH), 0.02)
    rwkv_r_k = nrm((L, RWKV_HEADS, RWKV_HEAD), 0.1)
    rwkv_gn_w = 1.0 + nrm((L, RWKV_WIDTH), 0.02)
    rwkv_gn_b = nrm((L, RWKV_WIDTH), 0.02)
    final_norm_g = 1.0 + nrm((D_MODEL,), 0.02)
    return {'x': x, 'norm_g': norm_g, 'w_in': w_in, 'w_out': w_out,
            'lru_conv_w': lru_conv_w, 'lru_conv_b': lru_conv_b, 'lru_wx': lru_wx, 'lru_bx': lru_bx,
            'lru_wa': lru_wa, 'lru_ba': lru_ba, 'lru_lambda': lru_lambda,
            'gdn_conv_w': gdn_conv_w, 'gdn_a_log': gdn_a_log, 'gdn_dt_bias': gdn_dt_bias, 'gdn_norm_g': gdn_norm_g,
            'rwkv_mu': rwkv_mu, 'rwkv_w0': rwkv_w0, 'rwkv_w_up': rwkv_w_up, 'rwkv_a0': rwkv_a0,
            'rwkv_a_up': rwkv_a_up, 'rwkv_k_k': rwkv_k_k, 'rwkv_k_a': rwkv_k_a, 'rwkv_r_k': rwkv_r_k,
            'rwkv_gn_w': rwkv_gn_w, 'rwkv_gn_b': rwkv_gn_b, 'final_norm_g': final_norm_g}


def reference(x, norm_g, w_in, w_out, lru_conv_w, lru_conv_b, lru_wx, lru_bx, lru_wa, lru_ba, lru_lambda,
              gdn_conv_w, gdn_a_log, gdn_dt_bias, gdn_norm_g, rwkv_mu, rwkv_w0, rwkv_w_up, rwkv_a0,
              rwkv_a_up, rwkv_k_k, rwkv_k_a, rwkv_r_k, rwkv_gn_w, rwkv_gn_b, final_norm_g):
    dt = x.dtype
    for l in range(DEPTH):
        h = rmsnorm(x, norm_g[l])
        proj = h @ w_in[l]
        a_x, a_g, b_qkv, b_g, b_beta, b_a, c_in, c_g = split_cols(proj, IN_SIZES)
        y_a = rg_lru(a_x, lru_conv_w[l], lru_conv_b[l], lru_wx[l], lru_bx[l], lru_wa[l], lru_ba[l], lru_lambda[l])
        y_b = gated_deltanet(b_qkv, b_beta, b_a, gdn_conv_w[l], gdn_a_log[l], gdn_dt_bias[l], gdn_norm_g[l])
        y_c = rwkv7_time_mix(c_in, rwkv_mu[l], rwkv_w0[l], rwkv_w_up[l], rwkv_a0[l], rwkv_a_up[l],
                             rwkv_k_k[l], rwkv_k_a[l], rwkv_r_k[l], rwkv_gn_w[l], rwkv_gn_b[l])
        y = jnp.concatenate([y_a * jax.nn.silu(a_g.astype(jnp.float32)),
                             y_b * jax.nn.silu(b_g.astype(jnp.float32)),
                             y_c * jax.nn.silu(c_g.astype(jnp.float32))], axis=-1).astype(dt)
        x = x + y @ w_out[l]
    return rmsnorm(x, final_norm_g)
```

```python
import functools

import jax
import jax.numpy as jnp
from jax import lax
from jax.experimental import pallas as pl
from jax.experimental.pallas import tpu as pltpu

F32 = jnp.float32
BF16 = jnp.bfloat16

NORM_EPS = 1e-6
CONV_W = 4
CHUNK = 64
LRU_WIDTH = 512
LRU_BLOCKS = 8
LRU_C = 8.0
GDN_HEAD = 128
GDN_HEADS = 6
GDN_WIDTH = GDN_HEADS * GDN_HEAD
RWKV_HEAD = 64
RWKV_WIDTH = 768
RWKV_PAIRS = RWKV_WIDTH // 128
LORA = 96
RWKV_GN_EPS = 64e-5

LANE = 128
N_PACKED = 7680
COL_BQ, COL_BK, COL_BV, COL_BG, COL_BBA = 8, 14, 20, 26, 32
COL_CR, COL_CK, COL_CV, COL_CWL, COL_CAL, COL_CG = 33, 39, 45, 51, 52, 53

VMEM_LIMIT = 52 * 1024 * 1024

_HI = lax.Precision.HIGHEST


def _dot(a, b):
    return jnp.dot(a.astype(BF16), b.astype(BF16), preferred_element_type=F32)


def _dot_nt(a, b):
    return lax.dot_general(a.astype(BF16), b.astype(BF16), (((1,), (1,)), ((), ())),
                           preferred_element_type=F32)


def _dot_tn(a, b):
    return lax.dot_general(a.astype(BF16), b.astype(BF16), (((0,), (0,)), ((), ())),
                           preferred_element_type=F32)


def _dot_hi(a, b):
    return jnp.dot(a, b, precision=_HI, preferred_element_type=F32)


def _dot_nt_hi(a, b):
    return lax.dot_general(a, b, (((1,), (1,)), ((), ())), precision=_HI,
                           preferred_element_type=F32)


def _softplus(x):
    return jnp.maximum(x, 0.0) + jnp.log1p(jnp.exp(-jnp.abs(x)))


def _silu(x):
    return x * jax.nn.sigmoid(x)


def _tri_inverse(n_mat, eye):
    p = eye + n_mat
    nk = n_mat
    for _ in range(5):
        nk = _dot_hi(nk, nk)
        p = p + _dot_hi(p, nk)
    return p


def _chunk_masks():
    row = lax.broadcasted_iota(jnp.int32, (CHUNK, CHUNK), 0)
    col = lax.broadcasted_iota(jnp.int32, (CHUNK, CHUNK), 1)
    return row >= col, row > col, (row == col).astype(F32)


def _inproj_kernel(x_ref, g_ref, w_ref, o_ref, h_ref):
    @pl.when(pl.program_id(1) == 0)
    def _():
        x = x_ref[...]
        ms = jnp.mean(x * x, axis=-1, keepdims=True)
        h_ref[...] = (x * lax.rsqrt(ms + NORM_EPS) * g_ref[...]).astype(BF16)

    o_ref[...] = jnp.dot(h_ref[...], w_ref[...], preferred_element_type=F32)


def _inproj(x2, g, w, tm=1024, tn=768):
    m, d = x2.shape
    n = w.shape[1]
    tm = min(tm, m)
    return pl.pallas_call(
        _inproj_kernel,
        grid=(m // tm, n // tn),
        in_specs=[pl.BlockSpec((tm, d), lambda i, j: (i, 0)),
                  pl.BlockSpec((1, d), lambda i, j: (0, 0)),
                  pl.BlockSpec((d, tn), lambda i, j: (0, j))],
        out_specs=pl.BlockSpec((tm, tn), lambda i, j: (i, j)),
        out_shape=jax.ShapeDtypeStruct((m, n), F32),
        scratch_shapes=[pltpu.VMEM((tm, d), BF16)],
        compiler_params=pltpu.CompilerParams(
            dimension_semantics=("parallel", "arbitrary"), vmem_limit_bytes=VMEM_LIMIT),
    )(x2, g, w)


def _outproj_kernel(ya_ref, yb_ref, yc_ref, x_ref, w_ref, fg_ref, o_ref, *, final_norm):
    wa = LRU_WIDTH
    wb = LRU_WIDTH + GDN_WIDTH
    acc = jnp.dot(ya_ref[...], w_ref[0:wa, :], preferred_element_type=F32)
    acc += jnp.dot(yb_ref[...], w_ref[wa:wb, :], preferred_element_type=F32)
    acc += jnp.dot(yc_ref[...], w_ref[wb:, :], preferred_element_type=F32)
    xn = x_ref[...] + acc
    if final_norm:
        ms = jnp.mean(xn * xn, axis=-1, keepdims=True)
        xn = xn * lax.rsqrt(ms + NORM_EPS) * fg_ref[...]
    o_ref[...] = xn


def _outproj(ya, yb, yc, x2, w, fg, final_norm, tm=512):
    m, d = x2.shape
    tm = min(tm, m)
    return pl.pallas_call(
        functools.partial(_outproj_kernel, final_norm=final_norm),
        grid=(m // tm,),
        in_specs=[pl.BlockSpec((tm, LRU_WIDTH), lambda i: (i, 0)),
                  pl.BlockSpec((tm, GDN_WIDTH), lambda i: (i, 0)),
                  pl.BlockSpec((tm, RWKV_WIDTH), lambda i: (i, 0)),
                  pl.BlockSpec((tm, d), lambda i: (i, 0)),
                  pl.BlockSpec((d, d), lambda i: (0, 0)),
                  pl.BlockSpec((1, d), lambda i: (0, 0))],
        out_specs=pl.BlockSpec((tm, d), lambda i: (i, 0)),
        out_shape=jax.ShapeDtypeStruct((m, d), F32),
        compiler_params=pltpu.CompilerParams(
            dimension_semantics=("parallel",), vmem_limit_bytes=VMEM_LIMIT),
    )(ya, yb, yc, x2, w, fg)


def _lru_kernel(x_ref, g_ref, cw_ref, cb_ref, wx_ref, bx_ref, wa_ref, ba_ref, lam_ref,
                y_ref, ext_ref, h_ref):
    ts, w = x_ref.shape

    @pl.when(pl.program_id(1) == 0)
    def _():
        ext_ref[0:8, :] = jnp.zeros((8, w), F32)
        h_ref[...] = jnp.zeros_like(h_ref)

    ext_ref[8:, :] = x_ref[...]
    cw = cw_ref[...]
    xc = cb_ref[...] + cw[0:1, :] * ext_ref[pl.ds(5, ts), :]
    for j in range(1, CONV_W):
        xc = xc + cw[j:j + 1, :] * ext_ref[pl.ds(5 + j, ts), :]
    ext_ref[0:8, :] = ext_ref[pl.ds(ts, 8), :]

    gate_x = jax.nn.sigmoid(_dot(xc, wx_ref[...]) + bx_ref[...])
    gate_a = jax.nn.sigmoid(_dot(xc, wa_ref[...]) + ba_ref[...])
    log_a = -LRU_C * gate_a * _softplus(-lam_ref[...])
    a = jnp.exp(log_a)
    mult = jnp.sqrt(jnp.maximum(-jnp.tanh(log_a) * (a * a + 1.0), 0.0))
    u = mult * (gate_x * xc)

    row = lax.broadcasted_iota(jnp.int32, (ts, w), 0)
    d = 1
    while d < ts:
        keep = row >= d
        u = jnp.where(keep, a * pltpu.roll(u, d, axis=0) + u, u)
        a = jnp.where(keep, a * pltpu.roll(a, d, axis=0), a)
        d *= 2
    h = u + a * h_ref[0:1, :]
    h_ref[0:1, :] = h[ts - 1:ts, :]
    y_ref[...] = (h * _silu(g_ref[...])).astype(y_ref.dtype)


def _lru(proj, bsz, seq, cw, cb, wx, bx, wa, ba, lam, ts=256):
    ns = seq // ts
    w = LRU_WIDTH
    vec = lambda: pl.BlockSpec((1, w), lambda b, s: (0, 0))
    return pl.pallas_call(
        _lru_kernel,
        grid=(bsz, ns),
        in_specs=[pl.BlockSpec((ts, w), lambda b, s: (b * ns + s, 0)),
                  pl.BlockSpec((ts, w), lambda b, s: (b * ns + s, 1)),
                  pl.BlockSpec((CONV_W, w), lambda b, s: (0, 0)),
                  vec(),
                  pl.BlockSpec((w, w), lambda b, s: (0, 0)),
                  vec(),
                  pl.BlockSpec((w, w), lambda b, s: (0, 0)),
                  vec(), vec()],
        out_specs=pl.BlockSpec((ts, w), lambda b, s: (b * ns + s, 0)),
        out_shape=jax.ShapeDtypeStruct((bsz * seq, w), BF16),
        scratch_shapes=[pltpu.VMEM((ts + 8, w), F32), pltpu.VMEM((8, w), F32)],
        compiler_params=pltpu.CompilerParams(
            dimension_semantics=("parallel", "arbitrary"), vmem_limit_bytes=VMEM_LIMIT),
    )(proj, proj, cw, cb, wx, bx, wa, ba, lam)


def _gdn_kernel(q_ref, k_ref, v_ref, gate_ref, ba_ref, cwq_ref, cwk_ref, cwv_ref, gp_ref, ng_ref,
                y_ref, ext_ref, qs_ref, ks_ref, vs_ref, bt_ref, gg_ref, u_ref, w_ref, qd_ref,
                kd_ref, qk_ref, gl_ref, o_ref):
    head = pl.program_id(1)
    seq, dh = q_ref.shape
    c = CHUNK
    n_chunks = seq // c

    def conv_silu(x_ref, cw_ref):
        ext_ref[0:8, :] = jnp.zeros((8, dh), F32)
        ext_ref[8:, :] = x_ref[...]
        cw = cw_ref[...]
        xc = cw[0:1, :] * ext_ref[pl.ds(5, seq), :]
        for j in range(1, CONV_W):
            xc = xc + cw[j:j + 1, :] * ext_ref[pl.ds(5 + j, seq), :]
        return _silu(xc)

    def l2norm(t):
        return t * lax.rsqrt(jnp.sum(t * t, axis=-1, keepdims=True) + 1e-6)

    qs_ref[...] = l2norm(conv_silu(q_ref, cwq_ref)) * (GDN_HEAD ** -0.5)
    ks_ref[...] = l2norm(conv_silu(k_ref, cwk_ref))
    vs_ref[...] = conv_silu(v_ref, cwv_ref)

    ba = ba_ref[...]
    lane = lax.broadcasted_iota(jnp.int32, ba.shape, 1)
    gp = gp_ref[...]
    beta_all = jax.nn.sigmoid(ba)
    g_all = -jnp.exp(gp[0:1, :]) * _softplus(ba + gp[1:2, :])
    beta_col = jnp.sum(jnp.where(lane == head, beta_all, 0.0), axis=1, keepdims=True)
    g_col = jnp.sum(jnp.where(lane == head + GDN_HEADS, g_all, 0.0), axis=1, keepdims=True)
    bt_ref[...] = jnp.broadcast_to(beta_col, (seq, dh))
    gg_ref[...] = jnp.broadcast_to(g_col, (seq, dh))

    incl, strict, eye = _chunk_masks()
    ltri = incl.astype(F32)
    pick0 = (lax.broadcasted_iota(jnp.int32, (c, dh), 1) == 0).astype(F32)

    def prepare(n, carry):
        sl = pl.ds(pl.multiple_of(n * c, c), c)
        q = qs_ref[sl, :]
        k = ks_ref[sl, :]
        v = vs_ref[sl, :]
        beta = bt_ref[sl, :]
        dec = _dot_hi(ltri, gg_ref[sl, :])
        dcol = dec[:, 0:c]
        drow = _dot_nt_hi(pick0, dec)
        lmask = jnp.where(incl, jnp.exp(jnp.where(incl, dcol - drow, 0.0)), 0.0)
        kb = k * beta
        a_mat = jnp.where(strict, _dot_nt(kb, k) * lmask, 0.0)
        t_mat = _tri_inverse(-a_mat, eye)
        edec = jnp.exp(dec)
        dlast = dec[c - 1:c, :]
        u_ref[sl, :] = _dot(t_mat, v * beta)
        w_ref[sl, :] = _dot(t_mat, kb * edec)
        qk_ref[sl, :] = jnp.where(incl, _dot_nt(q, k) * lmask, 0.0)
        qd_ref[sl, :] = q * edec
        kd_ref[sl, :] = k * jnp.exp(dlast - dec)
        gl_ref[pl.ds(pl.multiple_of(n * 8, 8), 8), :] = jnp.broadcast_to(jnp.exp(dlast), (8, dh))
        return carry

    lax.fori_loop(0, n_chunks, prepare, 0)

    def scan(n, state):
        sl = pl.ds(pl.multiple_of(n * c, c), c)
        v_new = u_ref[sl, :] - _dot(w_ref[sl, :], state)
        o_ref[sl, :] = _dot(qd_ref[sl, :], state) + _dot(qk_ref[sl, :], v_new)
        gl = gl_ref[pl.ds(pl.multiple_of(n * 8, 8), 8), :][0:1, :]
        return state * gl + _dot_tn(kd_ref[sl, :], v_new)

    lax.fori_loop(0, n_chunks, scan, jnp.zeros((dh, dh), F32))

    o = o_ref[...]
    o = o * lax.rsqrt(jnp.mean(o * o, axis=-1, keepdims=True) + NORM_EPS) * ng_ref[...]
    y_ref[...] = (o * _silu(gate_ref[...])).astype(y_ref.dtype)


def _gdn(proj, bsz, seq, cw, gp, ng):
    dh = GDN_HEAD
    col = lambda base: pl.BlockSpec((seq, dh), lambda b, h: (b, base + h))
    cwspec = lambda base: pl.BlockSpec((CONV_W, dh), lambda b, h: (0, base + h))
    full = lambda: pltpu.VMEM((seq, dh), F32)
    return pl.pallas_call(
        _gdn_kernel,
        grid=(bsz, GDN_HEADS),
        in_specs=[col(COL_BQ), col(COL_BK), col(COL_BV), col(COL_BG),
                  pl.BlockSpec((seq, dh), lambda b, h: (b, COL_BBA)),
                  cwspec(0), cwspec(GDN_HEADS), cwspec(2 * GDN_HEADS),
                  pl.BlockSpec((2, dh), lambda b, h: (0, 0)),
                  pl.BlockSpec((1, dh), lambda b, h: (0, 0))],
        out_specs=pl.BlockSpec((seq, dh), lambda b, h: (b, h)),
        out_shape=jax.ShapeDtypeStruct((bsz * seq, GDN_WIDTH), BF16),
        scratch_shapes=[pltpu.VMEM((seq + 8, dh), F32),
                        full(), full(), full(), full(), full(), full(), full(), full(), full(),
                        pltpu.VMEM((seq, CHUNK), F32),
                        pltpu.VMEM((seq // CHUNK * 8, dh), F32),
                        full()],
        compiler_params=pltpu.CompilerParams(
            dimension_semantics=("parallel", "parallel"), vmem_limit_bytes=VMEM_LIMIT),
    )(proj, proj, proj, proj, proj, cw, cw, cw, gp, ng)


def _rwkv_kernel(r_ref, k_ref, v_ref, wl_ref, al_ref, gate_ref,
                 mur_ref, muk_ref, muv_ref, muwl_ref, mual_ref,
                 w0_ref, wup_ref, a0_ref, aup_ref, kk_ref, ka_ref, rk_ref, gnw_ref, gnb_ref,
                 y_ref,
                 rs_ref, k2_ref, vs_ref, kn_ref, kna_ref, lw_ref,
                 wt_ref, u0_ref, o0_ref, rt_ref, mrb_ref, b2_ref, kv_ref, pc_ref, o_ref):
    seq, dw = r_ref.shape
    c = CHUNK
    hd = RWKV_HEAD
    n_chunks = seq // c

    row0 = lax.broadcasted_iota(jnp.int32, (seq, dw), 0) == 0

    def shift(x_ref, mu_ref):
        x = x_ref[...]
        prev = jnp.where(row0, 0.0, pltpu.roll(x, 1, axis=0))
        return x + (prev - x) * mu_ref[...]

    r = shift(r_ref, mur_ref)
    k = shift(k_ref, muk_ref)
    v = shift(v_ref, muv_ref)
    wl = shift(wl_ref, muwl_ref)
    al = shift(al_ref, mual_ref)

    seg_r = lax.broadcasted_iota(jnp.int32, (dw, dw), 0) // hd
    seg_c = lax.broadcasted_iota(jnp.int32, (dw, dw), 1) // hd
    seg = (seg_r == seg_c).astype(F32)

    w_log = -_softplus(-(w0_ref[...] + _dot(jnp.tanh(wl), wup_ref[...]))) - 0.5
    lw_ref[...] = -jnp.exp(w_log)
    a = jax.nn.sigmoid(a0_ref[...] + _dot(al, aup_ref[...]))
    kn = k * kk_ref[...]
    kn = kn * lax.rsqrt(_dot_hi(kn * kn, seg) + 1e-6)
    k2 = k * (1.0 + (a - 1.0) * ka_ref[...])
    rs_ref[...] = r
    k2_ref[...] = k2
    vs_ref[...] = v
    kn_ref[...] = kn
    kna_ref[...] = kn * a

    incl, strict, eye = _chunk_masks()
    ltri = incl.astype(F32)

    def prepare(n, carry):
        sl = pl.ds(pl.multiple_of(n * c, c), c)
        lw = lw_ref[sl, :]
        cum = _dot_hi(ltri, lw)
        clast = cum[c - 1:c, :]
        a_t = -kn_ref[sl, :] * jnp.exp(cum - lw)
        b_t = kna_ref[sl, :] * jnp.exp(-cum)
        k_t = k2_ref[sl, :] * jnp.exp(-cum)
        r_t = rs_ref[sl, :] * jnp.exp(cum)
        tail = jnp.exp(clast - cum)
        b_2 = kna_ref[sl, :] * tail
        k_2 = k2_ref[sl, :] * tail
        pc = jnp.broadcast_to(jnp.exp(clast), (c, dw))
        vv = vs_ref[sl, :]
        for hh in range(dw // hd):
            ls = slice(hh * hd, (hh + 1) * hd)
            ah, bh, kh, rh, vh = a_t[:, ls], b_t[:, ls], k_t[:, ls], r_t[:, ls], vv[:, ls]
            a_ab = jnp.where(strict, _dot_nt(ah, bh), 0.0)
            a_ak = jnp.where(strict, _dot_nt(ah, kh), 0.0)
            m_rb = jnp.where(incl, _dot_nt(rh, bh), 0.0)
            m_rk = jnp.where(incl, _dot_nt(rh, kh), 0.0)
            t_mat = _tri_inverse(a_ab, eye)
            wt_ref[hh, sl, :] = _dot(t_mat, ah)
            u0_ref[hh, sl, :] = _dot(t_mat, _dot(a_ak, vh))
            o0_ref[hh, sl, :] = _dot(m_rk, vh)
            rt_ref[hh, sl, :] = rh
            mrb_ref[hh, sl, :] = m_rb
            b2_ref[hh, sl, :] = b_2[:, ls]
            kv_ref[hh, sl, :] = _dot_tn(k_2[:, ls], vh)
            pc_ref[hh, sl, :] = pc[:, ls].T
        return carry

    lax.fori_loop(0, n_chunks, prepare, 0)

    def scan(n, states):
        sl = pl.ds(pl.multiple_of(n * c, c), c)
        new_states = []
        for hh in range(dw // hd):
            hm = states[hh]
            u = u0_ref[hh, sl, :] + _dot(wt_ref[hh, sl, :], hm)
            o = _dot(rt_ref[hh, sl, :], hm) + _dot(mrb_ref[hh, sl, :], u) + o0_ref[hh, sl, :]
            o_ref[sl, hh * hd:(hh + 1) * hd] = o
            new_states.append(pc_ref[hh, sl, :] * hm + _dot_tn(b2_ref[hh, sl, :], u) + kv_ref[hh, sl, :])
        return tuple(new_states)

    lax.fori_loop(0, n_chunks, scan, tuple(jnp.zeros((hd, hd), F32) for _ in range(dw // hd)))

    o = o_ref[...]
    mean = _dot_hi(o, seg) * (1.0 / hd)
    cen = o - mean
    var = _dot_hi(cen * cen, seg) * (1.0 / hd)
    yn = cen * lax.rsqrt(var + RWKV_GN_EPS) * gnw_ref[...] + gnb_ref[...]
    bonus = _dot_hi(rs_ref[...] * k2_ref[...] * rk_ref[...], seg) * vs_ref[...]
    y_ref[...] = ((yn + bonus) * _silu(gate_ref[...])).astype(y_ref.dtype)


def _rwkv(proj, bsz, seq, mu, w0, wup, a0, aup, kk, ka, rk, gnw, gnb):
    dw = LANE
    hd = RWKV_HEAD
    col = lambda base: pl.BlockSpec((seq, dw), lambda b, p: (b, base + p))
    fixed_col = lambda idx: pl.BlockSpec((seq, dw), lambda b, p: (b, idx))
    vec = lambda base: pl.BlockSpec((1, dw), lambda b, p: (0, base + p))
    fixed_vec = lambda idx: pl.BlockSpec((1, dw), lambda b, p: (0, idx))
    lora = lambda: pl.BlockSpec((dw, dw), lambda b, p: (0, p))
    full = lambda: pltpu.VMEM((seq, dw), F32)
    half = lambda: pltpu.VMEM((dw // hd, seq, hd), F32)
    np_ = RWKV_PAIRS
    return pl.pallas_call(
        _rwkv_kernel,
        grid=(bsz, np_),
        in_specs=[col(COL_CR), col(COL_CK), col(COL_CV), fixed_col(COL_CWL), fixed_col(COL_CAL),
                  col(COL_CG),
                  vec(0), vec(np_), vec(2 * np_), fixed_vec(3 * np_), fixed_vec(3 * np_ + 1),
                  vec(0), lora(), vec(0), lora(), vec(0), vec(0), vec(0), vec(0), vec(0)],
        out_specs=pl.BlockSpec((seq, dw), lambda b, p: (b, p)),
        out_shape=jax.ShapeDtypeStruct((bsz * seq, RWKV_WIDTH), BF16),
        scratch_shapes=[full(), full(), full(), full(), full(), full(),
                        half(), half(), half(), half(), half(), half(), half(), half(),
                        full()],
        compiler_params=pltpu.CompilerParams(
            dimension_semantics=("parallel", "parallel"), vmem_limit_bytes=VMEM_LIMIT),
    )(proj, proj, proj, proj, proj, proj, mu, mu, mu, mu, mu,
      w0, wup, a0, aup, kk, ka, rk, gnw, gnb)


def _pack_w_in(w_in):
    lyr, d, _ = w_in.shape
    w = w_in.astype(BF16)
    z = lambda n: jnp.zeros((lyr, d, n), BF16)
    c0 = 2 * LRU_WIDTH + 4 * GDN_WIDTH + 2 * GDN_HEADS
    c1 = c0 + 3 * RWKV_WIDTH
    parts = [w[..., :c0], z(COL_CR * LANE - c0),
             w[..., c0:c1], w[..., c1:c1 + LORA], z(LANE - LORA),
             w[..., c1 + LORA:c1 + 2 * LORA], z(LANE - LORA),
             w[..., c1 + 2 * LORA:]]
    packed = jnp.concatenate(parts, axis=-1)
    return jnp.concatenate([packed, z(N_PACKED - packed.shape[-1])], axis=-1)


def _pack_mu(mu):
    lyr = mu.shape[0]
    z = jnp.zeros((lyr, LANE - LORA), mu.dtype)
    c1 = 3 * RWKV_WIDTH
    return jnp.concatenate([mu[:, :c1], mu[:, c1:c1 + LORA], z, mu[:, c1 + LORA:], z], axis=-1)


def _block_diag(w):
    lyr, nb, bi, bj = w.shape
    eye = jnp.eye(nb, dtype=w.dtype)
    return jnp.einsum('lnij,nm->lnimj', w, eye).reshape(lyr, nb * bi, nb * bj)


def kernel(x, norm_g, w_in, w_out, lru_conv_w, lru_conv_b, lru_wx, lru_bx, lru_wa, lru_ba, lru_lambda,
           gdn_conv_w, gdn_a_log, gdn_dt_bias, gdn_norm_g, rwkv_mu, rwkv_w0, rwkv_w_up, rwkv_a0,
           rwkv_a_up, rwkv_k_k, rwkv_k_a, rwkv_r_k, rwkv_gn_w, rwkv_gn_b, final_norm_g):
    bsz, seq, d = x.shape
    depth = w_in.shape[0]
    x2 = x.reshape(bsz * seq, d)

    w_in_p = _pack_w_in(w_in)
    w_out_b = w_out.astype(BF16)
    wx_bd = _block_diag(lru_wx).astype(BF16)
    wa_bd = _block_diag(lru_wa).astype(BF16)
    mu_p = _pack_mu(rwkv_mu)
    pad_rows = lambda w: jnp.pad(w, ((0, 0), (0, LANE - LORA), (0, 0))).astype(BF16)
    wup_p = pad_rows(rwkv_w_up)
    aup_p = pad_rows(rwkv_a_up)
    gp = jnp.zeros((depth, 2, LANE), F32)
    gp = gp.at[:, 0, GDN_HEADS:2 * GDN_HEADS].set(gdn_a_log)
    gp = gp.at[:, 1, GDN_HEADS:2 * GDN_HEADS].set(gdn_dt_bias)
    row = lambda t: t.reshape(1, -1)

    for l in range(depth):
        proj = _inproj(x2, row(norm_g[l]), w_in_p[l])
        ya = _lru(proj, bsz, seq, lru_conv_w[l], row(lru_conv_b[l]), wx_bd[l], row(lru_bx[l]),
                  wa_bd[l], row(lru_ba[l]), row(lru_lambda[l]))
        yb = _gdn(proj, bsz, seq, gdn_conv_w[l], gp[l], row(gdn_norm_g[l]))
        yc = _rwkv(proj, bsz, seq, row(mu_p[l]), row(rwkv_w0[l]), wup_p[l], row(rwkv_a0[l]), aup_p[l],
                   row(rwkv_k_k[l]), row(rwkv_k_a[l]), row(rwkv_r_k[l]), row(rwkv_gn_w[l]),
                   row(rwkv_gn_b[l]))
        x2 = _outproj(ya, yb, yc, x2, w_out_b[l], row(final_norm_g), final_norm=(l == depth - 1))
    return x2.reshape(bsz, seq, d)
```

```python
import functools

import jax
import jax.numpy as jnp
from jax import lax
from jax.experimental import pallas as pl
from jax.experimental.pallas import tpu as pltpu

F32 = jnp.float32
BF16 = jnp.bfloat16

NORM_EPS = 1e-6
CONV_W = 4
CHUNK = 64
LRU_WIDTH = 512
LRU_C = 8.0
GDN_HEAD = 128
GDN_HEADS = 6
GDN_WIDTH = GDN_HEADS * GDN_HEAD
RWKV_HEAD = 64
RWKV_WIDTH = 768
LORA = 96
RWKV_GN_EPS = 64e-5

LANE = 128
RWKV_PAIRS = RWKV_WIDTH // LANE
N_PACKED = 7680
COL_BQ, COL_BK, COL_BV, COL_BG = 0, 6, 12, 18
COL_CR, COL_CK, COL_CV, COL_CG = 24, 30, 36, 42
COL_AX, COL_AG = 48, 52
COL_BBA, COL_CWL, COL_CAL = 56, 57, 58
SCAN_ROWS = 512

VMEM_LIMIT = 52 * 1024 * 1024


def _dot(a, b):
    return jnp.dot(a.astype(BF16), b.astype(BF16), preferred_element_type=F32)


def _dot_nt(a, b):
    return lax.dot_general(a.astype(BF16), b.astype(BF16), (((1,), (1,)), ((), ())),
                           preferred_element_type=F32)


def _dot_tn(a, b):
    return lax.dot_general(a.astype(BF16), b.astype(BF16), (((0,), (0,)), ((), ())),
                           preferred_element_type=F32)


def _split2(x):
    hi = x.astype(BF16)
    lo = (x - hi.astype(F32)).astype(BF16)
    return hi, lo


def _split3(x):
    hi = x.astype(BF16)
    r1 = x - hi.astype(F32)
    mid = r1.astype(BF16)
    lo = (r1 - mid.astype(F32)).astype(BF16)
    return hi, mid, lo


def _cumsum_rows(ltri3, x):
    return jnp.dot(ltri3, jnp.concatenate(_split3(x), axis=0), preferred_element_type=F32)


def _segsum(x, seg2):
    return jnp.dot(jnp.concatenate(_split2(x), axis=1), seg2, preferred_element_type=F32)


def _stack_lhs(a):
    ah, al = _split2(jnp.concatenate([a, a], axis=1))
    return jnp.concatenate([ah, al], axis=1)


def _stack_rhs(b):
    bh, bl = _split2(b)
    return jnp.concatenate([bh, bl, bh, bl], axis=0)


def _tri_inverse(n_mats, eye):
    ps = [eye + n for n in n_mats]
    nks = list(n_mats)
    rhss = [_stack_rhs(n) for n in nks]
    for _ in range(5):
        nks = [jnp.dot(_stack_lhs(n), r, preferred_element_type=F32) for n, r in zip(nks, rhss)]
        rhss = [_stack_rhs(n) for n in nks]
        ps = [p + jnp.dot(_stack_lhs(p), r, preferred_element_type=F32) for p, r in zip(ps, rhss)]
    return ps


def _chunk_masks():
    row = lax.broadcasted_iota(jnp.int32, (CHUNK, CHUNK), 0)
    col = lax.broadcasted_iota(jnp.int32, (CHUNK, CHUNK), 1)
    return row >= col, row > col, (row == col).astype(F32)


def _softplus(x):
    return jnp.maximum(x, 0.0) + jnp.log1p(jnp.exp(-jnp.abs(x)))


def _silu(x):
    return x * jax.nn.sigmoid(x)


def _inproj_kernel(x_ref, g_ref, w_ref, o_ref, h_ref):
    @pl.when(pl.program_id(1) == 0)
    def _():
        x = x_ref[...]
        ms = jnp.mean(x * x, axis=-1, keepdims=True)
        h_ref[...] = (x * lax.rsqrt(ms + NORM_EPS) * g_ref[...]).astype(BF16)

    o_ref[...] = jnp.dot(h_ref[...], w_ref[...], preferred_element_type=F32)


def _inproj(x2, g, w, tm=1024, tn=768):
    m, d = x2.shape
    n = w.shape[1]
    tm = min(tm, m)
    return pl.pallas_call(
        _inproj_kernel,
        grid=(m // tm, n // tn),
        in_specs=[pl.BlockSpec((tm, d), lambda i, j: (i, 0)),
                  pl.BlockSpec((1, d), lambda i, j: (0, 0)),
                  pl.BlockSpec((d, tn), lambda i, j: (0, j))],
        out_specs=pl.BlockSpec((tm, tn), lambda i, j: (i, j)),
        out_shape=jax.ShapeDtypeStruct((m, n), F32),
        scratch_shapes=[pltpu.VMEM((tm, d), BF16)],
        compiler_params=pltpu.CompilerParams(
            dimension_semantics=("parallel", "arbitrary"), vmem_limit_bytes=VMEM_LIMIT),
        name="inproj",
    )(x2, g, w)


def _outproj_kernel(ya_ref, yb_ref, yc_ref, x_ref, w_ref, fg_ref, o_ref, *, final_norm):
    wa = LRU_WIDTH
    wb = LRU_WIDTH + GDN_WIDTH
    acc = jnp.dot(ya_ref[...], w_ref[0:wa, :], preferred_element_type=F32)
    acc += jnp.dot(yb_ref[...], w_ref[wa:wb, :], preferred_element_type=F32)
    acc += jnp.dot(yc_ref[...], w_ref[wb:, :], preferred_element_type=F32)
    xn = x_ref[...] + acc
    if final_norm:
        ms = jnp.mean(xn * xn, axis=-1, keepdims=True)
        xn = xn * lax.rsqrt(ms + NORM_EPS) * fg_ref[...]
    o_ref[...] = xn


def _outproj(ya, yb, yc, x2, w, fg, final_norm, tm=512):
    m, d = x2.shape
    tm = min(tm, m)
    return pl.pallas_call(
        functools.partial(_outproj_kernel, final_norm=final_norm),
        grid=(m // tm,),
        in_specs=[pl.BlockSpec((tm, LRU_WIDTH), lambda i: (i, 0)),
                  pl.BlockSpec((tm, GDN_WIDTH), lambda i: (i, 0)),
                  pl.BlockSpec((tm, RWKV_WIDTH), lambda i: (i, 0)),
                  pl.BlockSpec((tm, d), lambda i: (i, 0)),
                  pl.BlockSpec((d, d), lambda i: (0, 0)),
                  pl.BlockSpec((1, d), lambda i: (0, 0))],
        out_specs=pl.BlockSpec((tm, d), lambda i: (i, 0)),
        out_shape=jax.ShapeDtypeStruct((m, d), F32),
        compiler_params=pltpu.CompilerParams(
            dimension_semantics=("parallel",), vmem_limit_bytes=VMEM_LIMIT),
        name="outproj",
    )(ya, yb, yc, x2, w, fg)


def _lru_kernel(x_ref, g_ref, cw_ref, cb_ref, wx_ref, bx_ref, wa_ref, ba_ref, lam_ref,
                y_ref, ext_ref, h_ref):
    ts, w = x_ref.shape

    @pl.when(pl.program_id(1) == 0)
    def _():
        ext_ref[0:8, :] = jnp.zeros((8, w), F32)
        h_ref[...] = jnp.zeros_like(h_ref)

    ext_ref[8:, :] = x_ref[...]
    cw = cw_ref[...]
    xc = cb_ref[...] + cw[0:1, :] * ext_ref[pl.ds(5, ts), :]
    for j in range(1, CONV_W):
        xc = xc + cw[j:j + 1, :] * ext_ref[pl.ds(5 + j, ts), :]
    ext_ref[0:8, :] = ext_ref[pl.ds(ts, 8), :]

    gate_x = jax.nn.sigmoid(_dot(xc, wx_ref[...]) + bx_ref[...])
    gate_a = jax.nn.sigmoid(_dot(xc, wa_ref[...]) + ba_ref[...])
    log_a = -LRU_C * gate_a * _softplus(-lam_ref[...])
    a = jnp.exp(log_a)
    mult = jnp.sqrt(jnp.maximum(-jnp.tanh(log_a) * (a * a + 1.0), 0.0))
    u = mult * (gate_x * xc)

    row = lax.broadcasted_iota(jnp.int32, (ts, w), 0)
    d = 1
    while d < ts:
        keep = row >= d
        u = jnp.where(keep, a * pltpu.roll(u, d, axis=0) + u, u)
        a = jnp.where(keep, a * pltpu.roll(a, d, axis=0), a)
        d *= 2
    h = u + a * h_ref[0:1, :]
    h_ref[0:1, :] = h[ts - 1:ts, :]
    y_ref[...] = (h * _silu(g_ref[...])).astype(y_ref.dtype)


def _lru(proj, bsz, seq, cw, cb, wx, bx, wa, ba, lam, ts=256):
    ns = seq // ts
    w = LRU_WIDTH
    vec = lambda: pl.BlockSpec((1, w), lambda b, s: (0, 0))
    return pl.pallas_call(
        _lru_kernel,
        grid=(bsz, ns),
        in_specs=[pl.BlockSpec((ts, w), lambda b, s: (b * ns + s, COL_AX * LANE // w)),
                  pl.BlockSpec((ts, w), lambda b, s: (b * ns + s, COL_AG * LANE // w)),
                  pl.BlockSpec((CONV_W, w), lambda b, s: (0, 0)),
                  vec(),
                  pl.BlockSpec((w, w), lambda b, s: (0, 0)),
                  vec(),
                  pl.BlockSpec((w, w), lambda b, s: (0, 0)),
                  vec(), vec()],
        out_specs=pl.BlockSpec((ts, w), lambda b, s: (b * ns + s, 0)),
        out_shape=jax.ShapeDtypeStruct((bsz * seq, w), BF16),
        scratch_shapes=[pltpu.VMEM((ts + 8, w), F32), pltpu.VMEM((8, w), F32)],
        compiler_params=pltpu.CompilerParams(
            dimension_semantics=("parallel", "arbitrary"), vmem_limit_bytes=VMEM_LIMIT),
        name="lru",
    )(proj, proj, cw, cb, wx, bx, wa, ba, lam)


def _gdn_prep_kernel(q_ref, k_ref, v_ref, ba_ref, cwq_ref, cwk_ref, cwv_ref, gp_ref,
                     u_ref, w_ref, qd_ref, kd_ref, qk_ref, gl_ref,
                     ext_ref, qs_ref, ks_ref, vs_ref, bt_ref, gg_ref, *, group):
    head = pl.program_id(1)
    seq, dh = q_ref.shape
    c = CHUNK
    n_chunks = seq // c

    def conv_silu(x_ref, cw_ref):
        ext_ref[0:8, :] = jnp.zeros((8, dh), F32)
        ext_ref[8:, :] = x_ref[...]
        cw = cw_ref[...]
        xc = cw[0:1, :] * ext_ref[pl.ds(5, seq), :]
        for j in range(1, CONV_W):
            xc = xc + cw[j:j + 1, :] * ext_ref[pl.ds(5 + j, seq), :]
        return _silu(xc)

    def l2norm(t):
        return t * lax.rsqrt(jnp.sum(t * t, axis=-1, keepdims=True) + 1e-6)

    qs_ref[...] = l2norm(conv_silu(q_ref, cwq_ref)) * (GDN_HEAD ** -0.5)
    ks_ref[...] = l2norm(conv_silu(k_ref, cwk_ref))
    vs_ref[...] = conv_silu(v_ref, cwv_ref)

    ba = ba_ref[...]
    lane = lax.broadcasted_iota(jnp.int32, ba.shape, 1)
    gp = gp_ref[...]
    beta_all = jax.nn.sigmoid(ba)
    g_all = -jnp.exp(gp[0:1, :]) * _softplus(ba + gp[1:2, :])
    beta_col = jnp.sum(jnp.where(lane == head, beta_all, 0.0), axis=1, keepdims=True)
    g_col = jnp.sum(jnp.where(lane == head + GDN_HEADS, g_all, 0.0), axis=1, keepdims=True)
    bt_ref[...] = jnp.broadcast_to(beta_col, (seq, dh))
    gg_ref[...] = jnp.broadcast_to(g_col, (seq, dh))

    incl, strict, eye = _chunk_masks()
    ltri = incl.astype(BF16)
    ltri3 = jnp.concatenate([ltri, ltri, ltri], axis=1)
    qk_pad = jnp.zeros((c, dh - c), F32)

    def prepare(n, carry):
        rng = range(group)
        sls = [pl.ds(pl.multiple_of((n * group + i) * c, c), c) for i in rng]
        qs = [qs_ref[s, :] for s in sls]
        ks = [ks_ref[s, :] for s in sls]
        betas = [bt_ref[s, :] for s in sls]
        decs = [_cumsum_rows(ltri3, gg_ref[s, :]) for s in sls]
        kbs = [k * b for k, b in zip(ks, betas)]
        grams = [_dot_nt(jnp.concatenate([kb, q], axis=0), k) for kb, q, k in zip(kbs, qs, ks)]
        lmasks = []
        for dec in decs:
            dcol = dec[:, 0:c]
            lmasks.append(jnp.where(incl, jnp.exp(jnp.where(incl, dcol - dcol.T, 0.0)), 0.0))
        a_mats = [jnp.where(strict, g[0:c, :] * m, 0.0) for g, m in zip(grams, lmasks)]
        t_mats = _tri_inverse([-a for a in a_mats], eye)
        edecs = [jnp.exp(dec) for dec in decs]
        uws = [_dot(t, jnp.concatenate([vs_ref[s, :] * b, kb * e], axis=1))
               for t, s, b, kb, e in zip(t_mats, sls, betas, kbs, edecs)]
        for i in rng:
            s = sls[i]
            dec = decs[i]
            dlast = dec[c - 1:c, :]
            qkm = jnp.where(incl, grams[i][c:, :] * lmasks[i], 0.0)
            u_ref[s, :] = uws[i][:, 0:dh]
            w_ref[s, :] = uws[i][:, dh:].astype(BF16)
            qk_ref[s, :] = jnp.concatenate([qkm, qk_pad], axis=1).astype(BF16)
            qd_ref[s, :] = (qs[i] * edecs[i]).astype(BF16)
            kd_ref[s, :] = (ks[i] * jnp.exp(dlast - dec)).astype(BF16)
            gl_ref[pl.ds(pl.multiple_of((n * group + i) * 8, 8), 8), :] = jnp.broadcast_to(
                jnp.exp(dlast), (8, dh))
        return carry

    lax.fori_loop(0, n_chunks // group, prepare, 0)


def _gdn_prep(proj, bsz, seq, cw, gp, group=8):
    dh = GDN_HEAD
    n8 = seq // CHUNK * 8
    col = lambda base: pl.BlockSpec((seq, dh), lambda b, h: (b, base + h))
    cwspec = lambda base: pl.BlockSpec((CONV_W, dh), lambda b, h: (0, base + h))
    out = lambda: pl.BlockSpec((seq, dh), lambda b, h: (b, h))
    full = lambda: pltpu.VMEM((seq, dh), F32)
    sds = lambda dt: jax.ShapeDtypeStruct((bsz * seq, GDN_WIDTH), dt)
    return pl.pallas_call(
        functools.partial(_gdn_prep_kernel, group=group),
        grid=(bsz, GDN_HEADS),
        in_specs=[col(COL_BQ), col(COL_BK), col(COL_BV),
                  pl.BlockSpec((seq, dh), lambda b, h: (b, COL_BBA)),
                  cwspec(0), cwspec(GDN_HEADS), cwspec(2 * GDN_HEADS),
                  pl.BlockSpec((2, dh), lambda b, h: (0, 0))],
        out_specs=[out(), out(), out(), out(), out(),
                   pl.BlockSpec((n8, dh), lambda b, h: (b, h))],
        out_shape=[sds(F32), sds(BF16), sds(BF16), sds(BF16), sds(BF16),
                   jax.ShapeDtypeStruct((bsz * n8, GDN_WIDTH), F32)],
        scratch_shapes=[pltpu.VMEM((seq + 8, dh), F32), full(), full(), full(), full(), full()],
        compiler_params=pltpu.CompilerParams(
            dimension_semantics=("parallel", "parallel"), vmem_limit_bytes=VMEM_LIMIT),
        name="gdn_prep",
    )(proj, proj, proj, proj, cw, cw, cw, gp)


def _gdn_scan_kernel(u_ref, w_ref, qd_ref, kd_ref, qk_ref, gl_ref, gate_ref, ng_ref,
                     y_ref, state_ref, o_ref):
    rows = u_ref.shape[0]
    c = CHUNK
    dh = GDN_HEAD

    @pl.when(pl.program_id(1) == 0)
    def _():
        state_ref[...] = jnp.zeros_like(state_ref)

    def step(n, carry):
        sl = pl.ds(pl.multiple_of(n * c, c), c)
        gsl = pl.ds(pl.multiple_of(n * 8, 8), 8)
        for h in range(GDN_HEADS):
            ls = slice(h * dh, (h + 1) * dh)
            state = state_ref[h]
            sb = state.astype(BF16)
            wq = jnp.concatenate([w_ref[sl, ls], qd_ref[sl, ls]], axis=0)
            ws = jnp.dot(wq, sb, preferred_element_type=F32)
            v_new = (u_ref[sl, ls] - ws[0:c, :]).astype(BF16)
            o_ref[sl, ls] = ws[c:, :] + jnp.dot(qk_ref[sl, ls][:, 0:c], v_new, preferred_element_type=F32)
            gl = gl_ref[gsl, ls][0:1, :]
            state_ref[h] = state * gl + lax.dot_general(
                kd_ref[sl, ls], v_new, (((0,), (0,)), ((), ())), preferred_element_type=F32)
        return carry

    lax.fori_loop(0, rows // c, step, 0)

    ng = ng_ref[...]
    for h in range(GDN_HEADS):
        ls = slice(h * dh, (h + 1) * dh)
        o = o_ref[:, ls]
        o = o * lax.rsqrt(jnp.mean(o * o, axis=-1, keepdims=True) + NORM_EPS) * ng
        y_ref[:, ls] = (o * _silu(gate_ref[:, ls])).astype(y_ref.dtype)


def _gdn_scan(u, w, qd, kd, qk, gl, proj, bsz, seq, ng):
    rows = min(SCAN_ROWS, seq)
    ns = seq // rows
    g8 = rows // CHUNK * 8
    wd = GDN_WIDTH
    blk = lambda: pl.BlockSpec((rows, wd), lambda b, s: (b * ns + s, 0))
    return pl.pallas_call(
        _gdn_scan_kernel,
        grid=(bsz, ns),
        in_specs=[blk(), blk(), blk(), blk(), blk(),
                  pl.BlockSpec((g8, wd), lambda b, s: (b * ns + s, 0)),
                  pl.BlockSpec((rows, wd), lambda b, s: (b * ns + s, COL_BG * LANE // wd)),
                  pl.BlockSpec((1, GDN_HEAD), lambda b, s: (0, 0))],
        out_specs=blk(),
        out_shape=jax.ShapeDtypeStruct((bsz * seq, wd), BF16),
        scratch_shapes=[pltpu.VMEM((GDN_HEADS, GDN_HEAD, GDN_HEAD), F32),
                        pltpu.VMEM((rows, wd), F32)],
        compiler_params=pltpu.CompilerParams(
            dimension_semantics=("parallel", "arbitrary"), vmem_limit_bytes=VMEM_LIMIT),
        name="gdn_scan",
    )(u, w, qd, kd, qk, gl, proj, ng)


def _pair_mask():
    r = lax.broadcasted_iota(jnp.int32, (LANE, LANE), 0) // RWKV_HEAD
    c = lax.broadcasted_iota(jnp.int32, (LANE, LANE), 1) // RWKV_HEAD
    return r == c


def _rwkv_prep_kernel(r_ref, k_ref, v_ref, wl_ref, al_ref,
                      mur_ref, muk_ref, muv_ref, muwl_ref, mual_ref,
                      w0_ref, wup_ref, a0_ref, aup_ref, kk_ref, ka_ref, rk_ref,
                      u0_ref, o0_ref, wt_ref, rt_ref, mrb_ref, b2_ref, kv_ref, pc_ref, bonus_ref,
                      rs_ref, k2_ref, vs_ref, kn_ref, kna_ref, lw_ref, *, group):
    seq, dw = r_ref.shape
    c = CHUNK
    hd = RWKV_HEAD
    n_chunks = seq // c

    row0 = lax.broadcasted_iota(jnp.int32, (seq, dw), 0) == 0

    def shift(x_ref, mu_ref):
        x = x_ref[...]
        prev = jnp.where(row0, 0.0, pltpu.roll(x, 1, axis=0))
        return x + (prev - x) * mu_ref[...]

    r = shift(r_ref, mur_ref)
    k = shift(k_ref, muk_ref)
    v = shift(v_ref, muv_ref)
    wl = shift(wl_ref, muwl_ref)
    al = shift(al_ref, mual_ref)

    seg = _pair_mask().astype(BF16)
    seg2 = jnp.concatenate([seg, seg], axis=0)

    w_log = -_softplus(-(w0_ref[...] + _dot(jnp.tanh(wl), wup_ref[...]))) - 0.5
    lw_ref[...] = -jnp.exp(w_log)
    a = jax.nn.sigmoid(a0_ref[...] + _dot(al, aup_ref[...]))
    kn = k * kk_ref[...]
    kn = kn * lax.rsqrt(_segsum(kn * kn, seg2) + 1e-6)
    k2 = k * (1.0 + (a - 1.0) * ka_ref[...])
    rs_ref[...] = r
    k2_ref[...] = k2
    vs_ref[...] = v
    kn_ref[...] = kn
    kna_ref[...] = kn * a
    bonus_ref[...] = _segsum(r * k2 * rk_ref[...], seg2) * v

    incl, strict, eye = _chunk_masks()
    ltri = incl.astype(BF16)
    ltri3 = jnp.concatenate([ltri, ltri, ltri], axis=1)

    heads = range(dw // hd)
    lanes = [slice(hh * hd, (hh + 1) * hd) for hh in heads]

    def prepare(n, carry):
        sls = [pl.ds(pl.multiple_of((n * group + i) * c, c), c) for i in range(group)]
        lws = [lw_ref[s, :] for s in sls]
        cums = [_cumsum_rows(ltri3, lw) for lw in lws]
        knas = [kna_ref[s, :] for s in sls]
        k2s = [k2_ref[s, :] for s in sls]
        vvs = [vs_ref[s, :] for s in sls]
        einvs = [jnp.exp(-cum) for cum in cums]
        a_ts = [-kn_ref[s, :] * jnp.exp(cum - lw) for s, cum, lw in zip(sls, cums, lws)]
        b_ts = [kna * e for kna, e in zip(knas, einvs)]
        k_ts = [k2c * e for k2c, e in zip(k2s, einvs)]
        r_ts = [rs_ref[s, :] * jnp.exp(cum) for s, cum in zip(sls, cums)]
        prob = [(i, ls) for i in range(group) for ls in lanes]
        grams = [_dot_nt(jnp.concatenate([a_ts[i][:, ls], r_ts[i][:, ls]], axis=0),
                         jnp.concatenate([b_ts[i][:, ls], k_ts[i][:, ls]], axis=0))
                 for i, ls in prob]
        t_mats = _tri_inverse([jnp.where(strict, g[0:c, 0:c], 0.0) for g in grams], eye)
        akvs = [_dot(jnp.where(strict, g[0:c, c:], 0.0), vvs[i][:, ls]) for g, (i, ls) in zip(grams, prob)]
        tws = [_dot(t, jnp.concatenate([a_ts[i][:, ls], akv], axis=1))
               for t, akv, (i, ls) in zip(t_mats, akvs, prob)]
        o0s = [_dot(jnp.where(incl, g[c:, c:], 0.0), vvs[i][:, ls]) for g, (i, ls) in zip(grams, prob)]
        cat = lambda xs: jnp.concatenate(xs, axis=1)
        nh = len(lanes)
        for i in range(group):
            s = sls[i]
            mine = slice(i * nh, (i + 1) * nh)
            clast = cums[i][c - 1:c, :]
            tail = jnp.exp(clast - cums[i])
            k_2 = k2s[i] * tail
            pc = jnp.broadcast_to(jnp.exp(clast), (c, dw))
            u0_ref[s, :] = cat([tw[:, hd:] for tw in tws[mine]])
            o0_ref[s, :] = cat(o0s[mine])
            wt_ref[s, :] = cat([tw[:, 0:hd] for tw in tws[mine]]).astype(BF16)
            rt_ref[s, :] = r_ts[i].astype(BF16)
            mrb_ref[s, :] = cat([jnp.where(incl, g[c:, 0:c], 0.0) for g in grams[mine]]).astype(BF16)
            b2_ref[s, :] = (knas[i] * tail).astype(BF16)
            kv_ref[s, :] = cat([_dot_tn(k_2[:, ls], vvs[i][:, ls]) for ls in lanes])
            pc_ref[s, :] = cat([pc[:, ls].T for ls in lanes])
        return carry

    lax.fori_loop(0, n_chunks // group, prepare, 0)


def _rwkv_prep(proj, bsz, seq, mu, w0, wup, a0, aup, kk, ka, rk, group=4):
    dw = LANE
    np_ = RWKV_PAIRS
    col = lambda base: pl.BlockSpec((seq, dw), lambda b, p: (b, base + p))
    fixed_col = lambda idx: pl.BlockSpec((seq, dw), lambda b, p: (b, idx))
    vec = lambda base: pl.BlockSpec((1, dw), lambda b, p: (0, base + p))
    fixed_vec = lambda idx: pl.BlockSpec((1, dw), lambda b, p: (0, idx))
    lora = lambda: pl.BlockSpec((dw, dw), lambda b, p: (0, p))
    out = lambda: pl.BlockSpec((seq, dw), lambda b, p: (b, p))
    full = lambda: pltpu.VMEM((seq, dw), F32)
    sds = lambda dt: jax.ShapeDtypeStruct((bsz * seq, RWKV_WIDTH), dt)
    return pl.pallas_call(
        functools.partial(_rwkv_prep_kernel, group=group),
        grid=(bsz, np_),
        in_specs=[col(COL_CR), col(COL_CK), col(COL_CV), fixed_col(COL_CWL), fixed_col(COL_CAL),
                  vec(0), vec(np_), vec(2 * np_), fixed_vec(3 * np_), fixed_vec(3 * np_ + 1),
                  vec(0), lora(), vec(0), lora(), vec(0), vec(0), vec(0)],
        out_specs=[out() for _ in range(9)],
        out_shape=[sds(F32), sds(F32), sds(BF16), sds(BF16), sds(BF16), sds(BF16),
                   sds(F32), sds(F32), sds(F32)],
        scratch_shapes=[full(), full(), full(), full(), full(), full()],
        compiler_params=pltpu.CompilerParams(
            dimension_semantics=("parallel", "parallel"), vmem_limit_bytes=VMEM_LIMIT),
        name="rwkv_prep",
    )(proj, proj, proj, proj, proj, mu, mu, mu, mu, mu, w0, wup, a0, aup, kk, ka, rk)


def _rwkv_scan_kernel(u0_ref, o0_ref, wt_ref, rt_ref, mrb_ref, b2_ref, kv_ref, pc_ref, bonus_ref,
                      gate_ref, gnw_ref, gnb_ref, y_ref, state_ref, o_ref):
    rows = u0_ref.shape[0]
    c = CHUNK
    dw = LANE
    pair = _pair_mask()

    @pl.when(pl.program_id(1) == 0)
    def _():
        state_ref[...] = jnp.zeros_like(state_ref)

    def block_diag(x):
        return jnp.where(pair, jnp.concatenate([x, x], axis=0), 0.0)

    def step(n, carry):
        sl = pl.ds(pl.multiple_of(n * c, c), c)
        for p in range(RWKV_PAIRS):
            ls = slice(p * dw, (p + 1) * dw)
            hm = state_ref[p]
            hb = hm.astype(BF16)
            wr = jnp.concatenate([wt_ref[sl, ls], rt_ref[sl, ls]], axis=0)
            wh = jnp.dot(wr, hb, preferred_element_type=F32)
            u = u0_ref[sl, ls] + wh[0:c, :]
            ub = u.astype(BF16)
            ubd = block_diag(u).astype(BF16)
            o_ref[sl, ls] = (wh[c:, :] + jnp.dot(mrb_ref[sl, ls], ubd, preferred_element_type=F32)
                             + o0_ref[sl, ls])
            upd = lax.dot_general(b2_ref[sl, ls], ub, (((0,), (0,)), ((), ())),
                                  preferred_element_type=F32)
            pc = pc_ref[sl, ls]
            state_ref[p] = (jnp.concatenate([pc, pc], axis=0) * hm
                            + jnp.where(pair, upd, 0.0) + block_diag(kv_ref[sl, ls]))
        return carry

    lax.fori_loop(0, rows // c, step, 0)

    seg = pair.astype(BF16)
    seg2 = jnp.concatenate([seg, seg], axis=0)
    inv_n = 1.0 / RWKV_HEAD
    for p in range(RWKV_PAIRS):
        ls = slice(p * dw, (p + 1) * dw)
        o = o_ref[:, ls]
        cen = o - _segsum(o, seg2) * inv_n
        var = _segsum(cen * cen, seg2) * inv_n
        yn = cen * lax.rsqrt(var + RWKV_GN_EPS) * gnw_ref[:, ls] + gnb_ref[:, ls]
        y_ref[:, ls] = ((yn + bonus_ref[:, ls]) * _silu(gate_ref[:, ls])).astype(y_ref.dtype)


def _rwkv_scan(prep, proj, bsz, seq, gnw, gnb):
    rows = min(SCAN_ROWS, seq)
    ns = seq // rows
    wd = RWKV_WIDTH
    blk = lambda: pl.BlockSpec((rows, wd), lambda b, s: (b * ns + s, 0))
    vec = lambda: pl.BlockSpec((1, wd), lambda b, s: (0, 0))
    return pl.pallas_call(
        _rwkv_scan_kernel,
        grid=(bsz, ns),
        in_specs=[blk() for _ in range(9)]
        + [pl.BlockSpec((rows, wd), lambda b, s: (b * ns + s, COL_CG * LANE // wd)), vec(), vec()],
        out_specs=blk(),
        out_shape=jax.ShapeDtypeStruct((bsz * seq, wd), BF16),
        scratch_shapes=[pltpu.VMEM((RWKV_PAIRS, LANE, LANE), F32),
                        pltpu.VMEM((rows, wd), F32)],
        compiler_params=pltpu.CompilerParams(
            dimension_semantics=("parallel", "arbitrary"), vmem_limit_bytes=VMEM_LIMIT),
        name="rwkv_scan",
    )(*prep, proj, gnw, gnb)


def _pack_w_in(w_in):
    lyr, d, _ = w_in.shape
    w = w_in.astype(BF16)
    z = lambda n: jnp.zeros((lyr, d, n), BF16)
    a1 = 2 * LRU_WIDTH
    b1 = a1 + 4 * GDN_WIDTH
    c0 = b1 + 2 * GDN_HEADS
    c1 = c0 + 3 * RWKV_WIDTH
    c2 = c1 + 2 * LORA
    parts = [w[..., a1:b1], w[..., c0:c1], w[..., c2:], w[..., :a1],
             w[..., b1:c0], z(LANE - 2 * GDN_HEADS),
             w[..., c1:c1 + LORA], z(LANE - LORA), w[..., c1 + LORA:c2], z(LANE - LORA)]
    packed = jnp.concatenate(parts, axis=-1)
    return jnp.concatenate([packed, z(N_PACKED - packed.shape[-1])], axis=-1)


def _pack_mu(mu):
    lyr = mu.shape[0]
    z = jnp.zeros((lyr, LANE - LORA), mu.dtype)
    c1 = 3 * RWKV_WIDTH
    return jnp.concatenate([mu[:, :c1], mu[:, c1:c1 + LORA], z, mu[:, c1 + LORA:], z], axis=-1)


def _block_diag(w):
    lyr, nb, bi, bj = w.shape
    eye = jnp.eye(nb, dtype=w.dtype)
    return jnp.einsum('lnij,nm->lnimj', w, eye).reshape(lyr, nb * bi, nb * bj)


def kernel(x, norm_g, w_in, w_out, lru_conv_w, lru_conv_b, lru_wx, lru_bx, lru_wa, lru_ba, lru_lambda,
           gdn_conv_w, gdn_a_log, gdn_dt_bias, gdn_norm_g, rwkv_mu, rwkv_w0, rwkv_w_up, rwkv_a0,
           rwkv_a_up, rwkv_k_k, rwkv_k_a, rwkv_r_k, rwkv_gn_w, rwkv_gn_b, final_norm_g):
    bsz, seq, d = x.shape
    depth = w_in.shape[0]
    x2 = x.reshape(bsz * seq, d)

    w_in_p = _pack_w_in(w_in)
    w_out_b = w_out.astype(BF16)
    wx_bd = _block_diag(lru_wx).astype(BF16)
    wa_bd = _block_diag(lru_wa).astype(BF16)
    mu_p = _pack_mu(rwkv_mu)
    pad_rows = lambda w: jnp.pad(w, ((0, 0), (0, LANE - LORA), (0, 0))).astype(BF16)
    wup_p = pad_rows(rwkv_w_up)
    aup_p = pad_rows(rwkv_a_up)
    gp = jnp.zeros((depth, 2, LANE), F32)
    gp = gp.at[:, 0, GDN_HEADS:2 * GDN_HEADS].set(gdn_a_log)
    gp = gp.at[:, 1, GDN_HEADS:2 * GDN_HEADS].set(gdn_dt_bias)
    row = lambda t: t.reshape(1, -1)

    for l in range(depth):
        proj = _inproj(x2, row(norm_g[l]), w_in_p[l])
        ya = _lru(proj, bsz, seq, lru_conv_w[l], row(lru_conv_b[l]), wx_bd[l], row(lru_bx[l]),
                  wa_bd[l], row(lru_ba[l]), row(lru_lambda[l]))
        gdn_ops = _gdn_prep(proj, bsz, seq, gdn_conv_w[l], gp[l])
        yb = _gdn_scan(*gdn_ops, proj, bsz, seq, row(gdn_norm_g[l]))
        rwkv_ops = _rwkv_prep(proj, bsz, seq, row(mu_p[l]), row(rwkv_w0[l]), wup_p[l],
                              row(rwkv_a0[l]), aup_p[l], row(rwkv_k_k[l]), row(rwkv_k_a[l]),
                              row(rwkv_r_k[l]))
        yc = _rwkv_scan(rwkv_ops, proj, bsz, seq, row(rwkv_gn_w[l]), row(rwkv_gn_b[l]))
        x2 = _outproj(ya, yb, yc, x2, w_out_b[l], row(final_norm_g), final_norm=(l == depth - 1))
    return x2.reshape(bsz, seq, d)
```

```python
import functools

import jax
import jax.numpy as jnp
from jax import lax
from jax.experimental import pallas as pl
from jax.experimental.pallas import tpu as pltpu

F32 = jnp.float32
BF16 = jnp.bfloat16

NORM_EPS = 1e-6
CONV_W = 4
CHUNK = 64
LRU_WIDTH = 512
LRU_C = 8.0
GDN_HEAD = 128
GDN_HEADS = 6
GDN_WIDTH = GDN_HEADS * GDN_HEAD
RWKV_HEAD = 64
RWKV_WIDTH = 768
LORA = 96
RWKV_GN_EPS = 64e-5

LANE = 128
RWKV_PAIRS = RWKV_WIDTH // LANE
N_PACKED = 7680
COL_BQ, COL_BK, COL_BV, COL_BG = 0, 6, 12, 18
COL_CR, COL_CK, COL_CV, COL_CG = 24, 30, 36, 42
COL_AX, COL_AG = 48, 52
COL_BBA, COL_CWL, COL_CAL = 56, 57, 58
SCAN_ROWS = 512

VMEM_LIMIT = 52 * 1024 * 1024


def _dot(a, b):
    return jnp.dot(a.astype(BF16), b.astype(BF16), preferred_element_type=F32)


def _dot_nt(a, b):
    return lax.dot_general(a.astype(BF16), b.astype(BF16), (((1,), (1,)), ((), ())),
                           preferred_element_type=F32)


def _dot_tn(a, b):
    return lax.dot_general(a.astype(BF16), b.astype(BF16), (((0,), (0,)), ((), ())),
                           preferred_element_type=F32)


def _split2(x):
    hi = x.astype(BF16)
    lo = (x - hi.astype(F32)).astype(BF16)
    return hi, lo


def _split3(x):
    hi = x.astype(BF16)
    r1 = x - hi.astype(F32)
    mid = r1.astype(BF16)
    lo = (r1 - mid.astype(F32)).astype(BF16)
    return hi, mid, lo


def _cumsum_rows(ltri3, x):
    return jnp.dot(ltri3, jnp.concatenate(_split3(x), axis=0), preferred_element_type=F32)


def _segsum(x, seg2):
    return jnp.dot(jnp.concatenate(_split2(x), axis=1), seg2, preferred_element_type=F32)


def _twin_matmul(a_pieces, b_pieces):
    ah, al = a_pieces
    bh, bl = b_pieces
    return jnp.dot(jnp.concatenate([ah, al], axis=1), jnp.concatenate([bh, bl, bh, bl], axis=0),
                   preferred_element_type=F32)


def _tri_inverse(n_twins, eye_twin):
    ps = [eye_twin + n for n in n_twins]
    nk_pieces = [_split2(n) for n in n_twins]
    for _ in range(5):
        nks = [_twin_matmul(x, x) for x in nk_pieces]
        nk_pieces = [_split2(n) for n in nks]
        ps = [p + _twin_matmul(_split2(p), x) for p, x in zip(ps, nk_pieces)]
    return ps


def _chunk_masks(width=CHUNK):
    row = lax.broadcasted_iota(jnp.int32, (CHUNK, width), 0)
    col = lax.broadcasted_iota(jnp.int32, (CHUNK, width), 1) % CHUNK
    return row >= col, row > col, (row == col).astype(F32)


def _softplus(x):
    return jnp.maximum(x, 0.0) + jnp.log1p(jnp.exp(-jnp.abs(x)))


def _silu(x):
    return x * jax.nn.sigmoid(x)


def _inproj_kernel(x_ref, g_ref, w_ref, o_ref, h_ref):
    @pl.when(pl.program_id(1) == 0)
    def _():
        x = x_ref[...]
        ms = jnp.mean(x * x, axis=-1, keepdims=True)
        h_ref[...] = (x * lax.rsqrt(ms + NORM_EPS) * g_ref[...]).astype(BF16)

    o_ref[...] = jnp.dot(h_ref[...], w_ref[...], preferred_element_type=F32)


def _layer_spec(lyr, block, index):
    return pl.BlockSpec((None,) + block, lambda *grid_idx: (lyr,) + index(*grid_idx))


def _inproj(lyr, x2, g, w, tm=1024, tn=1536):
    m, d = x2.shape
    n = w.shape[-1]
    tm = min(tm, m)
    return pl.pallas_call(
        _inproj_kernel,
        grid=(m // tm, n // tn),
        in_specs=[pl.BlockSpec((tm, d), lambda i, j: (i, 0)),
                  _layer_spec(lyr, (1, d), lambda i, j: (0, 0)),
                  _layer_spec(lyr, (d, tn), lambda i, j: (0, j))],
        out_specs=pl.BlockSpec((tm, tn), lambda i, j: (i, j)),
        out_shape=jax.ShapeDtypeStruct((m, n), F32),
        scratch_shapes=[pltpu.VMEM((tm, d), BF16)],
        compiler_params=pltpu.CompilerParams(
            dimension_semantics=("parallel", "arbitrary"), vmem_limit_bytes=VMEM_LIMIT),
        name="inproj",
    )(x2, g, w)


def _outproj_kernel(ya_ref, yb_ref, yc_ref, x_ref, w_ref, fg_ref, o_ref, *, final_norm):
    wa = LRU_WIDTH
    wb = LRU_WIDTH + GDN_WIDTH
    acc = jnp.dot(ya_ref[...], w_ref[0:wa, :], preferred_element_type=F32)
    acc += jnp.dot(yb_ref[...], w_ref[wa:wb, :], preferred_element_type=F32)
    acc += jnp.dot(yc_ref[...], w_ref[wb:, :], preferred_element_type=F32)
    xn = x_ref[...] + acc
    if final_norm:
        ms = jnp.mean(xn * xn, axis=-1, keepdims=True)
        xn = xn * lax.rsqrt(ms + NORM_EPS) * fg_ref[...]
    o_ref[...] = xn


def _outproj(lyr, ya, yb, yc, x2, w, fg, final_norm, tm=512):
    m, d = x2.shape
    tm = min(tm, m)
    return pl.pallas_call(
        functools.partial(_outproj_kernel, final_norm=final_norm),
        grid=(m // tm,),
        in_specs=[pl.BlockSpec((tm, LRU_WIDTH), lambda i: (i, 0)),
                  pl.BlockSpec((tm, GDN_WIDTH), lambda i: (i, 0)),
                  pl.BlockSpec((tm, RWKV_WIDTH), lambda i: (i, 0)),
                  pl.BlockSpec((tm, d), lambda i: (i, 0)),
                  _layer_spec(lyr, (d, d), lambda i: (0, 0)),
                  pl.BlockSpec((1, d), lambda i: (0, 0))],
        out_specs=pl.BlockSpec((tm, d), lambda i: (i, 0)),
        out_shape=jax.ShapeDtypeStruct((m, d), F32),
        compiler_params=pltpu.CompilerParams(
            dimension_semantics=("parallel",), vmem_limit_bytes=VMEM_LIMIT),
        name="outproj",
    )(ya, yb, yc, x2, w, fg)


def _lru_kernel(x_ref, g_ref, cw_ref, cb_ref, wx_ref, bx_ref, wa_ref, ba_ref, lam_ref,
                y_ref, ext_ref, h_ref):
    ts, w = x_ref.shape

    @pl.when(pl.program_id(1) == 0)
    def _():
        ext_ref[0:8, :] = jnp.zeros((8, w), F32)
        h_ref[...] = jnp.zeros_like(h_ref)

    ext_ref[8:, :] = x_ref[...]
    cw = cw_ref[...]
    xc = cb_ref[...] + cw[0:1, :] * ext_ref[pl.ds(5, ts), :]
    for j in range(1, CONV_W):
        xc = xc + cw[j:j + 1, :] * ext_ref[pl.ds(5 + j, ts), :]
    ext_ref[0:8, :] = ext_ref[pl.ds(ts, 8), :]

    gate_x = jax.nn.sigmoid(_dot(xc, wx_ref[...]) + bx_ref[...])
    gate_a = jax.nn.sigmoid(_dot(xc, wa_ref[...]) + ba_ref[...])
    log_a = -LRU_C * gate_a * _softplus(-lam_ref[...])
    a = jnp.exp(log_a)
    mult = jnp.sqrt(jnp.maximum(-jnp.tanh(log_a) * (a * a + 1.0), 0.0))
    u = mult * (gate_x * xc)

    row = lax.broadcasted_iota(jnp.int32, (ts, w), 0)
    d = 1
    while d < ts:
        keep = row >= d
        u = jnp.where(keep, a * pltpu.roll(u, d, axis=0) + u, u)
        a = jnp.where(keep, a * pltpu.roll(a, d, axis=0), a)
        d *= 2
    h = u + a * h_ref[0:1, :]
    h_ref[0:1, :] = h[ts - 1:ts, :]
    y_ref[...] = (h * _silu(g_ref[...])).astype(y_ref.dtype)


def _lru(lyr, proj, bsz, seq, cw, cb, wx, bx, wa, ba, lam, ts=256):
    ns = seq // ts
    w = LRU_WIDTH
    vec = lambda: _layer_spec(lyr, (1, w), lambda b, s: (0, 0))
    return pl.pallas_call(
        _lru_kernel,
        grid=(bsz, ns),
        in_specs=[pl.BlockSpec((ts, w), lambda b, s: (b * ns + s, COL_AX * LANE // w)),
                  pl.BlockSpec((ts, w), lambda b, s: (b * ns + s, COL_AG * LANE // w)),
                  _layer_spec(lyr, (CONV_W, w), lambda b, s: (0, 0)),
                  vec(),
                  _layer_spec(lyr, (w, w), lambda b, s: (0, 0)),
                  vec(),
                  _layer_spec(lyr, (w, w), lambda b, s: (0, 0)),
                  vec(), vec()],
        out_specs=pl.BlockSpec((ts, w), lambda b, s: (b * ns + s, 0)),
        out_shape=jax.ShapeDtypeStruct((bsz * seq, w), BF16),
        scratch_shapes=[pltpu.VMEM((ts + 8, w), F32), pltpu.VMEM((8, w), F32)],
        compiler_params=pltpu.CompilerParams(
            dimension_semantics=("parallel", "arbitrary"), vmem_limit_bytes=VMEM_LIMIT),
        name="lru",
    )(proj, proj, cw, cb, wx, bx, wa, ba, lam)


def _gdn_prep_kernel(q_ref, k_ref, v_ref, ba_ref, cwq_ref, cwk_ref, cwv_ref, gp_ref,
                     u_ref, w_ref, qd_ref, kd_ref, qk_ref, gl_ref,
                     ext_ref, qs_ref, ks_ref, vs_ref, bt_ref, gg_ref, *, group):
    head = pl.program_id(1)
    seq, dh = q_ref.shape
    c = CHUNK
    n_chunks = seq // c

    def conv_silu(x_ref, cw_ref):
        ext_ref[0:8, :] = jnp.zeros((8, dh), F32)
        ext_ref[8:, :] = x_ref[...]
        cw = cw_ref[...]
        xc = cw[0:1, :] * ext_ref[pl.ds(5, seq), :]
        for j in range(1, CONV_W):
            xc = xc + cw[j:j + 1, :] * ext_ref[pl.ds(5 + j, seq), :]
        return _silu(xc)

    def l2norm(t):
        return t * lax.rsqrt(jnp.sum(t * t, axis=-1, keepdims=True) + 1e-6)

    qs_ref[...] = l2norm(conv_silu(q_ref, cwq_ref)) * (GDN_HEAD ** -0.5)
    ks_ref[...] = l2norm(conv_silu(k_ref, cwk_ref))
    vs_ref[...] = conv_silu(v_ref, cwv_ref)

    ba = ba_ref[...]
    lane = lax.broadcasted_iota(jnp.int32, ba.shape, 1)
    gp = gp_ref[...]
    beta_all = jax.nn.sigmoid(ba)
    g_all = -jnp.exp(gp[0:1, :]) * _softplus(ba + gp[1:2, :])
    beta_col = jnp.sum(jnp.where(lane == head, beta_all, 0.0), axis=1, keepdims=True)
    g_col = jnp.sum(jnp.where(lane == head + GDN_HEADS, g_all, 0.0), axis=1, keepdims=True)
    bt_ref[...] = jnp.broadcast_to(beta_col, (seq, dh))
    gg_ref[...] = jnp.broadcast_to(g_col, (seq, dh))

    incl, _, _ = _chunk_masks()
    incl2, strict2, eye2 = _chunk_masks(2 * c)
    first_copy = lax.broadcasted_iota(jnp.int32, (c, dh), 1) < c
    ltri = incl.astype(BF16)
    ltri3 = jnp.concatenate([ltri, ltri, ltri], axis=1)

    def prepare(n, carry):
        rng = range(group)
        sls = [pl.ds(pl.multiple_of((n * group + i) * c, c), c) for i in rng]
        qs = [qs_ref[s, :] for s in sls]
        ks = [ks_ref[s, :] for s in sls]
        betas = [bt_ref[s, :] for s in sls]
        decs = [_cumsum_rows(ltri3, gg_ref[s, :]) for s in sls]
        kbs = [k * b for k, b in zip(ks, betas)]
        grams = [_dot_nt(jnp.concatenate([kb, q], axis=0), jnp.concatenate([k, k], axis=0))
                 for kb, q, k in zip(kbs, qs, ks)]
        lmasks = []
        for dec in decs:
            drow = jnp.concatenate([dec, dec], axis=0).T[0:c, :]
            lmasks.append(jnp.where(incl2, jnp.exp(jnp.where(incl2, dec - drow, 0.0)), 0.0))
        a_mats = [jnp.where(strict2, g[0:c, :] * m, 0.0) for g, m in zip(grams, lmasks)]
        t_mats = _tri_inverse([-a for a in a_mats], eye2)
        edecs = [jnp.exp(dec) for dec in decs]
        uws = [_dot(t[:, 0:c], jnp.concatenate([vs_ref[s, :] * b, kb * e], axis=1))
               for t, s, b, kb, e in zip(t_mats, sls, betas, kbs, edecs)]
        for i in rng:
            s = sls[i]
            dec = decs[i]
            dlast = dec[c - 1:c, :]
            u_ref[s, :] = uws[i][:, 0:dh]
            w_ref[s, :] = uws[i][:, dh:].astype(BF16)
            qk_ref[s, :] = jnp.where(first_copy, grams[i][c:, :] * lmasks[i], 0.0).astype(BF16)
            qd_ref[s, :] = (qs[i] * edecs[i]).astype(BF16)
            kd_ref[s, :] = (ks[i] * jnp.exp(dlast - dec)).astype(BF16)
            gl_ref[pl.ds(pl.multiple_of((n * group + i) * 8, 8), 8), :] = jnp.broadcast_to(
                jnp.exp(dlast), (8, dh))
        return carry

    lax.fori_loop(0, n_chunks // group, prepare, 0)


def _gdn_prep(lyr, proj, bsz, seq, cw, gp, group=8):
    dh = GDN_HEAD
    n8 = seq // CHUNK * 8
    col = lambda base: pl.BlockSpec((seq, dh), lambda b, h: (b, base + h))
    cwspec = lambda base: _layer_spec(lyr, (CONV_W, dh), lambda b, h: (0, base + h))
    out = lambda: pl.BlockSpec((seq, dh), lambda b, h: (b, h))
    full = lambda: pltpu.VMEM((seq, dh), F32)
    sds = lambda dt: jax.ShapeDtypeStruct((bsz * seq, GDN_WIDTH), dt)
    return pl.pallas_call(
        functools.partial(_gdn_prep_kernel, group=group),
        grid=(bsz, GDN_HEADS),
        in_specs=[col(COL_BQ), col(COL_BK), col(COL_BV),
                  pl.BlockSpec((seq, dh), lambda b, h: (b, COL_BBA)),
                  cwspec(0), cwspec(GDN_HEADS), cwspec(2 * GDN_HEADS),
                  _layer_spec(lyr, (2, dh), lambda b, h: (0, 0))],
        out_specs=[out(), out(), out(), out(), out(),
                   pl.BlockSpec((n8, dh), lambda b, h: (b, h))],
        out_shape=[sds(F32), sds(BF16), sds(BF16), sds(BF16), sds(BF16),
                   jax.ShapeDtypeStruct((bsz * n8, GDN_WIDTH), F32)],
        scratch_shapes=[pltpu.VMEM((seq + 8, dh), F32), full(), full(), full(), full(), full()],
        compiler_params=pltpu.CompilerParams(
            dimension_semantics=("parallel", "parallel"), vmem_limit_bytes=VMEM_LIMIT),
        name="gdn_prep",
    )(proj, proj, proj, proj, cw, cw, cw, gp)


def _gdn_scan_kernel(u_ref, w_ref, qd_ref, kd_ref, qk_ref, gl_ref, gate_ref, ng_ref,
                     y_ref, state_ref, o_ref):
    rows = u_ref.shape[0]
    c = CHUNK
    dh = GDN_HEAD

    @pl.when(pl.program_id(1) == 0)
    def _():
        state_ref[...] = jnp.zeros_like(state_ref)

    heads = range(GDN_HEADS)
    lanes = [slice(h * dh, (h + 1) * dh) for h in heads]

    def step(n, carry):
        sl = pl.ds(pl.multiple_of(n * c, c), c)
        gsl = pl.ds(pl.multiple_of(n * 8, 8), 8)
        states = [state_ref[h] for h in heads]
        wss = [jnp.dot(jnp.concatenate([w_ref[sl, ls], qd_ref[sl, ls]], axis=0), st.astype(BF16),
                       preferred_element_type=F32) for ls, st in zip(lanes, states)]
        v_news = [(u_ref[sl, ls] - ws[0:c, :]).astype(BF16) for ls, ws in zip(lanes, wss)]
        upds = [lax.dot_general(kd_ref[sl, ls], vn, (((0,), (0,)), ((), ())), preferred_element_type=F32)
                for ls, vn in zip(lanes, v_news)]
        outs = [jnp.dot(qk_ref[sl, ls][:, 0:c], vn, preferred_element_type=F32)
                for ls, vn in zip(lanes, v_news)]
        for h in heads:
            ls = lanes[h]
            state_ref[h] = states[h] * gl_ref[gsl, ls][0:1, :] + upds[h]
            o_ref[sl, ls] = wss[h][c:, :] + outs[h]
        return carry

    lax.fori_loop(0, rows // c, step, 0)

    ng = ng_ref[...]
    for h in range(GDN_HEADS):
        ls = slice(h * dh, (h + 1) * dh)
        o = o_ref[:, ls]
        o = o * lax.rsqrt(jnp.mean(o * o, axis=-1, keepdims=True) + NORM_EPS) * ng
        y_ref[:, ls] = (o * _silu(gate_ref[:, ls])).astype(y_ref.dtype)


def _gdn_scan(lyr, u, w, qd, kd, qk, gl, proj, bsz, seq, ng):
    rows = min(SCAN_ROWS, seq)
    ns = seq // rows
    g8 = rows // CHUNK * 8
    wd = GDN_WIDTH
    blk = lambda: pl.BlockSpec((rows, wd), lambda b, s: (b * ns + s, 0))
    return pl.pallas_call(
        _gdn_scan_kernel,
        grid=(bsz, ns),
        in_specs=[blk(), blk(), blk(), blk(), blk(),
                  pl.BlockSpec((g8, wd), lambda b, s: (b * ns + s, 0)),
                  pl.BlockSpec((rows, wd), lambda b, s: (b * ns + s, COL_BG * LANE // wd)),
                  _layer_spec(lyr, (1, GDN_HEAD), lambda b, s: (0, 0))],
        out_specs=blk(),
        out_shape=jax.ShapeDtypeStruct((bsz * seq, wd), BF16),
        scratch_shapes=[pltpu.VMEM((GDN_HEADS, GDN_HEAD, GDN_HEAD), F32),
                        pltpu.VMEM((rows, wd), F32)],
        compiler_params=pltpu.CompilerParams(
            dimension_semantics=("parallel", "arbitrary"), vmem_limit_bytes=VMEM_LIMIT),
        name="gdn_scan",
    )(u, w, qd, kd, qk, gl, proj, ng)


def _pair_mask():
    r = lax.broadcasted_iota(jnp.int32, (LANE, LANE), 0) // RWKV_HEAD
    c = lax.broadcasted_iota(jnp.int32, (LANE, LANE), 1) // RWKV_HEAD
    return r == c


def _rwkv_prep_kernel(r_ref, k_ref, v_ref, wl_ref, al_ref,
                      mur_ref, muk_ref, muv_ref, muwl_ref, mual_ref,
                      w0_ref, wup_ref, a0_ref, aup_ref, kk_ref, ka_ref, rk_ref,
                      u0_ref, o0_ref, wt_ref, rt_ref, mrb_ref, b2_ref, kv_ref, pc_ref, bonus_ref,
                      rs_ref, k2_ref, vs_ref, kn_ref, kna_ref, lw_ref, *, group):
    seq, dw = r_ref.shape
    c = CHUNK
    hd = RWKV_HEAD
    n_chunks = seq // c

    row0 = lax.broadcasted_iota(jnp.int32, (seq, dw), 0) == 0

    def shift(x_ref, mu_ref):
        x = x_ref[...]
        prev = jnp.where(row0, 0.0, pltpu.roll(x, 1, axis=0))
        return x + (prev - x) * mu_ref[...]

    r = shift(r_ref, mur_ref)
    k = shift(k_ref, muk_ref)
    v = shift(v_ref, muv_ref)
    wl = shift(wl_ref, muwl_ref)
    al = shift(al_ref, mual_ref)

    seg = _pair_mask().astype(BF16)
    seg2 = jnp.concatenate([seg, seg], axis=0)

    w_log = -_softplus(-(w0_ref[...] + _dot(jnp.tanh(wl), wup_ref[...]))) - 0.5
    lw_ref[...] = -jnp.exp(w_log)
    a = jax.nn.sigmoid(a0_ref[...] + _dot(al, aup_ref[...]))
    kn = k * kk_ref[...]
    kn = kn * lax.rsqrt(_segsum(kn * kn, seg2) + 1e-6)
    k2 = k * (1.0 + (a - 1.0) * ka_ref[...])
    rs_ref[...] = r
    k2_ref[...] = k2
    vs_ref[...] = v
    kn_ref[...] = kn
    kna_ref[...] = kn * a
    bonus_ref[...] = _segsum(r * k2 * rk_ref[...], seg2) * v

    incl, strict, _ = _chunk_masks()
    incl2, strict2, eye2 = _chunk_masks(2 * c)
    ltri = incl.astype(BF16)
    ltri3 = jnp.concatenate([ltri, ltri, ltri], axis=1)
    head0 = lax.broadcasted_iota(jnp.int32, (1, dw), 1) < hd
    head_masks = [head0, jnp.logical_not(head0)]

    def prepare(n, carry):
        sls = [pl.ds(pl.multiple_of((n * group + i) * c, c), c) for i in range(group)]
        lws = [lw_ref[s, :] for s in sls]
        cums = [_cumsum_rows(ltri3, lw) for lw in lws]
        knas = [kna_ref[s, :] for s in sls]
        k2s = [k2_ref[s, :] for s in sls]
        vvs = [vs_ref[s, :] for s in sls]
        einvs = [jnp.exp(-cum) for cum in cums]
        a_ts = [-kn_ref[s, :] * jnp.exp(cum - lw) for s, cum, lw in zip(sls, cums, lws)]
        r_ts = [rs_ref[s, :] * jnp.exp(cum) for s, cum in zip(sls, cums)]
        ars = [jnp.concatenate([a, r], axis=0) for a, r in zip(a_ts, r_ts)]
        bbs = [jnp.concatenate([kna * e] * 2, axis=0).astype(BF16) for kna, e in zip(knas, einvs)]
        kks = [jnp.concatenate([k2c * e] * 2, axis=0).astype(BF16) for k2c, e in zip(k2s, einvs)]
        prob = [(i, m) for i in range(group) for m in head_masks]
        lhss = [jnp.where(m, ars[i], 0.0).astype(BF16) for i, m in prob]
        g_bs = [_dot_nt(lhs, bbs[i]) for lhs, (i, _) in zip(lhss, prob)]
        g_ks = [_dot_nt(lhs, kks[i]) for lhs, (i, _) in zip(lhss, prob)]
        t_mats = _tri_inverse([jnp.where(strict2, g[0:c, :], 0.0) for g in g_bs], eye2)
        akvs = [_dot(jnp.where(strict, g[0:c, 0:c], 0.0), vvs[i]) for g, (i, _) in zip(g_ks, prob)]
        tws = [_dot(t[:, 0:c], jnp.concatenate([a_ts[i], akv], axis=1))
               for t, akv, (i, _) in zip(t_mats, akvs, prob)]
        o0s = [_dot(jnp.where(incl, g[c:, 0:c], 0.0), vvs[i]) for g, (i, _) in zip(g_ks, prob)]
        for i in range(group):
            s = sls[i]
            h0, h1 = 2 * i, 2 * i + 1
            pick = lambda x0, x1: jnp.where(head0, x0, x1)
            clast = cums[i][c - 1:c, :]
            tail = jnp.exp(clast - cums[i])
            kv_full = _dot_tn(k2s[i] * tail, vvs[i])
            pc_full = jnp.broadcast_to(jnp.exp(clast), (dw, dw)).T
            u0_ref[s, :] = pick(tws[h0][:, dw:], tws[h1][:, dw:])
            o0_ref[s, :] = pick(o0s[h0], o0s[h1])
            wt_ref[s, :] = pick(tws[h0][:, 0:dw], tws[h1][:, 0:dw]).astype(BF16)
            rt_ref[s, :] = r_ts[i].astype(BF16)
            mrb_ref[s, :] = jnp.where(incl2, pick(g_bs[h0][c:, :], g_bs[h1][c:, :]), 0.0).astype(BF16)
            b2_ref[s, :] = (knas[i] * tail).astype(BF16)
            kv_ref[s, :] = pick(kv_full[0:c, :], kv_full[c:, :])
            pc_ref[s, :] = pick(pc_full[0:c, :], pc_full[c:, :])
        return carry

    lax.fori_loop(0, n_chunks // group, prepare, 0)


def _rwkv_prep(lyr, proj, bsz, seq, mu, w0, wup, a0, aup, kk, ka, rk, group=4):
    dw = LANE
    np_ = RWKV_PAIRS
    col = lambda base: pl.BlockSpec((seq, dw), lambda b, p: (b, base + p))
    fixed_col = lambda idx: pl.BlockSpec((seq, dw), lambda b, p: (b, idx))
    vec = lambda base: _layer_spec(lyr, (1, dw), lambda b, p: (0, base + p))
    fixed_vec = lambda idx: _layer_spec(lyr, (1, dw), lambda b, p: (0, idx))
    lora = lambda: _layer_spec(lyr, (dw, dw), lambda b, p: (0, p))
    out = lambda: pl.BlockSpec((seq, dw), lambda b, p: (b, p))
    full = lambda: pltpu.VMEM((seq, dw), F32)
    sds = lambda dt: jax.ShapeDtypeStruct((bsz * seq, RWKV_WIDTH), dt)
    return pl.pallas_call(
        functools.partial(_rwkv_prep_kernel, group=group),
        grid=(bsz, np_),
        in_specs=[col(COL_CR), col(COL_CK), col(COL_CV), fixed_col(COL_CWL), fixed_col(COL_CAL),
                  vec(0), vec(np_), vec(2 * np_), fixed_vec(3 * np_), fixed_vec(3 * np_ + 1),
                  vec(0), lora(), vec(0), lora(), vec(0), vec(0), vec(0)],
        out_specs=[out() for _ in range(9)],
        out_shape=[sds(F32), sds(F32), sds(BF16), sds(BF16), sds(BF16), sds(BF16),
                   sds(F32), sds(F32), sds(F32)],
        scratch_shapes=[full(), full(), full(), full(), full(), full()],
        compiler_params=pltpu.CompilerParams(
            dimension_semantics=("parallel", "parallel"), vmem_limit_bytes=VMEM_LIMIT),
        name="rwkv_prep",
    )(proj, proj, proj, proj, proj, mu, mu, mu, mu, mu, w0, wup, a0, aup, kk, ka, rk)


def _rwkv_scan_kernel(u0_ref, o0_ref, wt_ref, rt_ref, mrb_ref, b2_ref, kv_ref, pc_ref, bonus_ref,
                      gate_ref, gnw_ref, gnb_ref, y_ref, state_ref, o_ref):
    rows = u0_ref.shape[0]
    c = CHUNK
    dw = LANE
    pair = _pair_mask()

    @pl.when(pl.program_id(1) == 0)
    def _():
        state_ref[...] = jnp.zeros_like(state_ref)

    def block_diag(x):
        return jnp.where(pair, jnp.concatenate([x, x], axis=0), 0.0)

    pairs = range(RWKV_PAIRS)
    lanes = [slice(p * dw, (p + 1) * dw) for p in pairs]

    def step(n, carry):
        sl = pl.ds(pl.multiple_of(n * c, c), c)
        hms = [state_ref[p] for p in pairs]
        whs = [jnp.dot(jnp.concatenate([wt_ref[sl, ls], rt_ref[sl, ls]], axis=0), hm.astype(BF16),
                       preferred_element_type=F32) for ls, hm in zip(lanes, hms)]
        us = [u0_ref[sl, ls] + wh[0:c, :] for ls, wh in zip(lanes, whs)]
        upds = [lax.dot_general(b2_ref[sl, ls], u.astype(BF16), (((0,), (0,)), ((), ())),
                                preferred_element_type=F32) for ls, u in zip(lanes, us)]
        outs = [jnp.dot(mrb_ref[sl, ls], block_diag(u).astype(BF16), preferred_element_type=F32)
                for ls, u in zip(lanes, us)]
        for p in pairs:
            ls = lanes[p]
            pc = pc_ref[sl, ls]
            state_ref[p] = (jnp.concatenate([pc, pc], axis=0) * hms[p]
                            + jnp.where(pair, upds[p], 0.0) + block_diag(kv_ref[sl, ls]))
            o_ref[sl, ls] = whs[p][c:, :] + outs[p] + o0_ref[sl, ls]
        return carry

    lax.fori_loop(0, rows // c, step, 0)

    seg = pair.astype(BF16)
    seg2 = jnp.concatenate([seg, seg], axis=0)
    inv_n = 1.0 / RWKV_HEAD
    for p in range(RWKV_PAIRS):
        ls = slice(p * dw, (p + 1) * dw)
        o = o_ref[:, ls]
        cen = o - _segsum(o, seg2) * inv_n
        var = _segsum(cen * cen, seg2) * inv_n
        yn = cen * lax.rsqrt(var + RWKV_GN_EPS) * gnw_ref[:, ls] + gnb_ref[:, ls]
        y_ref[:, ls] = ((yn + bonus_ref[:, ls]) * _silu(gate_ref[:, ls])).astype(y_ref.dtype)


def _rwkv_scan(lyr, prep, proj, bsz, seq, gnw, gnb):
    rows = min(SCAN_ROWS, seq)
    ns = seq // rows
    wd = RWKV_WIDTH
    blk = lambda: pl.BlockSpec((rows, wd), lambda b, s: (b * ns + s, 0))
    vec = lambda: _layer_spec(lyr, (1, wd), lambda b, s: (0, 0))
    return pl.pallas_call(
        _rwkv_scan_kernel,
        grid=(bsz, ns),
        in_specs=[blk() for _ in range(9)]
        + [pl.BlockSpec((rows, wd), lambda b, s: (b * ns + s, COL_CG * LANE // wd)), vec(), vec()],
        out_specs=blk(),
        out_shape=jax.ShapeDtypeStruct((bsz * seq, wd), BF16),
        scratch_shapes=[pltpu.VMEM((RWKV_PAIRS, LANE, LANE), F32),
                        pltpu.VMEM((rows, wd), F32)],
        compiler_params=pltpu.CompilerParams(
            dimension_semantics=("parallel", "arbitrary"), vmem_limit_bytes=VMEM_LIMIT),
        name="rwkv_scan",
    )(*prep, proj, gnw, gnb)


def _pack_w_in_kernel(w_ref, o_ref):
    rows = w_ref.shape[0]
    a1 = 2 * LRU_WIDTH
    b1 = a1 + 4 * GDN_WIDTH
    c0 = b1 + 2 * GDN_HEADS
    c1 = c0 + 3 * RWKV_WIDTH
    c2 = c1 + 2 * LORA
    end = c2 + RWKV_WIDTH

    def put(dst, lo, hi, width):
        x = w_ref[:, lo:hi]
        if width > hi - lo:
            x = jnp.concatenate([x, jnp.zeros((rows, width - (hi - lo)), x.dtype)], axis=1)
        o_ref[:, dst:dst + width] = x.astype(BF16)

    put(COL_BQ * LANE, a1, b1, b1 - a1)
    put(COL_CR * LANE, c0, c1, c1 - c0)
    put(COL_CG * LANE, c2, end, end - c2)
    put(COL_AX * LANE, 0, a1, a1)
    put(COL_BBA * LANE, b1, c0, LANE)
    put(COL_CWL * LANE, c1, c1 + LORA, LANE)
    put(COL_CAL * LANE, c1 + LORA, c2, LANE)
    o_ref[:, (COL_CAL + 1) * LANE:] = jnp.zeros((rows, N_PACKED - (COL_CAL + 1) * LANE), BF16)


def _pack_w_in(w_in, tk=256):
    lyr, d, n = w_in.shape
    return pl.pallas_call(
        _pack_w_in_kernel,
        grid=(lyr, d // tk),
        in_specs=[pl.BlockSpec((None, tk, n), lambda l, i: (l, i, 0))],
        out_specs=pl.BlockSpec((None, tk, N_PACKED), lambda l, i: (l, i, 0)),
        out_shape=jax.ShapeDtypeStruct((lyr, d, N_PACKED), BF16),
        compiler_params=pltpu.CompilerParams(
            dimension_semantics=("parallel", "parallel"), vmem_limit_bytes=VMEM_LIMIT),
        name="pack_w_in",
    )(w_in)


def _pack_mu(mu):
    lyr = mu.shape[0]
    z = jnp.zeros((lyr, LANE - LORA), mu.dtype)
    c1 = 3 * RWKV_WIDTH
    return jnp.concatenate([mu[:, :c1], mu[:, c1:c1 + LORA], z, mu[:, c1 + LORA:], z], axis=-1)


def _block_diag(w):
    lyr, nb, bi, bj = w.shape
    eye = jnp.eye(nb, dtype=w.dtype)
    return jnp.einsum('lnij,nm->lnimj', w, eye).reshape(lyr, nb * bi, nb * bj)


def kernel(x, norm_g, w_in, w_out, lru_conv_w, lru_conv_b, lru_wx, lru_bx, lru_wa, lru_ba, lru_lambda,
           gdn_conv_w, gdn_a_log, gdn_dt_bias, gdn_norm_g, rwkv_mu, rwkv_w0, rwkv_w_up, rwkv_a0,
           rwkv_a_up, rwkv_k_k, rwkv_k_a, rwkv_r_k, rwkv_gn_w, rwkv_gn_b, final_norm_g):
    bsz, seq, d = x.shape
    depth = w_in.shape[0]
    x2 = x.reshape(bsz * seq, d)

    w_in_p = _pack_w_in(w_in)
    w_out_b = w_out.astype(BF16)
    wx_bd = _block_diag(lru_wx).astype(BF16)
    wa_bd = _block_diag(lru_wa).astype(BF16)
    mu_p = _pack_mu(rwkv_mu)
    pad_rows = lambda w: jnp.pad(w, ((0, 0), (0, LANE - LORA), (0, 0))).astype(BF16)
    wup_p = pad_rows(rwkv_w_up)
    aup_p = pad_rows(rwkv_a_up)
    gp = jnp.zeros((depth, 2, LANE), F32)
    gp = gp.at[:, 0, GDN_HEADS:2 * GDN_HEADS].set(gdn_a_log)
    gp = gp.at[:, 1, GDN_HEADS:2 * GDN_HEADS].set(gdn_dt_bias)
    rows = lambda t: t.reshape(depth, 1, -1)
    norm_g, lru_conv_b, lru_bx, lru_ba, lru_lambda, gdn_norm_g = map(
        rows, (norm_g, lru_conv_b, lru_bx, lru_ba, lru_lambda, gdn_norm_g))
    mu_p, rwkv_w0, rwkv_a0, rwkv_k_k, rwkv_k_a, rwkv_r_k, rwkv_gn_w, rwkv_gn_b = map(
        rows, (mu_p, rwkv_w0, rwkv_a0, rwkv_k_k, rwkv_k_a, rwkv_r_k, rwkv_gn_w, rwkv_gn_b))
    final_g = final_norm_g.reshape(1, -1)

    for l in range(depth):
        proj = _inproj(l, x2, norm_g, w_in_p)
        ya = _lru(l, proj, bsz, seq, lru_conv_w, lru_conv_b, wx_bd, lru_bx, wa_bd, lru_ba, lru_lambda)
        gdn_ops = _gdn_prep(l, proj, bsz, seq, gdn_conv_w, gp)
        yb = _gdn_scan(l, *gdn_ops, proj, bsz, seq, gdn_norm_g)
        rwkv_ops = _rwkv_prep(l, proj, bsz, seq, mu_p, rwkv_w0, wup_p, rwkv_a0, aup_p,
                              rwkv_k_k, rwkv_k_a, rwkv_r_k)
        yc = _rwkv_scan(l, rwkv_ops, proj, bsz, seq, rwkv_gn_w, rwkv_gn_b)
        x2 = _outproj(l, ya, yb, yc, x2, w_out_b, final_g, final_norm=(l == depth - 1))
    return x2.reshape(bsz, seq, d)
```

```python
import functools

import jax
import jax.numpy as jnp
from jax import lax
from jax.experimental import pallas as pl
from jax.experimental.pallas import tpu as pltpu

F32 = jnp.float32
BF16 = jnp.bfloat16

NORM_EPS = 1e-6
CONV_W = 4
CHUNK = 64
LRU_WIDTH = 512
LRU_C = 8.0
GDN_HEAD = 128
GDN_HEADS = 6
GDN_WIDTH = GDN_HEADS * GDN_HEAD
RWKV_HEAD = 64
RWKV_WIDTH = 768
LORA = 96
RWKV_GN_EPS = 64e-5

LANE = 128
RWKV_PAIRS = RWKV_WIDTH // LANE
N_PACKED = 7680
COL_BQ, COL_BK, COL_BV, COL_BG = 0, 6, 12, 18
COL_CR, COL_CK, COL_CV, COL_CG = 24, 30, 36, 42
COL_AX, COL_AG = 48, 52
COL_BBA, COL_CWL, COL_CAL = 56, 57, 58
SCAN_ROWS = 512

VMEM_LIMIT = 52 * 1024 * 1024


def _dot(a, b):
    return jnp.dot(a.astype(BF16), b.astype(BF16), preferred_element_type=F32)


def _dot_nt(a, b):
    return lax.dot_general(a.astype(BF16), b.astype(BF16), (((1,), (1,)), ((), ())),
                           preferred_element_type=F32)


def _dot_tn(a, b):
    return lax.dot_general(a.astype(BF16), b.astype(BF16), (((0,), (0,)), ((), ())),
                           preferred_element_type=F32)


def _split2(x):
    hi = x.astype(BF16)
    lo = (x - hi.astype(F32)).astype(BF16)
    return hi, lo


def _split3(x):
    hi = x.astype(BF16)
    r1 = x - hi.astype(F32)
    mid = r1.astype(BF16)
    lo = (r1 - mid.astype(F32)).astype(BF16)
    return hi, mid, lo


def _cumsum_rows(ltri3, x):
    return jnp.dot(ltri3, jnp.concatenate(_split3(x), axis=0), preferred_element_type=F32)


def _segsum(x, seg2):
    return jnp.dot(jnp.concatenate(_split2(x), axis=1), seg2, preferred_element_type=F32)


def _hilo_lhs(a_twin, first_copy):
    lo = a_twin - a_twin.astype(BF16).astype(F32)
    return jnp.where(first_copy, a_twin, lo).astype(BF16)


def _hilo_matmul(lhs, b):
    return jnp.dot(lhs, jnp.concatenate([b, b], axis=0), preferred_element_type=F32)


def _tri_inverse_t(a_twins, eye_twin):
    c, w = eye_twin.shape
    first_copy = lax.broadcasted_iota(jnp.int32, (c, w), 1) < c
    qs = [eye_twin + a for a in a_twins]
    a_s = [_hilo_matmul(_hilo_lhs(a, first_copy), a.astype(BF16)) for a in a_twins]
    for _ in range(4):
        outs = [_hilo_matmul(_hilo_lhs(a, first_copy),
                             jnp.concatenate([a.astype(BF16), q.astype(BF16)], axis=1))
                for a, q in zip(a_s, qs)]
        qs = [q + o[:, w:] for q, o in zip(qs, outs)]
        a_s = [o[:, 0:w] for o in outs]
    return [q + _hilo_matmul(_hilo_lhs(a, first_copy), q.astype(BF16)) for q, a in zip(qs, a_s)]


def _chunk_masks(width=CHUNK):
    row = lax.broadcasted_iota(jnp.int32, (CHUNK, width), 0)
    col = lax.broadcasted_iota(jnp.int32, (CHUNK, width), 1) % CHUNK
    return row >= col, row > col, row <= col, row < col, (row == col).astype(F32)


def _softplus(x):
    return jnp.maximum(x, 0.0) + jnp.log1p(jnp.exp(-jnp.abs(x)))


def _sigmoid(x):
    return 0.5 * jnp.tanh(0.5 * x) + 0.5


def _silu(x):
    return x * _sigmoid(x)


def _inproj_kernel(x_ref, g_ref, w_ref, o_ref, h_ref):
    @pl.when(pl.program_id(1) == 0)
    def _():
        x = x_ref[...]
        ms = jnp.mean(x * x, axis=-1, keepdims=True)
        h_ref[...] = (x * lax.rsqrt(ms + NORM_EPS) * g_ref[...]).astype(BF16)

    o_ref[...] = jnp.dot(h_ref[...], w_ref[...], preferred_element_type=F32)


def _layer_spec(lyr, block, index):
    return pl.BlockSpec((None,) + block, lambda *grid_idx: (lyr,) + index(*grid_idx))


def _inproj(lyr, x2, g, w, tm=1024, tn=1536):
    m, d = x2.shape
    n = w.shape[-1]
    tm = min(tm, m)
    return pl.pallas_call(
        _inproj_kernel,
        grid=(m // tm, n // tn),
        in_specs=[pl.BlockSpec((tm, d), lambda i, j: (i, 0)),
                  _layer_spec(lyr, (1, d), lambda i, j: (0, 0)),
                  _layer_spec(lyr, (d, tn), lambda i, j: (0, j))],
        out_specs=pl.BlockSpec((tm, tn), lambda i, j: (i, j)),
        out_shape=jax.ShapeDtypeStruct((m, n), F32),
        scratch_shapes=[pltpu.VMEM((tm, d), BF16)],
        compiler_params=pltpu.CompilerParams(
            dimension_semantics=("parallel", "arbitrary"), vmem_limit_bytes=VMEM_LIMIT),
        name="inproj",
    )(x2, g, w)


def _outproj_kernel(ya_ref, yb_ref, yc_ref, x_ref, w_ref, fg_ref, o_ref, *, final_norm):
    wa = LRU_WIDTH
    wb = LRU_WIDTH + GDN_WIDTH
    acc = jnp.dot(ya_ref[...], w_ref[0:wa, :], preferred_element_type=F32)
    acc += jnp.dot(yb_ref[...], w_ref[wa:wb, :], preferred_element_type=F32)
    acc += jnp.dot(yc_ref[...], w_ref[wb:, :], preferred_element_type=F32)
    xn = x_ref[...] + acc
    if final_norm:
        ms = jnp.mean(xn * xn, axis=-1, keepdims=True)
        xn = xn * lax.rsqrt(ms + NORM_EPS) * fg_ref[...]
    o_ref[...] = xn


def _outproj(lyr, ya, yb, yc, x2, w, fg, final_norm, tm=512):
    m, d = x2.shape
    tm = min(tm, m)
    return pl.pallas_call(
        functools.partial(_outproj_kernel, final_norm=final_norm),
        grid=(m // tm,),
        in_specs=[pl.BlockSpec((tm, LRU_WIDTH), lambda i: (i, 0)),
                  pl.BlockSpec((tm, GDN_WIDTH), lambda i: (i, 0)),
                  pl.BlockSpec((tm, RWKV_WIDTH), lambda i: (i, 0)),
                  pl.BlockSpec((tm, d), lambda i: (i, 0)),
                  _layer_spec(lyr, (d, d), lambda i: (0, 0)),
                  pl.BlockSpec((1, d), lambda i: (0, 0))],
        out_specs=pl.BlockSpec((tm, d), lambda i: (i, 0)),
        out_shape=jax.ShapeDtypeStruct((m, d), F32),
        compiler_params=pltpu.CompilerParams(
            dimension_semantics=("parallel",), vmem_limit_bytes=VMEM_LIMIT),
        name="outproj",
    )(ya, yb, yc, x2, w, fg)


def _lru_kernel(x_ref, g_ref, cw_ref, cb_ref, wx_ref, bx_ref, wa_ref, ba_ref, lam_ref,
                y_ref, ext_ref, h_ref):
    ts, w = x_ref.shape

    @pl.when(pl.program_id(1) == 0)
    def _():
        ext_ref[0:8, :] = jnp.zeros((8, w), F32)
        h_ref[...] = jnp.zeros_like(h_ref)

    ext_ref[8:, :] = x_ref[...]
    cw = cw_ref[...]
    xc = cb_ref[...] + cw[0:1, :] * ext_ref[pl.ds(5, ts), :]
    for j in range(1, CONV_W):
        xc = xc + cw[j:j + 1, :] * ext_ref[pl.ds(5 + j, ts), :]
    ext_ref[0:8, :] = ext_ref[pl.ds(ts, 8), :]

    gate_x = _sigmoid(_dot(xc, wx_ref[...]) + bx_ref[...])
    gate_a = _sigmoid(_dot(xc, wa_ref[...]) + ba_ref[...])
    log_a = -LRU_C * gate_a * _softplus(-lam_ref[...])
    a = jnp.exp(log_a)
    mult = jnp.sqrt(jnp.maximum(-jnp.tanh(log_a) * (a * a + 1.0), 0.0))
    u = mult * (gate_x * xc)

    row = lax.broadcasted_iota(jnp.int32, (ts, w), 0)
    d = 1
    while d < ts:
        keep = row >= d
        u = jnp.where(keep, a * pltpu.roll(u, d, axis=0) + u, u)
        a = jnp.where(keep, a * pltpu.roll(a, d, axis=0), a)
        d *= 2
    h = u + a * h_ref[0:1, :]
    h_ref[0:1, :] = h[ts - 1:ts, :]
    y_ref[...] = (h * _silu(g_ref[...])).astype(y_ref.dtype)


def _lru(lyr, proj, bsz, seq, cw, cb, wx, bx, wa, ba, lam, ts=256):
    ns = seq // ts
    w = LRU_WIDTH
    vec = lambda: _layer_spec(lyr, (1, w), lambda b, s: (0, 0))
    return pl.pallas_call(
        _lru_kernel,
        grid=(bsz, ns),
        in_specs=[pl.BlockSpec((ts, w), lambda b, s: (b * ns + s, COL_AX * LANE // w)),
                  pl.BlockSpec((ts, w), lambda b, s: (b * ns + s, COL_AG * LANE // w)),
                  _layer_spec(lyr, (CONV_W, w), lambda b, s: (0, 0)),
                  vec(),
                  _layer_spec(lyr, (w, w), lambda b, s: (0, 0)),
                  vec(),
                  _layer_spec(lyr, (w, w), lambda b, s: (0, 0)),
                  vec(), vec()],
        out_specs=pl.BlockSpec((ts, w), lambda b, s: (b * ns + s, 0)),
        out_shape=jax.ShapeDtypeStruct((bsz * seq, w), BF16),
        scratch_shapes=[pltpu.VMEM((ts + 8, w), F32), pltpu.VMEM((8, w), F32)],
        compiler_params=pltpu.CompilerParams(
            dimension_semantics=("parallel", "arbitrary"), vmem_limit_bytes=VMEM_LIMIT),
        name="lru",
    )(proj, proj, cw, cb, wx, bx, wa, ba, lam)


def _gdn_prep_kernel(q_ref, k_ref, v_ref, ba_ref, cwq_ref, cwk_ref, cwv_ref, gp_ref,
                     u_ref, w_ref, qd_ref, kd_ref, qk_ref, gl_ref,
                     ext_ref, qs_ref, ks_ref, vs_ref, bt_ref, gg_ref, *, group):
    head = pl.program_id(1)
    seq, dh = q_ref.shape
    c = CHUNK
    n_chunks = seq // c
    group = min(group, n_chunks)

    def conv_silu(x_ref, cw_ref):
        ext_ref[0:8, :] = jnp.zeros((8, dh), F32)
        ext_ref[8:, :] = x_ref[...]
        cw = cw_ref[...]
        xc = cw[0:1, :] * ext_ref[pl.ds(5, seq), :]
        for j in range(1, CONV_W):
            xc = xc + cw[j:j + 1, :] * ext_ref[pl.ds(5 + j, seq), :]
        return _silu(xc)

    def l2norm(t):
        return t * lax.rsqrt(jnp.sum(t * t, axis=-1, keepdims=True) + 1e-6)

    qs_ref[...] = l2norm(conv_silu(q_ref, cwq_ref)) * (GDN_HEAD ** -0.5)
    ks_ref[...] = l2norm(conv_silu(k_ref, cwk_ref))
    vs_ref[...] = conv_silu(v_ref, cwv_ref)

    ba = ba_ref[...]
    lane = lax.broadcasted_iota(jnp.int32, ba.shape, 1)
    gp = gp_ref[...]
    beta_all = _sigmoid(ba)
    g_all = -jnp.exp(gp[0:1, :]) * _softplus(ba + gp[1:2, :])
    beta_col = jnp.sum(jnp.where(lane == head, beta_all, 0.0), axis=1, keepdims=True)
    g_col = jnp.sum(jnp.where(lane == head + GDN_HEADS, g_all, 0.0), axis=1, keepdims=True)
    bt_ref[...] = jnp.broadcast_to(beta_col, (seq, dh))
    gg_ref[...] = jnp.broadcast_to(g_col, (seq, dh))

    incl = _chunk_masks()[0]
    incl2, _, upper2, strict_upper2, eye2 = _chunk_masks(2 * c)
    first_copy = lax.broadcasted_iota(jnp.int32, (c, dh), 1) < c
    ltri = incl.astype(BF16)
    ltri3 = jnp.concatenate([ltri, ltri, ltri], axis=1)

    def prepare(n, carry):
        rng = range(group)
        sls = [pl.ds(pl.multiple_of((n * group + i) * c, c), c) for i in rng]
        qs = [qs_ref[s, :] for s in sls]
        ks = [ks_ref[s, :] for s in sls]
        betas = [bt_ref[s, :] for s in sls]
        decs = [_cumsum_rows(ltri3, gg_ref[s, :]) for s in sls]
        kbs = [k * b for k, b in zip(ks, betas)]
        gram_ts = [_dot_nt(k, jnp.concatenate([kb, kb], axis=0)) for k, kb in zip(ks, kbs)]
        gram_qs = [_dot_nt(q, jnp.concatenate([k, k], axis=0)) for q, k in zip(qs, ks)]
        lmasks, lmask_ts = [], []
        for dec in decs:
            diff = dec - jnp.concatenate([dec, dec], axis=0).T[0:c, :]
            lmasks.append(jnp.where(incl2, jnp.exp(jnp.where(incl2, diff, 0.0)), 0.0))
            lmask_ts.append(jnp.where(upper2, jnp.exp(jnp.where(upper2, -diff, 0.0)), 0.0))
        t_mat_ts = _tri_inverse_t([jnp.where(strict_upper2, -(g * m), 0.0)
                                   for g, m in zip(gram_ts, lmask_ts)], eye2)
        edecs = [jnp.exp(dec) for dec in decs]
        uws = [_dot_tn(t[:, 0:c], jnp.concatenate([vs_ref[s, :] * b, kb * e], axis=1))
               for t, s, b, kb, e in zip(t_mat_ts, sls, betas, kbs, edecs)]
        for i in rng:
            s = sls[i]
            dec = decs[i]
            dlast = dec[c - 1:c, :]
            u_ref[s, :] = uws[i][:, 0:dh]
            w_ref[s, :] = uws[i][:, dh:].astype(BF16)
            qk_ref[s, :] = jnp.where(first_copy, gram_qs[i] * lmasks[i], 0.0).astype(BF16)
            qd_ref[s, :] = (qs[i] * edecs[i]).astype(BF16)
            kd_ref[s, :] = (ks[i] * jnp.exp(dlast - dec)).astype(BF16)
            gl_ref[pl.ds(pl.multiple_of((n * group + i) * 8, 8), 8), :] = jnp.broadcast_to(
                jnp.exp(dlast), (8, dh))
        return carry

    lax.fori_loop(0, n_chunks // group, prepare, 0)


def _gdn_prep(lyr, proj, bsz, seq, cw, gp, group=16):
    dh = GDN_HEAD
    n8 = seq // CHUNK * 8
    col = lambda base: pl.BlockSpec((seq, dh), lambda b, h: (b, base + h))
    cwspec = lambda base: _layer_spec(lyr, (CONV_W, dh), lambda b, h: (0, base + h))
    out = lambda: pl.BlockSpec((seq, dh), lambda b, h: (b, h))
    full = lambda: pltpu.VMEM((seq, dh), F32)
    sds = lambda dt: jax.ShapeDtypeStruct((bsz * seq, GDN_WIDTH), dt)
    return pl.pallas_call(
        functools.partial(_gdn_prep_kernel, group=group),
        grid=(bsz, GDN_HEADS),
        in_specs=[col(COL_BQ), col(COL_BK), col(COL_BV),
                  pl.BlockSpec((seq, dh), lambda b, h: (b, COL_BBA)),
                  cwspec(0), cwspec(GDN_HEADS), cwspec(2 * GDN_HEADS),
                  _layer_spec(lyr, (2, dh), lambda b, h: (0, 0))],
        out_specs=[out(), out(), out(), out(), out(),
                   pl.BlockSpec((n8, dh), lambda b, h: (b, h))],
        out_shape=[sds(F32), sds(BF16), sds(BF16), sds(BF16), sds(BF16),
                   jax.ShapeDtypeStruct((bsz * n8, GDN_WIDTH), F32)],
        scratch_shapes=[pltpu.VMEM((seq + 8, dh), F32), full(), full(), full(), full(), full()],
        compiler_params=pltpu.CompilerParams(
            dimension_semantics=("parallel", "parallel"), vmem_limit_bytes=VMEM_LIMIT),
        name="gdn_prep",
    )(proj, proj, proj, proj, cw, cw, cw, gp)


def _gdn_scan_kernel(u_ref, w_ref, qd_ref, kd_ref, qk_ref, gl_ref, gate_ref, ng_ref,
                     y_ref, state_ref, o_ref):
    rows = u_ref.shape[0]
    c = CHUNK
    dh = GDN_HEAD

    @pl.when(pl.program_id(1) == 0)
    def _():
        state_ref[...] = jnp.zeros_like(state_ref)

    heads = range(GDN_HEADS)
    lanes = [slice(h * dh, (h + 1) * dh) for h in heads]

    def step(n, carry):
        sl = pl.ds(pl.multiple_of(n * c, c), c)
        gsl = pl.ds(pl.multiple_of(n * 8, 8), 8)
        states = [state_ref[h] for h in heads]
        wss = [jnp.dot(jnp.concatenate([w_ref[sl, ls], qd_ref[sl, ls]], axis=0), st.astype(BF16),
                       preferred_element_type=F32) for ls, st in zip(lanes, states)]
        v_news = [(u_ref[sl, ls] - ws[0:c, :]).astype(BF16) for ls, ws in zip(lanes, wss)]
        upds = [lax.dot_general(kd_ref[sl, ls], vn, (((0,), (0,)), ((), ())), preferred_element_type=F32)
                for ls, vn in zip(lanes, v_news)]
        outs = [jnp.dot(qk_ref[sl, ls][:, 0:c], vn, preferred_element_type=F32)
                for ls, vn in zip(lanes, v_news)]
        for h in heads:
            ls = lanes[h]
            state_ref[h] = states[h] * gl_ref[gsl, ls][0:1, :] + upds[h]
            o_ref[sl, ls] = wss[h][c:, :] + outs[h]
        return carry

    lax.fori_loop(0, rows // c, step, 0)

    ng = ng_ref[...]
    for h in range(GDN_HEADS):
        ls = slice(h * dh, (h + 1) * dh)
        o = o_ref[:, ls]
        o = o * lax.rsqrt(jnp.mean(o * o, axis=-1, keepdims=True) + NORM_EPS) * ng
        y_ref[:, ls] = (o * _silu(gate_ref[:, ls])).astype(y_ref.dtype)


def _gdn_scan(lyr, u, w, qd, kd, qk, gl, proj, bsz, seq, ng):
    rows = min(SCAN_ROWS, seq)
    ns = seq // rows
    g8 = rows // CHUNK * 8
    wd = GDN_WIDTH
    blk = lambda: pl.BlockSpec((rows, wd), lambda b, s: (b * ns + s, 0))
    return pl.pallas_call(
        _gdn_scan_kernel,
        grid=(bsz, ns),
        in_specs=[blk(), blk(), blk(), blk(), blk(),
                  pl.BlockSpec((g8, wd), lambda b, s: (b * ns + s, 0)),
                  pl.BlockSpec((rows, wd), lambda b, s: (b * ns + s, COL_BG * LANE // wd)),
                  _layer_spec(lyr, (1, GDN_HEAD), lambda b, s: (0, 0))],
        out_specs=blk(),
        out_shape=jax.ShapeDtypeStruct((bsz * seq, wd), BF16),
        scratch_shapes=[pltpu.VMEM((GDN_HEADS, GDN_HEAD, GDN_HEAD), F32),
                        pltpu.VMEM((rows, wd), F32)],
        compiler_params=pltpu.CompilerParams(
            dimension_semantics=("parallel", "arbitrary"), vmem_limit_bytes=VMEM_LIMIT),
        name="gdn_scan",
    )(u, w, qd, kd, qk, gl, proj, ng)


def _pair_mask():
    r = lax.broadcasted_iota(jnp.int32, (LANE, LANE), 0) // RWKV_HEAD
    c = lax.broadcasted_iota(jnp.int32, (LANE, LANE), 1) // RWKV_HEAD
    return r == c


def _rwkv_prep_kernel(r_ref, k_ref, v_ref, wl_ref, al_ref,
                      mur_ref, muk_ref, muv_ref, muwl_ref, mual_ref,
                      w0_ref, wup_ref, a0_ref, aup_ref, kk_ref, ka_ref, rk_ref,
                      u0_ref, o0_ref, wt_ref, rt_ref, mrb_ref, b2_ref, kv_ref, pc_ref, bonus_ref,
                      rs_ref, k2_ref, vs_ref, kn_ref, kna_ref, lw_ref, *, group):
    seq, dw = r_ref.shape
    c = CHUNK
    hd = RWKV_HEAD
    n_chunks = seq // c
    group = min(group, n_chunks)

    row0 = lax.broadcasted_iota(jnp.int32, (seq, dw), 0) == 0

    def shift(x_ref, mu_ref):
        x = x_ref[...]
        prev = jnp.where(row0, 0.0, pltpu.roll(x, 1, axis=0))
        return x + (prev - x) * mu_ref[...]

    r = shift(r_ref, mur_ref)
    k = shift(k_ref, muk_ref)
    v = shift(v_ref, muv_ref)
    wl = shift(wl_ref, muwl_ref)
    al = shift(al_ref, mual_ref)

    seg = _pair_mask().astype(BF16)
    seg2 = jnp.concatenate([seg, seg], axis=0)

    w_log = -_softplus(-(w0_ref[...] + _dot(jnp.tanh(wl), wup_ref[...]))) - 0.5
    lw_ref[...] = -jnp.exp(w_log)
    a = _sigmoid(a0_ref[...] + _dot(al, aup_ref[...]))
    kn = k * kk_ref[...]
    kn = kn * lax.rsqrt(_segsum(kn * kn, seg2) + 1e-6)
    k2 = k * (1.0 + (a - 1.0) * ka_ref[...])
    rs_ref[...] = r
    k2_ref[...] = k2
    vs_ref[...] = v
    kn_ref[...] = kn
    kna_ref[...] = kn * a
    bonus_ref[...] = _segsum(r * k2 * rk_ref[...], seg2) * v

    incl, strict = _chunk_masks()[0:2]
    incl2, _, _, strict_upper2, eye2 = _chunk_masks(2 * c)
    ltri = incl.astype(BF16)
    ltri3 = jnp.concatenate([ltri, ltri, ltri], axis=1)
    head0 = lax.broadcasted_iota(jnp.int32, (1, dw), 1) < hd
    head_masks = [head0, jnp.logical_not(head0)]

    def prepare(n, carry):
        sls = [pl.ds(pl.multiple_of((n * group + i) * c, c), c) for i in range(group)]
        lws = [lw_ref[s, :] for s in sls]
        cums = [_cumsum_rows(ltri3, lw) for lw in lws]
        knas = [kna_ref[s, :] for s in sls]
        k2s = [k2_ref[s, :] for s in sls]
        vvs = [vs_ref[s, :] for s in sls]
        einvs = [jnp.exp(-cum) for cum in cums]
        a_ts = [-kn_ref[s, :] * jnp.exp(cum - lw) for s, cum, lw in zip(sls, cums, lws)]
        r_ts = [rs_ref[s, :] * jnp.exp(cum) for s, cum in zip(sls, cums)]
        b_ts = [(kna * e).astype(BF16) for kna, e in zip(knas, einvs)]
        a_bs = [a.astype(BF16) for a in a_ts]
        r_bs = [r.astype(BF16) for r in r_ts]
        zero = jnp.zeros((), BF16)
        aas = [jnp.concatenate([a, a], axis=0) for a in a_bs]
        bbs = [jnp.concatenate([b, b], axis=0) for b in b_ts]
        kks = [jnp.concatenate([k2c * e] * 2, axis=0).astype(BF16) for k2c, e in zip(k2s, einvs)]
        prob = [(i, m) for i in range(group) for m in head_masks]
        a_ab_ts = [_dot_nt(jnp.where(m, b_ts[i], zero), aas[i]) for i, m in prob]
        m_rbs = [_dot_nt(jnp.where(m, r_bs[i], zero), bbs[i]) for i, m in prob]
        g_ks = [_dot_nt(jnp.where(m, jnp.concatenate([a_bs[i], r_bs[i]], axis=0), zero), kks[i])
                for i, m in prob]
        t_mat_ts = _tri_inverse_t([jnp.where(strict_upper2, g, 0.0) for g in a_ab_ts], eye2)
        akvs = [_dot(jnp.where(strict, g[0:c, 0:c], 0.0), vvs[i]) for g, (i, _) in zip(g_ks, prob)]
        tws = [_dot_tn(t[:, 0:c], jnp.concatenate([a_ts[i], akv], axis=1))
               for t, akv, (i, _) in zip(t_mat_ts, akvs, prob)]
        o0s = [_dot(jnp.where(incl, g[c:, 0:c], 0.0), vvs[i]) for g, (i, _) in zip(g_ks, prob)]
        for i in range(group):
            s = sls[i]
            h0, h1 = 2 * i, 2 * i + 1
            pick = lambda x0, x1: jnp.where(head0, x0, x1)
            clast = cums[i][c - 1:c, :]
            tail = jnp.exp(clast - cums[i])
            kv_full = _dot_tn(k2s[i] * tail, vvs[i])
            pc_full = jnp.broadcast_to(jnp.exp(clast), (dw, dw)).T
            u0_ref[s, :] = pick(tws[h0][:, dw:], tws[h1][:, dw:])
            o0_ref[s, :] = pick(o0s[h0], o0s[h1])
            wt_ref[s, :] = pick(tws[h0][:, 0:dw], tws[h1][:, 0:dw]).astype(BF16)
            rt_ref[s, :] = r_bs[i]
            mrb_ref[s, :] = jnp.where(incl2, pick(m_rbs[h0], m_rbs[h1]), 0.0).astype(BF16)
            b2_ref[s, :] = (knas[i] * tail).astype(BF16)
            kv_ref[s, :] = pick(kv_full[0:c, :], kv_full[c:, :])
            pc_ref[s, :] = pick(pc_full[0:c, :], pc_full[c:, :])
        return carry

    lax.fori_loop(0, n_chunks // group, prepare, 0)


def _rwkv_prep(lyr, proj, bsz, seq, mu, w0, wup, a0, aup, kk, ka, rk, group=8):
    dw = LANE
    np_ = RWKV_PAIRS
    col = lambda base: pl.BlockSpec((seq, dw), lambda b, p: (b, base + p))
    fixed_col = lambda idx: pl.BlockSpec((seq, dw), lambda b, p: (b, idx))
    vec = lambda base: _layer_spec(lyr, (1, dw), lambda b, p: (0, base + p))
    fixed_vec = lambda idx: _layer_spec(lyr, (1, dw), lambda b, p: (0, idx))
    lora = lambda: _layer_spec(lyr, (dw, dw), lambda b, p: (0, p))
    out = lambda: pl.BlockSpec((seq, dw), lambda b, p: (b, p))
    full = lambda: pltpu.VMEM((seq, dw), F32)
    sds = lambda dt: jax.ShapeDtypeStruct((bsz * seq, RWKV_WIDTH), dt)
    return pl.pallas_call(
        functools.partial(_rwkv_prep_kernel, group=group),
        grid=(bsz, np_),
        in_specs=[col(COL_CR), col(COL_CK), col(COL_CV), fixed_col(COL_CWL), fixed_col(COL_CAL),
                  vec(0), vec(np_), vec(2 * np_), fixed_vec(3 * np_), fixed_vec(3 * np_ + 1),
                  vec(0), lora(), vec(0), lora(), vec(0), vec(0), vec(0)],
        out_specs=[out() for _ in range(9)],
        out_shape=[sds(F32), sds(F32), sds(BF16), sds(BF16), sds(BF16), sds(BF16),
                   sds(F32), sds(F32), sds(F32)],
        scratch_shapes=[full(), full(), full(), full(), full(), full()],
        compiler_params=pltpu.CompilerParams(
            dimension_semantics=("parallel", "parallel"), vmem_limit_bytes=VMEM_LIMIT),
        name="rwkv_prep",
    )(proj, proj, proj, proj, proj, mu, mu, mu, mu, mu, w0, wup, a0, aup, kk, ka, rk)


def _rwkv_scan_kernel(u0_ref, o0_ref, wt_ref, rt_ref, mrb_ref, b2_ref, kv_ref, pc_ref, bonus_ref,
                      gate_ref, gnw_ref, gnb_ref, y_ref, state_ref, o_ref):
    rows = u0_ref.shape[0]
    c = CHUNK
    dw = LANE
    pair = _pair_mask()

    @pl.when(pl.program_id(1) == 0)
    def _():
        state_ref[...] = jnp.zeros_like(state_ref)

    def block_diag(x):
        return jnp.where(pair, jnp.concatenate([x, x], axis=0), 0.0)

    pairs = range(RWKV_PAIRS)
    lanes = [slice(p * dw, (p + 1) * dw) for p in pairs]

    def step(n, carry):
        sl = pl.ds(pl.multiple_of(n * c, c), c)
        hms = [state_ref[p] for p in pairs]
        whs = [jnp.dot(jnp.concatenate([wt_ref[sl, ls], rt_ref[sl, ls]], axis=0), hm.astype(BF16),
                       preferred_element_type=F32) for ls, hm in zip(lanes, hms)]
        us = [u0_ref[sl, ls] + wh[0:c, :] for ls, wh in zip(lanes, whs)]
        upds = [lax.dot_general(b2_ref[sl, ls], u.astype(BF16), (((0,), (0,)), ((), ())),
                                preferred_element_type=F32) for ls, u in zip(lanes, us)]
        outs = [jnp.dot(mrb_ref[sl, ls], block_diag(u).astype(BF16), preferred_element_type=F32)
                for ls, u in zip(lanes, us)]
        for p in pairs:
            ls = lanes[p]
            pc = pc_ref[sl, ls]
            state_ref[p] = (jnp.concatenate([pc, pc], axis=0) * hms[p]
                            + jnp.where(pair, upds[p], 0.0) + block_diag(kv_ref[sl, ls]))
            o_ref[sl, ls] = whs[p][c:, :] + outs[p] + o0_ref[sl, ls]
        return carry

    lax.fori_loop(0, rows // c, step, 0)

    seg = pair.astype(BF16)
    seg2 = jnp.concatenate([seg, seg], axis=0)
    inv_n = 1.0 / RWKV_HEAD
    for p in range(RWKV_PAIRS):
        ls = slice(p * dw, (p + 1) * dw)
        o = o_ref[:, ls]
        cen = o - _segsum(o, seg2) * inv_n
        var = _segsum(cen * cen, seg2) * inv_n
        yn = cen * lax.rsqrt(var + RWKV_GN_EPS) * gnw_ref[:, ls] + gnb_ref[:, ls]
        y_ref[:, ls] = ((yn + bonus_ref[:, ls]) * _silu(gate_ref[:, ls])).astype(y_ref.dtype)


def _rwkv_scan(lyr, prep, proj, bsz, seq, gnw, gnb):
    rows = min(SCAN_ROWS, seq)
    ns = seq // rows
    wd = RWKV_WIDTH
    blk = lambda: pl.BlockSpec((rows, wd), lambda b, s: (b * ns + s, 0))
    vec = lambda: _layer_spec(lyr, (1, wd), lambda b, s: (0, 0))
    return pl.pallas_call(
        _rwkv_scan_kernel,
        grid=(bsz, ns),
        in_specs=[blk() for _ in range(9)]
        + [pl.BlockSpec((rows, wd), lambda b, s: (b * ns + s, COL_CG * LANE // wd)), vec(), vec()],
        out_specs=blk(),
        out_shape=jax.ShapeDtypeStruct((bsz * seq, wd), BF16),
        scratch_shapes=[pltpu.VMEM((RWKV_PAIRS, LANE, LANE), F32),
                        pltpu.VMEM((rows, wd), F32)],
        compiler_params=pltpu.CompilerParams(
            dimension_semantics=("parallel", "arbitrary"), vmem_limit_bytes=VMEM_LIMIT),
        name="rwkv_scan",
    )(*prep, proj, gnw, gnb)


def _pack_w_in_kernel(w_ref, o_ref):
    rows = w_ref.shape[0]
    a1 = 2 * LRU_WIDTH
    b1 = a1 + 4 * GDN_WIDTH
    c0 = b1 + 2 * GDN_HEADS
    c1 = c0 + 3 * RWKV_WIDTH
    c2 = c1 + 2 * LORA
    end = c2 + RWKV_WIDTH

    def put(dst, lo, hi, width):
        x = w_ref[:, lo:hi]
        if width > hi - lo:
            x = jnp.concatenate([x, jnp.zeros((rows, width - (hi - lo)), x.dtype)], axis=1)
        o_ref[:, dst:dst + width] = x.astype(BF16)

    put(COL_BQ * LANE, a1, b1, b1 - a1)
    put(COL_CR * LANE, c0, c1, c1 - c0)
    put(COL_CG * LANE, c2, end, end - c2)
    put(COL_AX * LANE, 0, a1, a1)
    put(COL_BBA * LANE, b1, c0, LANE)
    put(COL_CWL * LANE, c1, c1 + LORA, LANE)
    put(COL_CAL * LANE, c1 + LORA, c2, LANE)
    o_ref[:, (COL_CAL + 1) * LANE:] = jnp.zeros((rows, N_PACKED - (COL_CAL + 1) * LANE), BF16)


def _pack_w_in(w_in, tk=256):
    lyr, d, n = w_in.shape
    return pl.pallas_call(
        _pack_w_in_kernel,
        grid=(lyr, d // tk),
        in_specs=[pl.BlockSpec((None, tk, n), lambda l, i: (l, i, 0))],
        out_specs=pl.BlockSpec((None, tk, N_PACKED), lambda l, i: (l, i, 0)),
        out_shape=jax.ShapeDtypeStruct((lyr, d, N_PACKED), BF16),
        compiler_params=pltpu.CompilerParams(
            dimension_semantics=("parallel", "parallel"), vmem_limit_bytes=VMEM_LIMIT),
        name="pack_w_in",
    )(w_in)


def _pack_mu(mu):
    lyr = mu.shape[0]
    z = jnp.zeros((lyr, LANE - LORA), mu.dtype)
    c1 = 3 * RWKV_WIDTH
    return jnp.concatenate([mu[:, :c1], mu[:, c1:c1 + LORA], z, mu[:, c1 + LORA:], z], axis=-1)


def _block_diag(w):
    lyr, nb, bi, bj = w.shape
    eye = jnp.eye(nb, dtype=w.dtype)
    return jnp.einsum('lnij,nm->lnimj', w, eye).reshape(lyr, nb * bi, nb * bj)


def kernel(x, norm_g, w_in, w_out, lru_conv_w, lru_conv_b, lru_wx, lru_bx, lru_wa, lru_ba, lru_lambda,
           gdn_conv_w, gdn_a_log, gdn_dt_bias, gdn_norm_g, rwkv_mu, rwkv_w0, rwkv_w_up, rwkv_a0,
           rwkv_a_up, rwkv_k_k, rwkv_k_a, rwkv_r_k, rwkv_gn_w, rwkv_gn_b, final_norm_g):
    bsz, seq, d = x.shape
    depth = w_in.shape[0]
    x2 = x.reshape(bsz * seq, d)

    w_in_p = _pack_w_in(w_in)
    w_out_b = w_out.astype(BF16)
    wx_bd = _block_diag(lru_wx).astype(BF16)
    wa_bd = _block_diag(lru_wa).astype(BF16)
    mu_p = _pack_mu(rwkv_mu)
    pad_rows = lambda w: jnp.pad(w, ((0, 0), (0, LANE - LORA), (0, 0))).astype(BF16)
    wup_p = pad_rows(rwkv_w_up)
    aup_p = pad_rows(rwkv_a_up)
    gp = jnp.zeros((depth, 2, LANE), F32)
    gp = gp.at[:, 0, GDN_HEADS:2 * GDN_HEADS].set(gdn_a_log)
    gp = gp.at[:, 1, GDN_HEADS:2 * GDN_HEADS].set(gdn_dt_bias)
    rows = lambda t: t.reshape(depth, 1, -1)
    norm_g, lru_conv_b, lru_bx, lru_ba, lru_lambda, gdn_norm_g = map(
        rows, (norm_g, lru_conv_b, lru_bx, lru_ba, lru_lambda, gdn_norm_g))
    mu_p, rwkv_w0, rwkv_a0, rwkv_k_k, rwkv_k_a, rwkv_r_k, rwkv_gn_w, rwkv_gn_b = map(
        rows, (mu_p, rwkv_w0, rwkv_a0, rwkv_k_k, rwkv_k_a, rwkv_r_k, rwkv_gn_w, rwkv_gn_b))
    final_g = final_norm_g.reshape(1, -1)

    for l in range(depth):
        proj = _inproj(l, x2, norm_g, w_in_p)
        ya = _lru(l, proj, bsz, seq, lru_conv_w, lru_conv_b, wx_bd, lru_bx, wa_bd, lru_ba, lru_lambda)
        gdn_ops = _gdn_prep(l, proj, bsz, seq, gdn_conv_w, gp)
        yb = _gdn_scan(l, *gdn_ops, proj, bsz, seq, gdn_norm_g)
        rwkv_ops = _rwkv_prep(l, proj, bsz, seq, mu_p, rwkv_w0, wup_p, rwkv_a0, aup_p,
                              rwkv_k_k, rwkv_k_a, rwkv_r_k)
        yc = _rwkv_scan(l, rwkv_ops, proj, bsz, seq, rwkv_gn_w, rwkv_gn_b)
        x2 = _outproj(l, ya, yb, yc, x2, w_out_b, final_g, final_norm=(l == depth - 1))
    return x2.reshape(bsz, seq, d)
```

```python
import functools

import jax
import jax.numpy as jnp
from jax import lax
from jax.experimental import pallas as pl
from jax.experimental.pallas import tpu as pltpu

F32 = jnp.float32
BF16 = jnp.bfloat16

NORM_EPS = 1e-6
CONV_W = 4
CHUNK = 64
LRU_WIDTH = 512
LRU_C = 8.0
GDN_HEAD = 128
GDN_HEADS = 6
GDN_WIDTH = GDN_HEADS * GDN_HEAD
RWKV_HEAD = 64
RWKV_WIDTH = 768
LORA = 96
RWKV_GN_EPS = 64e-5

LANE = 128
RWKV_PAIRS = RWKV_WIDTH // LANE
N_PACKED = 7680
COL_BQ, COL_BK, COL_BV, COL_BG = 0, 6, 12, 18
COL_CR, COL_CK, COL_CV, COL_CG = 24, 30, 36, 42
COL_AX, COL_AG = 48, 52
COL_BBA, COL_CWL, COL_CAL = 56, 57, 58
SCAN_ROWS = 512

VMEM_LIMIT = 52 * 1024 * 1024


def _dot(a, b):
    return jnp.dot(a.astype(BF16), b.astype(BF16), preferred_element_type=F32)


def _dot_nt(a, b):
    return lax.dot_general(a.astype(BF16), b.astype(BF16), (((1,), (1,)), ((), ())),
                           preferred_element_type=F32)


def _dot_tn(a, b):
    return lax.dot_general(a.astype(BF16), b.astype(BF16), (((0,), (0,)), ((), ())),
                           preferred_element_type=F32)


def _split2(x):
    hi = x.astype(BF16)
    lo = (x - hi.astype(F32)).astype(BF16)
    return hi, lo


def _split3(x):
    hi = x.astype(BF16)
    r1 = x - hi.astype(F32)
    mid = r1.astype(BF16)
    lo = (r1 - mid.astype(F32)).astype(BF16)
    return hi, mid, lo


def _cumsum_rows(ltri3, x):
    return jnp.dot(ltri3, jnp.concatenate(_split3(x), axis=0), preferred_element_type=F32)


def _segsum(x, seg2):
    return jnp.dot(jnp.concatenate(_split2(x), axis=1), seg2, preferred_element_type=F32)


def _hilo_lhs(a_twin, first_copy):
    lo = a_twin - a_twin.astype(BF16).astype(F32)
    return jnp.where(first_copy, a_twin, lo).astype(BF16)


def _hilo_matmul(lhs, b):
    return jnp.dot(lhs, jnp.concatenate([b, b], axis=0), preferred_element_type=F32)


def _tri_inverse_t(a_twins, eye_twin):
    c, w = eye_twin.shape
    first_copy = lax.broadcasted_iota(jnp.int32, (c, w), 1) < c
    qs = [eye_twin + a for a in a_twins]
    a_s = [_hilo_matmul(_hilo_lhs(a, first_copy), a.astype(BF16)) for a in a_twins]
    for _ in range(4):
        outs = [_hilo_matmul(_hilo_lhs(a, first_copy),
                             jnp.concatenate([a.astype(BF16), q.astype(BF16)], axis=1))
                for a, q in zip(a_s, qs)]
        qs = [q + o[:, w:] for q, o in zip(qs, outs)]
        a_s = [o[:, 0:w] for o in outs]
    return [q + _hilo_matmul(_hilo_lhs(a, first_copy), q.astype(BF16)) for q, a in zip(qs, a_s)]


def _chunk_masks(width=CHUNK):
    row = lax.broadcasted_iota(jnp.int32, (CHUNK, width), 0)
    col = lax.broadcasted_iota(jnp.int32, (CHUNK, width), 1) % CHUNK
    return row >= col, row > col, row <= col, row < col, (row == col).astype(F32)


def _softplus(x):
    return jnp.maximum(x, 0.0) + jnp.log1p(jnp.exp(-jnp.abs(x)))


def _sigmoid(x):
    return 0.5 * jnp.tanh(0.5 * x) + 0.5


def _silu(x):
    return x * _sigmoid(x)


def _inproj_kernel(x_ref, g_ref, w_ref, o_ref, h_ref):
    @pl.when(pl.program_id(1) == 0)
    def _():
        x = x_ref[...]
        ms = jnp.mean(x * x, axis=-1, keepdims=True)
        h_ref[...] = (x * lax.rsqrt(ms + NORM_EPS) * g_ref[...]).astype(BF16)

    o_ref[...] = lax.dot_general(h_ref[...], w_ref[...], (((1,), (1,)), ((), ())),
                                 preferred_element_type=F32)


def _layer_spec(lyr, block, index):
    return pl.BlockSpec((None,) + block, lambda *grid_idx: (lyr,) + index(*grid_idx))


def _inproj(lyr, x2, g, w, tm=1024, tn=1536):
    m, d = x2.shape
    n = w.shape[1]
    tm = min(tm, m)
    return pl.pallas_call(
        _inproj_kernel,
        grid=(m // tm, n // tn),
        in_specs=[pl.BlockSpec((tm, d), lambda i, j: (i, 0)),
                  _layer_spec(lyr, (1, d), lambda i, j: (0, 0)),
                  _layer_spec(lyr, (tn, d), lambda i, j: (j, 0))],
        out_specs=pl.BlockSpec((tm, tn), lambda i, j: (i, j)),
        out_shape=jax.ShapeDtypeStruct((m, n), F32),
        scratch_shapes=[pltpu.VMEM((tm, d), BF16)],
        compiler_params=pltpu.CompilerParams(
            dimension_semantics=("parallel", "arbitrary"), vmem_limit_bytes=VMEM_LIMIT),
        name="inproj",
    )(x2, g, w)


def _outproj_kernel(ya_ref, yb_ref, yc_ref, x_ref, w_ref, fg_ref, o_ref, *, final_norm):
    wa = LRU_WIDTH
    wb = LRU_WIDTH + GDN_WIDTH
    acc = jnp.dot(ya_ref[...], w_ref[0:wa, :], preferred_element_type=F32)
    acc += jnp.dot(yb_ref[...], w_ref[wa:wb, :], preferred_element_type=F32)
    acc += jnp.dot(yc_ref[...], w_ref[wb:, :], preferred_element_type=F32)
    xn = x_ref[...] + acc
    if final_norm:
        ms = jnp.mean(xn * xn, axis=-1, keepdims=True)
        xn = xn * lax.rsqrt(ms + NORM_EPS) * fg_ref[...]
    o_ref[...] = xn


def _outproj(lyr, ya, yb, yc, x2, w, fg, final_norm, tm=512):
    m, d = x2.shape
    tm = min(tm, m)
    return pl.pallas_call(
        functools.partial(_outproj_kernel, final_norm=final_norm),
        grid=(m // tm,),
        in_specs=[pl.BlockSpec((tm, LRU_WIDTH), lambda i: (i, 0)),
                  pl.BlockSpec((tm, GDN_WIDTH), lambda i: (i, 0)),
                  pl.BlockSpec((tm, RWKV_WIDTH), lambda i: (i, 0)),
                  pl.BlockSpec((tm, d), lambda i: (i, 0)),
                  _layer_spec(lyr, (d, d), lambda i: (0, 0)),
                  pl.BlockSpec((1, d), lambda i: (0, 0))],
        out_specs=pl.BlockSpec((tm, d), lambda i: (i, 0)),
        out_shape=jax.ShapeDtypeStruct((m, d), F32),
        compiler_params=pltpu.CompilerParams(
            dimension_semantics=("parallel",), vmem_limit_bytes=VMEM_LIMIT),
        name="outproj",
    )(ya, yb, yc, x2, w, fg)


def _lru_kernel(x_ref, g_ref, cw_ref, cb_ref, wx_ref, bx_ref, wa_ref, ba_ref, lam_ref,
                y_ref, ext_ref, h_ref):
    ts, w = x_ref.shape

    @pl.when(pl.program_id(1) == 0)
    def _():
        ext_ref[0:8, :] = jnp.zeros((8, w), F32)
        h_ref[...] = jnp.zeros_like(h_ref)

    ext_ref[8:, :] = x_ref[...]
    cw = cw_ref[...]
    xc = cb_ref[...] + cw[0:1, :] * ext_ref[pl.ds(5, ts), :]
    for j in range(1, CONV_W):
        xc = xc + cw[j:j + 1, :] * ext_ref[pl.ds(5 + j, ts), :]
    ext_ref[0:8, :] = ext_ref[pl.ds(ts, 8), :]

    gate_x = _sigmoid(_dot(xc, wx_ref[...]) + bx_ref[...])
    gate_a = _sigmoid(_dot(xc, wa_ref[...]) + ba_ref[...])
    log_a = -LRU_C * gate_a * _softplus(-lam_ref[...])
    a = jnp.exp(log_a)
    mult = jnp.sqrt(jnp.maximum(-jnp.tanh(log_a) * (a * a + 1.0), 0.0))
    u = mult * (gate_x * xc)

    row = lax.broadcasted_iota(jnp.int32, (ts, w), 0)
    d = 1
    while d < ts:
        keep = row >= d
        u = jnp.where(keep, a * pltpu.roll(u, d, axis=0) + u, u)
        a = jnp.where(keep, a * pltpu.roll(a, d, axis=0), a)
        d *= 2
    h = u + a * h_ref[0:1, :]
    h_ref[0:1, :] = h[ts - 1:ts, :]
    y_ref[...] = (h * _silu(g_ref[...])).astype(y_ref.dtype)


def _lru(lyr, proj, bsz, seq, cw, cb, wx, bx, wa, ba, lam, ts=256):
    ns = seq // ts
    w = LRU_WIDTH
    vec = lambda: _layer_spec(lyr, (1, w), lambda b, s: (0, 0))
    return pl.pallas_call(
        _lru_kernel,
        grid=(bsz, ns),
        in_specs=[pl.BlockSpec((ts, w), lambda b, s: (b * ns + s, COL_AX * LANE // w)),
                  pl.BlockSpec((ts, w), lambda b, s: (b * ns + s, COL_AG * LANE // w)),
                  _layer_spec(lyr, (CONV_W, w), lambda b, s: (0, 0)),
                  vec(),
                  _layer_spec(lyr, (w, w), lambda b, s: (0, 0)),
                  vec(),
                  _layer_spec(lyr, (w, w), lambda b, s: (0, 0)),
                  vec(), vec()],
        out_specs=pl.BlockSpec((ts, w), lambda b, s: (b * ns + s, 0)),
        out_shape=jax.ShapeDtypeStruct((bsz * seq, w), BF16),
        scratch_shapes=[pltpu.VMEM((ts + 8, w), F32), pltpu.VMEM((8, w), F32)],
        compiler_params=pltpu.CompilerParams(
            dimension_semantics=("parallel", "arbitrary"), vmem_limit_bytes=VMEM_LIMIT),
        name="lru",
    )(proj, proj, cw, cb, wx, bx, wa, ba, lam)


def _gdn_prep_kernel(q_ref, k_ref, v_ref, ba_ref, cwq_ref, cwk_ref, cwv_ref, gp_ref,
                     u_ref, w_ref, qd_ref, kd_ref, qk_ref, gl_ref,
                     ext_ref, qs_ref, ks_ref, vs_ref, bt_ref, gg_ref, *, group):
    head = pl.program_id(1)
    seq, dh = q_ref.shape
    c = CHUNK
    n_chunks = seq // c
    group = min(group, n_chunks)

    def conv_silu(x_ref, cw_ref):
        ext_ref[0:8, :] = jnp.zeros((8, dh), F32)
        ext_ref[8:, :] = x_ref[...]
        cw = cw_ref[...]
        xc = cw[0:1, :] * ext_ref[pl.ds(5, seq), :]
        for j in range(1, CONV_W):
            xc = xc + cw[j:j + 1, :] * ext_ref[pl.ds(5 + j, seq), :]
        return _silu(xc)

    def l2norm(t):
        return t * lax.rsqrt(jnp.sum(t * t, axis=-1, keepdims=True) + 1e-6)

    qs_ref[...] = l2norm(conv_silu(q_ref, cwq_ref)) * (GDN_HEAD ** -0.5)
    ks_ref[...] = l2norm(conv_silu(k_ref, cwk_ref))
    vs_ref[...] = conv_silu(v_ref, cwv_ref)

    ba = ba_ref[...]
    lane = lax.broadcasted_iota(jnp.int32, ba.shape, 1)
    gp = gp_ref[...]
    beta_all = _sigmoid(ba)
    g_all = -jnp.exp(gp[0:1, :]) * _softplus(ba + gp[1:2, :])
    beta_col = jnp.sum(jnp.where(lane == head, beta_all, 0.0), axis=1, keepdims=True)
    g_col = jnp.sum(jnp.where(lane == head + GDN_HEADS, g_all, 0.0), axis=1, keepdims=True)
    bt_ref[...] = jnp.broadcast_to(beta_col, (seq, dh))
    gg_ref[...] = jnp.broadcast_to(g_col, (seq, dh))

    incl = _chunk_masks()[0]
    incl2, _, upper2, strict_upper2, eye2 = _chunk_masks(2 * c)
    first_copy = lax.broadcasted_iota(jnp.int32, (c, dh), 1) < c
    ltri = incl.astype(BF16)
    ltri3 = jnp.concatenate([ltri, ltri, ltri], axis=1)

    def prepare(n, carry):
        rng = range(group)
        sls = [pl.ds(pl.multiple_of((n * group + i) * c, c), c) for i in rng]
        qs = [qs_ref[s, :] for s in sls]
        ks = [ks_ref[s, :] for s in sls]
        betas = [bt_ref[s, :] for s in sls]
        decs = [_cumsum_rows(ltri3, gg_ref[s, :]) for s in sls]
        kbs = [k * b for k, b in zip(ks, betas)]
        gram_ts = [_dot_nt(k, jnp.concatenate([kb, kb], axis=0)) for k, kb in zip(ks, kbs)]
        gram_qs = [_dot_nt(q, jnp.concatenate([k, k], axis=0)) for q, k in zip(qs, ks)]
        lmasks, lmask_ts = [], []
        for dec in decs:
            diff = dec - jnp.concatenate([dec, dec], axis=0).T[0:c, :]
            lmasks.append(jnp.where(incl2, jnp.exp(jnp.where(incl2, diff, 0.0)), 0.0))
            lmask_ts.append(jnp.where(upper2, jnp.exp(jnp.where(upper2, -diff, 0.0)), 0.0))
        t_mat_ts = _tri_inverse_t([jnp.where(strict_upper2, -(g * m), 0.0)
                                   for g, m in zip(gram_ts, lmask_ts)], eye2)
        edecs = [jnp.exp(dec) for dec in decs]
        uws = [_dot_tn(t[:, 0:c], jnp.concatenate([vs_ref[s, :] * b, kb * e], axis=1))
               for t, s, b, kb, e in zip(t_mat_ts, sls, betas, kbs, edecs)]
        for i in rng:
            s = sls[i]
            dec = decs[i]
            dlast = dec[c - 1:c, :]
            u_ref[s, :] = uws[i][:, 0:dh]
            w_ref[s, :] = uws[i][:, dh:].astype(BF16)
            qk_ref[s, :] = jnp.where(first_copy, gram_qs[i] * lmasks[i], 0.0).astype(BF16)
            qd_ref[s, :] = (qs[i] * edecs[i]).astype(BF16)
            kd_ref[s, :] = (ks[i] * jnp.exp(dlast - dec)).astype(BF16)
            gl_ref[pl.ds(pl.multiple_of((n * group + i) * 8, 8), 8), :] = jnp.broadcast_to(
                jnp.exp(dlast), (8, dh))
        return carry

    lax.fori_loop(0, n_chunks // group, prepare, 0)


def _gdn_prep(lyr, proj, bsz, seq, cw, gp, group=16):
    dh = GDN_HEAD
    n8 = seq // CHUNK * 8
    col = lambda base: pl.BlockSpec((seq, dh), lambda b, h: (b, base + h))
    cwspec = lambda base: _layer_spec(lyr, (CONV_W, dh), lambda b, h: (0, base + h))
    out = lambda: pl.BlockSpec((seq, dh), lambda b, h: (b, h))
    full = lambda: pltpu.VMEM((seq, dh), F32)
    sds = lambda dt: jax.ShapeDtypeStruct((bsz * seq, GDN_WIDTH), dt)
    return pl.pallas_call(
        functools.partial(_gdn_prep_kernel, group=group),
        grid=(bsz, GDN_HEADS),
        in_specs=[col(COL_BQ), col(COL_BK), col(COL_BV),
                  pl.BlockSpec((seq, dh), lambda b, h: (b, COL_BBA)),
                  cwspec(0), cwspec(GDN_HEADS), cwspec(2 * GDN_HEADS),
                  _layer_spec(lyr, (2, dh), lambda b, h: (0, 0))],
        out_specs=[out(), out(), out(), out(), out(),
                   pl.BlockSpec((n8, dh), lambda b, h: (b, h))],
        out_shape=[sds(F32), sds(BF16), sds(BF16), sds(BF16), sds(BF16),
                   jax.ShapeDtypeStruct((bsz * n8, GDN_WIDTH), F32)],
        scratch_shapes=[pltpu.VMEM((seq + 8, dh), F32), full(), full(), full(), full(), full()],
        compiler_params=pltpu.CompilerParams(
            dimension_semantics=("parallel", "parallel"), vmem_limit_bytes=VMEM_LIMIT),
        name="gdn_prep",
    )(proj, proj, proj, proj, cw, cw, cw, gp)


def _gdn_scan_kernel(u_ref, w_ref, qd_ref, kd_ref, qk_ref, gl_ref, gate_ref, ng_ref,
                     y_ref, state_ref, o_ref):
    rows = u_ref.shape[0]
    c = CHUNK
    dh = GDN_HEAD

    @pl.when(pl.program_id(1) == 0)
    def _():
        state_ref[...] = jnp.zeros_like(state_ref)

    heads = range(GDN_HEADS)
    lanes = [slice(h * dh, (h + 1) * dh) for h in heads]

    def step(n, carry):
        sl = pl.ds(pl.multiple_of(n * c, c), c)
        gsl = pl.ds(pl.multiple_of(n * 8, 8), 8)
        states = [state_ref[h] for h in heads]
        wss = [jnp.dot(jnp.concatenate([w_ref[sl, ls], qd_ref[sl, ls]], axis=0), st.astype(BF16),
                       preferred_element_type=F32) for ls, st in zip(lanes, states)]
        v_news = [(u_ref[sl, ls] - ws[0:c, :]).astype(BF16) for ls, ws in zip(lanes, wss)]
        upds = [lax.dot_general(kd_ref[sl, ls], vn, (((0,), (0,)), ((), ())), preferred_element_type=F32)
                for ls, vn in zip(lanes, v_news)]
        outs = [jnp.dot(qk_ref[sl, ls][:, 0:c], vn, preferred_element_type=F32)
                for ls, vn in zip(lanes, v_news)]
        for h in heads:
            ls = lanes[h]
            state_ref[h] = states[h] * gl_ref[gsl, ls][0:1, :] + upds[h]
            o_ref[sl, ls] = wss[h][c:, :] + outs[h]
        return carry

    lax.fori_loop(0, rows // c, step, 0)

    ng = ng_ref[...]
    for h in range(GDN_HEADS):
        ls = slice(h * dh, (h + 1) * dh)
        o = o_ref[:, ls]
        o = o * lax.rsqrt(jnp.mean(o * o, axis=-1, keepdims=True) + NORM_EPS) * ng
        y_ref[:, ls] = (o * _silu(gate_ref[:, ls])).astype(y_ref.dtype)


def _gdn_scan(lyr, u, w, qd, kd, qk, gl, proj, bsz, seq, ng):
    rows = min(SCAN_ROWS, seq)
    ns = seq // rows
    g8 = rows // CHUNK * 8
    wd = GDN_WIDTH
    blk = lambda: pl.BlockSpec((rows, wd), lambda b, s: (b * ns + s, 0))
    return pl.pallas_call(
        _gdn_scan_kernel,
        grid=(bsz, ns),
        in_specs=[blk(), blk(), blk(), blk(), blk(),
                  pl.BlockSpec((g8, wd), lambda b, s: (b * ns + s, 0)),
                  pl.BlockSpec((rows, wd), lambda b, s: (b * ns + s, COL_BG * LANE // wd)),
                  _layer_spec(lyr, (1, GDN_HEAD), lambda b, s: (0, 0))],
        out_specs=blk(),
        out_shape=jax.ShapeDtypeStruct((bsz * seq, wd), BF16),
        scratch_shapes=[pltpu.VMEM((GDN_HEADS, GDN_HEAD, GDN_HEAD), F32),
                        pltpu.VMEM((rows, wd), F32)],
        compiler_params=pltpu.CompilerParams(
            dimension_semantics=("parallel", "arbitrary"), vmem_limit_bytes=VMEM_LIMIT),
        name="gdn_scan",
    )(u, w, qd, kd, qk, gl, proj, ng)


def _pair_mask():
    r = lax.broadcasted_iota(jnp.int32, (LANE, LANE), 0) // RWKV_HEAD
    c = lax.broadcasted_iota(jnp.int32, (LANE, LANE), 1) // RWKV_HEAD
    return r == c


def _rwkv_prep_kernel(r_ref, k_ref, v_ref, wl_ref, al_ref,
                      mur_ref, muk_ref, muv_ref, muwl_ref, mual_ref,
                      w0_ref, wup_ref, a0_ref, aup_ref, kk_ref, ka_ref, rk_ref,
                      u0_ref, o0_ref, wt_ref, rt_ref, mrb_ref, b2_ref, kv_ref, pc_ref, bonus_ref,
                      rs_ref, k2_ref, vs_ref, kn_ref, kna_ref, lw_ref, *, group):
    seq, dw = r_ref.shape
    c = CHUNK
    hd = RWKV_HEAD
    n_chunks = seq // c
    group = min(group, n_chunks)

    row0 = lax.broadcasted_iota(jnp.int32, (seq, dw), 0) == 0

    def shift(x_ref, mu_ref):
        x = x_ref[...]
        prev = jnp.where(row0, 0.0, pltpu.roll(x, 1, axis=0))
        return x + (prev - x) * mu_ref[...]

    r = shift(r_ref, mur_ref)
    k = shift(k_ref, muk_ref)
    v = shift(v_ref, muv_ref)
    wl = shift(wl_ref, muwl_ref)
    al = shift(al_ref, mual_ref)

    seg = _pair_mask().astype(BF16)
    seg2 = jnp.concatenate([seg, seg], axis=0)

    w_log = -_softplus(-(w0_ref[...] + _dot(jnp.tanh(wl), wup_ref[...]))) - 0.5
    lw_ref[...] = -jnp.exp(w_log)
    a = _sigmoid(a0_ref[...] + _dot(al, aup_ref[...]))
    kn = k * kk_ref[...]
    kn = kn * lax.rsqrt(_segsum(kn * kn, seg2) + 1e-6)
    k2 = k * (1.0 + (a - 1.0) * ka_ref[...])
    rs_ref[...] = r
    k2_ref[...] = k2
    vs_ref[...] = v
    kn_ref[...] = kn
    kna_ref[...] = kn * a
    bonus_ref[...] = _segsum(r * k2 * rk_ref[...], seg2) * v

    incl, strict = _chunk_masks()[0:2]
    incl2, _, _, strict_upper2, eye2 = _chunk_masks(2 * c)
    ltri = incl.astype(BF16)
    ltri3 = jnp.concatenate([ltri, ltri, ltri], axis=1)
    head0 = lax.broadcasted_iota(jnp.int32, (1, dw), 1) < hd
    head_masks = [head0, jnp.logical_not(head0)]

    def prepare(n, carry):
        sls = [pl.ds(pl.multiple_of((n * group + i) * c, c), c) for i in range(group)]
        lws = [lw_ref[s, :] for s in sls]
        cums = [_cumsum_rows(ltri3, lw) for lw in lws]
        knas = [kna_ref[s, :] for s in sls]
        k2s = [k2_ref[s, :] for s in sls]
        vvs = [vs_ref[s, :] for s in sls]
        einvs = [jnp.exp(-cum) for cum in cums]
        a_ts = [-kn_ref[s, :] * jnp.exp(cum - lw) for s, cum, lw in zip(sls, cums, lws)]
        r_ts = [rs_ref[s, :] * jnp.exp(cum) for s, cum in zip(sls, cums)]
        b_ts = [(kna * e).astype(BF16) for kna, e in zip(knas, einvs)]
        a_bs = [a.astype(BF16) for a in a_ts]
        r_bs = [r.astype(BF16) for r in r_ts]
        zero = jnp.zeros((), BF16)
        aas = [jnp.concatenate([a, a], axis=0) for a in a_bs]
        bbs = [jnp.concatenate([b, b], axis=0) for b in b_ts]
        kks = [jnp.concatenate([k2c * e] * 2, axis=0).astype(BF16) for k2c, e in zip(k2s, einvs)]
        prob = [(i, m) for i in range(group) for m in head_masks]
        a_ab_ts = [_dot_nt(jnp.where(m, b_ts[i], zero), aas[i]) for i, m in prob]
        m_rbs = [_dot_nt(jnp.where(m, r_bs[i], zero), bbs[i]) for i, m in prob]
        g_ks = [_dot_nt(jnp.where(m, jnp.concatenate([a_bs[i], r_bs[i]], axis=0), zero), kks[i])
                for i, m in prob]
        t_mat_ts = _tri_inverse_t([jnp.where(strict_upper2, g, 0.0) for g in a_ab_ts], eye2)
        akvs = [_dot(jnp.where(strict, g[0:c, 0:c], 0.0), vvs[i]) for g, (i, _) in zip(g_ks, prob)]
        tws = [_dot_tn(t[:, 0:c], jnp.concatenate([a_ts[i], akv], axis=1))
               for t, akv, (i, _) in zip(t_mat_ts, akvs, prob)]
        o0s = [_dot(jnp.where(incl, g[c:, 0:c], 0.0), vvs[i]) for g, (i, _) in zip(g_ks, prob)]
        for i in range(group):
            s = sls[i]
            h0, h1 = 2 * i, 2 * i + 1
            pick = lambda x0, x1: jnp.where(head0, x0, x1)
            clast = cums[i][c - 1:c, :]
            tail = jnp.exp(clast - cums[i])
            kv_full = _dot_tn(k2s[i] * tail, vvs[i])
            pc_full = jnp.broadcast_to(jnp.exp(clast), (dw, dw)).T
            u0_ref[s, :] = pick(tws[h0][:, dw:], tws[h1][:, dw:])
            o0_ref[s, :] = pick(o0s[h0], o0s[h1])
            wt_ref[s, :] = pick(tws[h0][:, 0:dw], tws[h1][:, 0:dw]).astype(BF16)
            rt_ref[s, :] = r_bs[i]
            mrb_ref[s, :] = jnp.where(incl2, pick(m_rbs[h0], m_rbs[h1]), 0.0).astype(BF16)
            b2_ref[s, :] = (knas[i] * tail).astype(BF16)
            kv_ref[s, :] = pick(kv_full[0:c, :], kv_full[c:, :])
            pc_ref[s, :] = pick(pc_full[0:c, :], pc_full[c:, :])
        return carry

    lax.fori_loop(0, n_chunks // group, prepare, 0)


def _rwkv_prep(lyr, proj, bsz, seq, mu, w0, wup, a0, aup, kk, ka, rk, group=8):
    dw = LANE
    np_ = RWKV_PAIRS
    col = lambda base: pl.BlockSpec((seq, dw), lambda b, p: (b, base + p))
    fixed_col = lambda idx: pl.BlockSpec((seq, dw), lambda b, p: (b, idx))
    vec = lambda base: _layer_spec(lyr, (1, dw), lambda b, p: (0, base + p))
    fixed_vec = lambda idx: _layer_spec(lyr, (1, dw), lambda b, p: (0, idx))
    lora = lambda: _layer_spec(lyr, (dw, dw), lambda b, p: (0, p))
    out = lambda: pl.BlockSpec((seq, dw), lambda b, p: (b, p))
    full = lambda: pltpu.VMEM((seq, dw), F32)
    sds = lambda dt: jax.ShapeDtypeStruct((bsz * seq, RWKV_WIDTH), dt)
    return pl.pallas_call(
        functools.partial(_rwkv_prep_kernel, group=group),
        grid=(bsz, np_),
        in_specs=[col(COL_CR), col(COL_CK), col(COL_CV), fixed_col(COL_CWL), fixed_col(COL_CAL),
                  vec(0), vec(np_), vec(2 * np_), fixed_vec(3 * np_), fixed_vec(3 * np_ + 1),
                  vec(0), lora(), vec(0), lora(), vec(0), vec(0), vec(0)],
        out_specs=[out() for _ in range(9)],
        out_shape=[sds(F32), sds(F32), sds(BF16), sds(BF16), sds(BF16), sds(BF16),
                   sds(F32), sds(F32), sds(F32)],
        scratch_shapes=[full(), full(), full(), full(), full(), full()],
        compiler_params=pltpu.CompilerParams(
            dimension_semantics=("parallel", "parallel"), vmem_limit_bytes=VMEM_LIMIT),
        name="rwkv_prep",
    )(proj, proj, proj, proj, proj, mu, mu, mu, mu, mu, w0, wup, a0, aup, kk, ka, rk)


def _rwkv_scan_kernel(u0_ref, o0_ref, wt_ref, rt_ref, mrb_ref, b2_ref, kv_ref, pc_ref, bonus_ref,
                      gate_ref, gnw_ref, gnb_ref, y_ref, state_ref, o_ref):
    rows = u0_ref.shape[0]
    c = CHUNK
    dw = LANE
    pair = _pair_mask()

    @pl.when(pl.program_id(1) == 0)
    def _():
        state_ref[...] = jnp.zeros_like(state_ref)

    def block_diag(x):
        return jnp.where(pair, jnp.concatenate([x, x], axis=0), 0.0)

    pairs = range(RWKV_PAIRS)
    lanes = [slice(p * dw, (p + 1) * dw) for p in pairs]

    def step(n, carry):
        sl = pl.ds(pl.multiple_of(n * c, c), c)
        hms = [state_ref[p] for p in pairs]
        whs = [jnp.dot(jnp.concatenate([wt_ref[sl, ls], rt_ref[sl, ls]], axis=0), hm.astype(BF16),
                       preferred_element_type=F32) for ls, hm in zip(lanes, hms)]
        us = [u0_ref[sl, ls] + wh[0:c, :] for ls, wh in zip(lanes, whs)]
        upds = [lax.dot_general(b2_ref[sl, ls], u.astype(BF16), (((0,), (0,)), ((), ())),
                                preferred_element_type=F32) for ls, u in zip(lanes, us)]
        outs = [jnp.dot(mrb_ref[sl, ls], block_diag(u).astype(BF16), preferred_element_type=F32)
                for ls, u in zip(lanes, us)]
        for p in pairs:
            ls = lanes[p]
            pc = pc_ref[sl, ls]
            state_ref[p] = (jnp.concatenate([pc, pc], axis=0) * hms[p]
                            + jnp.where(pair, upds[p], 0.0) + block_diag(kv_ref[sl, ls]))
            o_ref[sl, ls] = whs[p][c:, :] + outs[p] + o0_ref[sl, ls]
        return carry

    lax.fori_loop(0, rows // c, step, 0)

    seg = pair.astype(BF16)
    seg2 = jnp.concatenate([seg, seg], axis=0)
    inv_n = 1.0 / RWKV_HEAD
    for p in range(RWKV_PAIRS):
        ls = slice(p * dw, (p + 1) * dw)
        o = o_ref[:, ls]
        cen = o - _segsum(o, seg2) * inv_n
        var = _segsum(cen * cen, seg2) * inv_n
        yn = cen * lax.rsqrt(var + RWKV_GN_EPS) * gnw_ref[:, ls] + gnb_ref[:, ls]
        y_ref[:, ls] = ((yn + bonus_ref[:, ls]) * _silu(gate_ref[:, ls])).astype(y_ref.dtype)


def _rwkv_scan(lyr, prep, proj, bsz, seq, gnw, gnb):
    rows = min(SCAN_ROWS, seq)
    ns = seq // rows
    wd = RWKV_WIDTH
    blk = lambda: pl.BlockSpec((rows, wd), lambda b, s: (b * ns + s, 0))
    vec = lambda: _layer_spec(lyr, (1, wd), lambda b, s: (0, 0))
    return pl.pallas_call(
        _rwkv_scan_kernel,
        grid=(bsz, ns),
        in_specs=[blk() for _ in range(9)]
        + [pl.BlockSpec((rows, wd), lambda b, s: (b * ns + s, COL_CG * LANE // wd)), vec(), vec()],
        out_specs=blk(),
        out_shape=jax.ShapeDtypeStruct((bsz * seq, wd), BF16),
        scratch_shapes=[pltpu.VMEM((RWKV_PAIRS, LANE, LANE), F32),
                        pltpu.VMEM((rows, wd), F32)],
        compiler_params=pltpu.CompilerParams(
            dimension_semantics=("parallel", "arbitrary"), vmem_limit_bytes=VMEM_LIMIT),
        name="rwkv_scan",
    )(*prep, proj, gnw, gnb)


def _pack_source(j):
    a1 = 2 * LRU_WIDTH
    b1 = a1 + 4 * GDN_WIDTH
    c0 = b1 + 2 * GDN_HEADS
    c1 = c0 + 3 * RWKV_WIDTH
    c2 = c1 + 2 * LORA
    start, valid = jnp.int32(0), jnp.int32(0)
    for first, count, src, width in ((COL_BQ, COL_CR - COL_BQ, a1, LANE),
                                     (COL_CR, COL_CG - COL_CR, c0, LANE),
                                     (COL_CG, COL_AX - COL_CG, c2, LANE),
                                     (COL_AX, COL_BBA - COL_AX, 0, LANE),
                                     (COL_BBA, 1, b1, 2 * GDN_HEADS),
                                     (COL_CWL, 1, c1, LORA),
                                     (COL_CAL, 1, c1 + LORA, LORA)):
        inside = (j >= first) & (j < first + count)
        start = jnp.where(inside, src + (j - first) * LANE, start)
        valid = jnp.where(inside, width, valid)
    return start, valid


def _pack_w_in_kernel(w_ref, o_ref):
    _, valid = _pack_source(pl.program_id(0))
    keep = lax.broadcasted_iota(jnp.int32, (w_ref.shape[0], w_ref.shape[2]), 0) < valid
    for lyr in range(w_ref.shape[1]):
        o_ref[lyr] = jnp.where(keep, w_ref[:, lyr, :], 0.0).astype(BF16)


def _pack_w_in(w_in):
    lyr, d, n = w_in.shape
    w_t = jnp.transpose(w_in, (2, 0, 1))
    return pl.pallas_call(
        _pack_w_in_kernel,
        grid=(N_PACKED // LANE,),
        in_specs=[pl.BlockSpec((pl.Element(LANE), pl.Element(lyr), pl.Element(d)),
                               lambda j: (_pack_source(j)[0], 0, 0))],
        out_specs=pl.BlockSpec((lyr, LANE, d), lambda j: (0, j, 0)),
        out_shape=jax.ShapeDtypeStruct((lyr, N_PACKED, d), BF16),
        compiler_params=pltpu.CompilerParams(
            dimension_semantics=("parallel",), vmem_limit_bytes=VMEM_LIMIT),
        name="pack_w_in",
    )(w_t)


def _pack_mu(mu):
    lyr = mu.shape[0]
    z = jnp.zeros((lyr, LANE - LORA), mu.dtype)
    c1 = 3 * RWKV_WIDTH
    return jnp.concatenate([mu[:, :c1], mu[:, c1:c1 + LORA], z, mu[:, c1 + LORA:], z], axis=-1)


def _block_diag(w):
    lyr, nb, bi, bj = w.shape
    eye = jnp.eye(nb, dtype=w.dtype)
    return jnp.einsum('lnij,nm->lnimj', w, eye).reshape(lyr, nb * bi, nb * bj)


def kernel(x, norm_g, w_in, w_out, lru_conv_w, lru_conv_b, lru_wx, lru_bx, lru_wa, lru_ba, lru_lambda,
           gdn_conv_w, gdn_a_log, gdn_dt_bias, gdn_norm_g, rwkv_mu, rwkv_w0, rwkv_w_up, rwkv_a0,
           rwkv_a_up, rwkv_k_k, rwkv_k_a, rwkv_r_k, rwkv_gn_w, rwkv_gn_b, final_norm_g):
    bsz, seq, d = x.shape
    depth = w_in.shape[0]
    x2 = x.reshape(bsz * seq, d)

    w_in_p = _pack_w_in(w_in)
    w_out_b = w_out.astype(BF16)
    wx_bd = _block_diag(lru_wx).astype(BF16)
    wa_bd = _block_diag(lru_wa).astype(BF16)
    mu_p = _pack_mu(rwkv_mu)
    pad_rows = lambda w: jnp.pad(w, ((0, 0), (0, LANE - LORA), (0, 0))).astype(BF16)
    wup_p = pad_rows(rwkv_w_up)
    aup_p = pad_rows(rwkv_a_up)
    gp = jnp.zeros((depth, 2, LANE), F32)
    gp = gp.at[:, 0, GDN_HEADS:2 * GDN_HEADS].set(gdn_a_log)
    gp = gp.at[:, 1, GDN_HEADS:2 * GDN_HEADS].set(gdn_dt_bias)
    rows = lambda t: t.reshape(depth, 1, -1)
    norm_g, lru_conv_b, lru_bx, lru_ba, lru_lambda, gdn_norm_g = map(
        rows, (norm_g, lru_conv_b, lru_bx, lru_ba, lru_lambda, gdn_norm_g))
    mu_p, rwkv_w0, rwkv_a0, rwkv_k_k, rwkv_k_a, rwkv_r_k, rwkv_gn_w, rwkv_gn_b = map(
        rows, (mu_p, rwkv_w0, rwkv_a0, rwkv_k_k, rwkv_k_a, rwkv_r_k, rwkv_gn_w, rwkv_gn_b))
    final_g = final_norm_g.reshape(1, -1)

    for l in range(depth):
        proj = _inproj(l, x2, norm_g, w_in_p)
        ya = _lru(l, proj, bsz, seq, lru_conv_w, lru_conv_b, wx_bd, lru_bx, wa_bd, lru_ba, lru_lambda)
        gdn_ops = _gdn_prep(l, proj, bsz, seq, gdn_conv_w, gp)
        yb = _gdn_scan(l, *gdn_ops, proj, bsz, seq, gdn_norm_g)
        rwkv_ops = _rwkv_prep(l, proj, bsz, seq, mu_p, rwkv_w0, wup_p, rwkv_a0, aup_p,
                              rwkv_k_k, rwkv_k_a, rwkv_r_k)
        yc = _rwkv_scan(l, rwkv_ops, proj, bsz, seq, rwkv_gn_w, rwkv_gn_b)
        x2 = _outproj(l, ya, yb, yc, x2, w_out_b, final_g, final_norm=(l == depth - 1))
    return x2.reshape(bsz, seq, d)
```

```python
import functools

import jax
import jax.numpy as jnp
from jax import lax
from jax.experimental import pallas as pl
from jax.experimental.pallas import tpu as pltpu

F32 = jnp.float32
BF16 = jnp.bfloat16

NORM_EPS = 1e-6
CONV_W = 4
CHUNK = 64
LRU_WIDTH = 512
LRU_C = 8.0
GDN_HEAD = 128
GDN_HEADS = 6
GDN_WIDTH = GDN_HEADS * GDN_HEAD
RWKV_HEAD = 64
RWKV_WIDTH = 768
LORA = 96
RWKV_GN_EPS = 64e-5

LANE = 128
RWKV_PAIRS = RWKV_WIDTH // LANE
N_PACKED = 7680
COL_BQ, COL_BK, COL_BV, COL_BG = 0, 6, 12, 18
COL_CR, COL_CK, COL_CV, COL_CG = 24, 30, 36, 42
COL_AX, COL_AG = 48, 52
COL_BBA, COL_CWL, COL_CAL = 56, 57, 58
SCAN_ROWS = 512

VMEM_LIMIT = 52 * 1024 * 1024


def _dot(a, b):
    return jnp.dot(a.astype(BF16), b.astype(BF16), preferred_element_type=F32)


def _dot_nt(a, b):
    return lax.dot_general(a.astype(BF16), b.astype(BF16), (((1,), (1,)), ((), ())),
                           preferred_element_type=F32)


def _dot_tn(a, b):
    return lax.dot_general(a.astype(BF16), b.astype(BF16), (((0,), (0,)), ((), ())),
                           preferred_element_type=F32)


def _split2(x):
    hi = x.astype(BF16)
    lo = (x - hi.astype(F32)).astype(BF16)
    return hi, lo


def _split3(x):
    hi = x.astype(BF16)
    r1 = x - hi.astype(F32)
    mid = r1.astype(BF16)
    lo = (r1 - mid.astype(F32)).astype(BF16)
    return hi, mid, lo


def _cumsum_rows(ltri3, x):
    return jnp.dot(ltri3, jnp.concatenate(_split3(x), axis=0), preferred_element_type=F32)


def _segsum(x, seg2):
    return jnp.dot(jnp.concatenate(_split2(x), axis=1), seg2, preferred_element_type=F32)


def _tri_inverse_t(a_twins, eye_twin):
    c, w = eye_twin.shape
    first_half = lax.broadcasted_iota(jnp.int32, (c, w), 1) < c
    xs = [jnp.where(first_half, a, eye_twin) for a in a_twins]
    for _ in range(6):
        xbs = [x.astype(BF16) for x in xs]
        outs = [jnp.dot(xb[:, 0:c], xb, preferred_element_type=F32) for xb in xbs]
        xs = [jnp.where(first_half, o, x + o) for x, o in zip(xs, outs)]
    return xs


def _chunk_masks(width=CHUNK):
    row = lax.broadcasted_iota(jnp.int32, (CHUNK, width), 0)
    col = lax.broadcasted_iota(jnp.int32, (CHUNK, width), 1) % CHUNK
    return row >= col, row > col, row <= col, row < col, (row == col).astype(F32)


def _softplus(x):
    return jnp.maximum(x, 0.0) + jnp.log1p(jnp.exp(-jnp.abs(x)))


def _sigmoid(x):
    return 0.5 * jnp.tanh(0.5 * x) + 0.5


def _silu(x):
    return x * _sigmoid(x)


def _inproj_kernel(x_ref, g_ref, w_ref, o_ref, h_ref):
    @pl.when(pl.program_id(1) == 0)
    def _():
        x = x_ref[...]
        ms = jnp.mean(x * x, axis=-1, keepdims=True)
        h_ref[...] = (x * lax.rsqrt(ms + NORM_EPS) * g_ref[...]).astype(BF16)

    o_ref[...] = lax.dot_general(h_ref[...], w_ref[...], (((1,), (1,)), ((), ())),
                                 preferred_element_type=F32)


def _layer_spec(lyr, block, index):
    return pl.BlockSpec((None,) + block, lambda *grid_idx: (lyr,) + index(*grid_idx))


def _inproj(lyr, x2, g, w, tm=1024, tn=1536):
    m, d = x2.shape
    n = w.shape[1]
    tm = min(tm, m)
    return pl.pallas_call(
        _inproj_kernel,
        grid=(m // tm, n // tn),
        in_specs=[pl.BlockSpec((tm, d), lambda i, j: (i, 0)),
                  _layer_spec(lyr, (1, d), lambda i, j: (0, 0)),
                  _layer_spec(lyr, (tn, d), lambda i, j: (j, 0))],
        out_specs=pl.BlockSpec((tm, tn), lambda i, j: (i, j)),
        out_shape=jax.ShapeDtypeStruct((m, n), F32),
        scratch_shapes=[pltpu.VMEM((tm, d), BF16)],
        compiler_params=pltpu.CompilerParams(
            dimension_semantics=("parallel", "arbitrary"), vmem_limit_bytes=VMEM_LIMIT),
        name="inproj",
    )(x2, g, w)


def _outproj_kernel(ya_ref, yb_ref, yc_ref, x_ref, w_ref, fg_ref, o_ref, *, final_norm):
    wa = LRU_WIDTH
    wb = LRU_WIDTH + GDN_WIDTH
    acc = jnp.dot(ya_ref[...], w_ref[0:wa, :], preferred_element_type=F32)
    acc += jnp.dot(yb_ref[...], w_ref[wa:wb, :], preferred_element_type=F32)
    acc += jnp.dot(yc_ref[...], w_ref[wb:, :], preferred_element_type=F32)
    xn = x_ref[...] + acc
    if final_norm:
        ms = jnp.mean(xn * xn, axis=-1, keepdims=True)
        xn = xn * lax.rsqrt(ms + NORM_EPS) * fg_ref[...]
    o_ref[...] = xn


def _outproj(lyr, ya, yb, yc, x2, w, fg, final_norm, tm=512):
    m, d = x2.shape
    tm = min(tm, m)
    return pl.pallas_call(
        functools.partial(_outproj_kernel, final_norm=final_norm),
        grid=(m // tm,),
        in_specs=[pl.BlockSpec((tm, LRU_WIDTH), lambda i: (i, 0)),
                  pl.BlockSpec((tm, GDN_WIDTH), lambda i: (i, 0)),
                  pl.BlockSpec((tm, RWKV_WIDTH), lambda i: (i, 0)),
                  pl.BlockSpec((tm, d), lambda i: (i, 0)),
                  _layer_spec(lyr, (d, d), lambda i: (0, 0)),
                  pl.BlockSpec((1, d), lambda i: (0, 0))],
        out_specs=pl.BlockSpec((tm, d), lambda i: (i, 0)),
        out_shape=jax.ShapeDtypeStruct((m, d), F32),
        compiler_params=pltpu.CompilerParams(
            dimension_semantics=("parallel",), vmem_limit_bytes=VMEM_LIMIT),
        name="outproj",
    )(ya, yb, yc, x2, w, fg)


def _lru_kernel(x_ref, g_ref, cw_ref, cb_ref, wx_ref, bx_ref, wa_ref, ba_ref, lam_ref,
                y_ref, ext_ref, h_ref):
    ts, w = x_ref.shape

    @pl.when(pl.program_id(1) == 0)
    def _():
        ext_ref[0:8, :] = jnp.zeros((8, w), F32)
        h_ref[...] = jnp.zeros_like(h_ref)

    ext_ref[8:, :] = x_ref[...]
    cw = cw_ref[...]
    xc = cb_ref[...] + cw[0:1, :] * ext_ref[pl.ds(5, ts), :]
    for j in range(1, CONV_W):
        xc = xc + cw[j:j + 1, :] * ext_ref[pl.ds(5 + j, ts), :]
    ext_ref[0:8, :] = ext_ref[pl.ds(ts, 8), :]

    gate_x = _sigmoid(_dot(xc, wx_ref[...]) + bx_ref[...])
    gate_a = _sigmoid(_dot(xc, wa_ref[...]) + ba_ref[...])
    log_a = -LRU_C * gate_a * _softplus(-lam_ref[...])
    a = jnp.exp(log_a)
    mult = jnp.sqrt(jnp.maximum(-jnp.tanh(log_a) * (a * a + 1.0), 0.0))
    u = mult * (gate_x * xc)

    row = lax.broadcasted_iota(jnp.int32, (ts, w), 0)
    d = 1
    while d < ts:
        keep = row >= d
        u = jnp.where(keep, a * pltpu.roll(u, d, axis=0) + u, u)
        a = jnp.where(keep, a * pltpu.roll(a, d, axis=0), a)
        d *= 2
    h = u + a * h_ref[0:1, :]
    h_ref[0:1, :] = h[ts - 1:ts, :]
    y_ref[...] = (h * _silu(g_ref[...])).astype(y_ref.dtype)


def _lru(lyr, proj, bsz, seq, cw, cb, wx, bx, wa, ba, lam, ts=256):
    ns = seq // ts
    w = LRU_WIDTH
    vec = lambda: _layer_spec(lyr, (1, w), lambda b, s: (0, 0))
    return pl.pallas_call(
        _lru_kernel,
        grid=(bsz, ns),
        in_specs=[pl.BlockSpec((ts, w), lambda b, s: (b * ns + s, COL_AX * LANE // w)),
                  pl.BlockSpec((ts, w), lambda b, s: (b * ns + s, COL_AG * LANE // w)),
                  _layer_spec(lyr, (CONV_W, w), lambda b, s: (0, 0)),
                  vec(),
                  _layer_spec(lyr, (w, w), lambda b, s: (0, 0)),
                  vec(),
                  _layer_spec(lyr, (w, w), lambda b, s: (0, 0)),
                  vec(), vec()],
        out_specs=pl.BlockSpec((ts, w), lambda b, s: (b * ns + s, 0)),
        out_shape=jax.ShapeDtypeStruct((bsz * seq, w), BF16),
        scratch_shapes=[pltpu.VMEM((ts + 8, w), F32), pltpu.VMEM((8, w), F32)],
        compiler_params=pltpu.CompilerParams(
            dimension_semantics=("parallel", "arbitrary"), vmem_limit_bytes=VMEM_LIMIT),
        name="lru",
    )(proj, proj, cw, cb, wx, bx, wa, ba, lam)


def _gdn_prep_kernel(q_ref, k_ref, v_ref, ba_ref, cwq_ref, cwk_ref, cwv_ref, gp_ref,
                     u_ref, w_ref, qd_ref, kd_ref, qk_ref, gl_ref,
                     ext_ref, qs_ref, ks_ref, vs_ref, bt_ref, gg_ref, *, group):
    head = pl.program_id(1)
    seq, dh = q_ref.shape
    c = CHUNK
    n_chunks = seq // c
    group = min(group, n_chunks)

    def conv_silu(x_ref, cw_ref):
        ext_ref[0:8, :] = jnp.zeros((8, dh), F32)
        ext_ref[8:, :] = x_ref[...]
        cw = cw_ref[...]
        xc = cw[0:1, :] * ext_ref[pl.ds(5, seq), :]
        for j in range(1, CONV_W):
            xc = xc + cw[j:j + 1, :] * ext_ref[pl.ds(5 + j, seq), :]
        return _silu(xc)

    def l2norm(t):
        return t * lax.rsqrt(jnp.sum(t * t, axis=-1, keepdims=True) + 1e-6)

    qs_ref[...] = l2norm(conv_silu(q_ref, cwq_ref)) * (GDN_HEAD ** -0.5)
    ks_ref[...] = l2norm(conv_silu(k_ref, cwk_ref))
    vs_ref[...] = conv_silu(v_ref, cwv_ref)

    ba = ba_ref[...]
    lane = lax.broadcasted_iota(jnp.int32, ba.shape, 1)
    gp = gp_ref[...]
    beta_all = _sigmoid(ba)
    g_all = -jnp.exp(gp[0:1, :]) * _softplus(ba + gp[1:2, :])
    beta_col = jnp.sum(jnp.where(lane == head, beta_all, 0.0), axis=1, keepdims=True)
    g_col = jnp.sum(jnp.where(lane == head + GDN_HEADS, g_all, 0.0), axis=1, keepdims=True)
    bt_ref[...] = jnp.broadcast_to(beta_col, (seq, dh))
    gg_ref[...] = jnp.broadcast_to(g_col, (seq, dh))

    incl = _chunk_masks()[0]
    incl2, _, upper2, strict_upper2, eye2 = _chunk_masks(2 * c)
    first_copy = lax.broadcasted_iota(jnp.int32, (c, dh), 1) < c
    ltri = incl.astype(BF16)
    ltri3 = jnp.concatenate([ltri, ltri, ltri], axis=1)

    def prepare(n, carry):
        rng = range(group)
        sls = [pl.ds(pl.multiple_of((n * group + i) * c, c), c) for i in rng]
        qs = [qs_ref[s, :] for s in sls]
        ks = [ks_ref[s, :] for s in sls]
        betas = [bt_ref[s, :] for s in sls]
        decs = [_cumsum_rows(ltri3, gg_ref[s, :]) for s in sls]
        kbs = [k * b for k, b in zip(ks, betas)]
        gram_ts = [_dot_nt(k, jnp.concatenate([kb, kb], axis=0)) for k, kb in zip(ks, kbs)]
        gram_qs = [_dot_nt(q, jnp.concatenate([k, k], axis=0)) for q, k in zip(qs, ks)]
        lmasks, lmask_ts = [], []
        for dec in decs:
            diff = dec - jnp.concatenate([dec, dec], axis=0).T[0:c, :]
            lmasks.append(jnp.where(incl2, jnp.exp(jnp.where(incl2, diff, 0.0)), 0.0))
            lmask_ts.append(jnp.where(upper2, jnp.exp(jnp.where(upper2, -diff, 0.0)), 0.0))
        t_mat_ts = _tri_inverse_t([jnp.where(strict_upper2, -(g * m), 0.0)
                                   for g, m in zip(gram_ts, lmask_ts)], eye2)
        edecs = [jnp.exp(dec) for dec in decs]
        uws = [_dot_tn(x, jnp.concatenate([vs_ref[s, :] * b, kb * e], axis=1))[c:, :]
               for x, s, b, kb, e in zip(t_mat_ts, sls, betas, kbs, edecs)]
        for i in rng:
            s = sls[i]
            dec = decs[i]
            dlast = dec[c - 1:c, :]
            u_ref[s, :] = uws[i][:, 0:dh]
            w_ref[s, :] = uws[i][:, dh:].astype(BF16)
            qk_ref[s, :] = jnp.where(first_copy, gram_qs[i] * lmasks[i], 0.0).astype(BF16)
            qd_ref[s, :] = (qs[i] * edecs[i]).astype(BF16)
            kd_ref[s, :] = (ks[i] * jnp.exp(dlast - dec)).astype(BF16)
            gl_ref[pl.ds(pl.multiple_of((n * group + i) * 8, 8), 8), :] = jnp.broadcast_to(
                jnp.exp(dlast), (8, dh))
        return carry

    lax.fori_loop(0, n_chunks // group, prepare, 0)


def _gdn_prep(lyr, proj, bsz, seq, cw, gp, group=32):
    dh = GDN_HEAD
    n8 = seq // CHUNK * 8
    col = lambda base: pl.BlockSpec((seq, dh), lambda b, h: (b, base + h))
    cwspec = lambda base: _layer_spec(lyr, (CONV_W, dh), lambda b, h: (0, base + h))
    out = lambda: pl.BlockSpec((seq, dh), lambda b, h: (b, h))
    full = lambda: pltpu.VMEM((seq, dh), F32)
    sds = lambda dt: jax.ShapeDtypeStruct((bsz * seq, GDN_WIDTH), dt)
    return pl.pallas_call(
        functools.partial(_gdn_prep_kernel, group=group),
        grid=(bsz, GDN_HEADS),
        in_specs=[col(COL_BQ), col(COL_BK), col(COL_BV),
                  pl.BlockSpec((seq, dh), lambda b, h: (b, COL_BBA)),
                  cwspec(0), cwspec(GDN_HEADS), cwspec(2 * GDN_HEADS),
                  _layer_spec(lyr, (2, dh), lambda b, h: (0, 0))],
        out_specs=[out(), out(), out(), out(), out(),
                   pl.BlockSpec((n8, dh), lambda b, h: (b, h))],
        out_shape=[sds(F32), sds(BF16), sds(BF16), sds(BF16), sds(BF16),
                   jax.ShapeDtypeStruct((bsz * n8, GDN_WIDTH), F32)],
        scratch_shapes=[pltpu.VMEM((seq + 8, dh), F32), full(), full(), full(), full(), full()],
        compiler_params=pltpu.CompilerParams(
            dimension_semantics=("parallel", "parallel"), vmem_limit_bytes=VMEM_LIMIT),
        name="gdn_prep",
    )(proj, proj, proj, proj, cw, cw, cw, gp)


def _gdn_scan_kernel(u_ref, w_ref, qd_ref, kd_ref, qk_ref, gl_ref, gate_ref, ng_ref,
                     y_ref, state_ref, o_ref):
    rows = u_ref.shape[0]
    c = CHUNK
    dh = GDN_HEAD

    @pl.when(pl.program_id(1) == 0)
    def _():
        state_ref[...] = jnp.zeros_like(state_ref)

    heads = range(GDN_HEADS)
    lanes = [slice(h * dh, (h + 1) * dh) for h in heads]

    def step(n, carry):
        sl = pl.ds(pl.multiple_of(n * c, c), c)
        gsl = pl.ds(pl.multiple_of(n * 8, 8), 8)
        states = [state_ref[h] for h in heads]
        wss = [jnp.dot(jnp.concatenate([w_ref[sl, ls], qd_ref[sl, ls]], axis=0), st.astype(BF16),
                       preferred_element_type=F32) for ls, st in zip(lanes, states)]
        v_news = [(u_ref[sl, ls] - ws[0:c, :]).astype(BF16) for ls, ws in zip(lanes, wss)]
        upds = [lax.dot_general(kd_ref[sl, ls], vn, (((0,), (0,)), ((), ())), preferred_element_type=F32)
                for ls, vn in zip(lanes, v_news)]
        outs = [jnp.dot(qk_ref[sl, ls][:, 0:c], vn, preferred_element_type=F32)
                for ls, vn in zip(lanes, v_news)]
        for h in heads:
            ls = lanes[h]
            state_ref[h] = states[h] * gl_ref[gsl, ls][0:1, :] + upds[h]
            o_ref[sl, ls] = wss[h][c:, :] + outs[h]
        return carry

    lax.fori_loop(0, rows // c, step, 0)

    ng = ng_ref[...]
    for h in range(GDN_HEADS):
        ls = slice(h * dh, (h + 1) * dh)
        o = o_ref[:, ls]
        o = o * lax.rsqrt(jnp.mean(o * o, axis=-1, keepdims=True) + NORM_EPS) * ng
        y_ref[:, ls] = (o * _silu(gate_ref[:, ls])).astype(y_ref.dtype)


def _gdn_scan(lyr, u, w, qd, kd, qk, gl, proj, bsz, seq, ng):
    rows = min(SCAN_ROWS, seq)
    ns = seq // rows
    g8 = rows // CHUNK * 8
    wd = GDN_WIDTH
    blk = lambda: pl.BlockSpec((rows, wd), lambda b, s: (b * ns + s, 0))
    return pl.pallas_call(
        _gdn_scan_kernel,
        grid=(bsz, ns),
        in_specs=[blk(), blk(), blk(), blk(), blk(),
                  pl.BlockSpec((g8, wd), lambda b, s: (b * ns + s, 0)),
                  pl.BlockSpec((rows, wd), lambda b, s: (b * ns + s, COL_BG * LANE // wd)),
                  _layer_spec(lyr, (1, GDN_HEAD), lambda b, s: (0, 0))],
        out_specs=blk(),
        out_shape=jax.ShapeDtypeStruct((bsz * seq, wd), BF16),
        scratch_shapes=[pltpu.VMEM((GDN_HEADS, GDN_HEAD, GDN_HEAD), F32),
                        pltpu.VMEM((rows, wd), F32)],
        compiler_params=pltpu.CompilerParams(
            dimension_semantics=("parallel", "arbitrary"), vmem_limit_bytes=VMEM_LIMIT),
        name="gdn_scan",
    )(u, w, qd, kd, qk, gl, proj, ng)


def _pair_mask():
    r = lax.broadcasted_iota(jnp.int32, (LANE, LANE), 0) // RWKV_HEAD
    c = lax.broadcasted_iota(jnp.int32, (LANE, LANE), 1) // RWKV_HEAD
    return r == c


def _rwkv_prep_kernel(r_ref, k_ref, v_ref, wl_ref, al_ref,
                      mur_ref, muk_ref, muv_ref, muwl_ref, mual_ref,
                      w0_ref, wup_ref, a0_ref, aup_ref, kk_ref, ka_ref, rk_ref,
                      u0_ref, o0_ref, wt_ref, rt_ref, mrb_ref, b2_ref, kv_ref, pc_ref, bonus_ref,
                      rs_ref, k2_ref, vs_ref, kn_ref, kna_ref, lw_ref, *, group):
    seq, dw = r_ref.shape
    c = CHUNK
    hd = RWKV_HEAD
    n_chunks = seq // c
    group = min(group, n_chunks)

    row0 = lax.broadcasted_iota(jnp.int32, (seq, dw), 0) == 0

    def shift(x_ref, mu_ref):
        x = x_ref[...]
        prev = jnp.where(row0, 0.0, pltpu.roll(x, 1, axis=0))
        return x + (prev - x) * mu_ref[...]

    r = shift(r_ref, mur_ref)
    k = shift(k_ref, muk_ref)
    v = shift(v_ref, muv_ref)
    wl = shift(wl_ref, muwl_ref)
    al = shift(al_ref, mual_ref)

    seg = _pair_mask().astype(BF16)
    seg2 = jnp.concatenate([seg, seg], axis=0)

    w_log = -_softplus(-(w0_ref[...] + _dot(jnp.tanh(wl), wup_ref[...]))) - 0.5
    lw_ref[...] = -jnp.exp(w_log)
    a = _sigmoid(a0_ref[...] + _dot(al, aup_ref[...]))
    kn = k * kk_ref[...]
    kn = kn * lax.rsqrt(_segsum(kn * kn, seg2) + 1e-6)
    k2 = k * (1.0 + (a - 1.0) * ka_ref[...])
    rs_ref[...] = r
    k2_ref[...] = k2
    vs_ref[...] = v
    kn_ref[...] = kn
    kna_ref[...] = kn * a
    bonus_ref[...] = _segsum(r * k2 * rk_ref[...], seg2) * v

    incl, strict = _chunk_masks()[0:2]
    incl2, _, _, strict_upper2, eye2 = _chunk_masks(2 * c)
    ltri = incl.astype(BF16)
    ltri3 = jnp.concatenate([ltri, ltri, ltri], axis=1)
    head0 = lax.broadcasted_iota(jnp.int32, (1, dw), 1) < hd
    head_masks = [head0, jnp.logical_not(head0)]

    def prepare(n, carry):
        sls = [pl.ds(pl.multiple_of((n * group + i) * c, c), c) for i in range(group)]
        lws = [lw_ref[s, :] for s in sls]
        cums = [_cumsum_rows(ltri3, lw) for lw in lws]
        knas = [kna_ref[s, :] for s in sls]
        k2s = [k2_ref[s, :] for s in sls]
        vvs = [vs_ref[s, :] for s in sls]
        einvs = [jnp.exp(-cum) for cum in cums]
        a_ts = [-kn_ref[s, :] * jnp.exp(cum - lw) for s, cum, lw in zip(sls, cums, lws)]
        r_ts = [rs_ref[s, :] * jnp.exp(cum) for s, cum in zip(sls, cums)]
        b_ts = [(kna * e).astype(BF16) for kna, e in zip(knas, einvs)]
        a_bs = [a.astype(BF16) for a in a_ts]
        r_bs = [r.astype(BF16) for r in r_ts]
        zero = jnp.zeros((), BF16)
        aas = [jnp.concatenate([a, a], axis=0) for a in a_bs]
        bbs = [jnp.concatenate([b, b], axis=0) for b in b_ts]
        kks = [jnp.concatenate([k2c * e] * 2, axis=0).astype(BF16) for k2c, e in zip(k2s, einvs)]
        prob = [(i, m) for i in range(group) for m in head_masks]
        a_ab_ts = [_dot_nt(jnp.where(m, b_ts[i], zero), aas[i]) for i, m in prob]
        m_rbs = [_dot_nt(jnp.where(m, r_bs[i], zero), bbs[i]) for i, m in prob]
        g_ks = [_dot_nt(jnp.where(m, jnp.concatenate([a_bs[i], r_bs[i]], axis=0), zero), kks[i])
                for i, m in prob]
        t_mat_ts = _tri_inverse_t([jnp.where(strict_upper2, g, 0.0) for g in a_ab_ts], eye2)
        akvs = [_dot(jnp.where(strict, g[0:c, 0:c], 0.0), vvs[i]) for g, (i, _) in zip(g_ks, prob)]
        tws = [_dot_tn(x, jnp.concatenate([a_ts[i], akv], axis=1))[c:, :]
               for x, akv, (i, _) in zip(t_mat_ts, akvs, prob)]
        o0s = [_dot(jnp.where(incl, g[c:, 0:c], 0.0), vvs[i]) for g, (i, _) in zip(g_ks, prob)]
        for i in range(group):
            s = sls[i]
            h0, h1 = 2 * i, 2 * i + 1
            pick = lambda x0, x1: jnp.where(head0, x0, x1)
            clast = cums[i][c - 1:c, :]
            tail = jnp.exp(clast - cums[i])
            kv_full = _dot_tn(k2s[i] * tail, vvs[i])
            pc_full = jnp.broadcast_to(jnp.exp(clast), (dw, dw)).T
            u0_ref[s, :] = pick(tws[h0][:, dw:], tws[h1][:, dw:])
            o0_ref[s, :] = pick(o0s[h0], o0s[h1])
            wt_ref[s, :] = pick(tws[h0][:, 0:dw], tws[h1][:, 0:dw]).astype(BF16)
            rt_ref[s, :] = r_bs[i]
            mrb_ref[s, :] = jnp.where(incl2, pick(m_rbs[h0], m_rbs[h1]), 0.0).astype(BF16)
            b2_ref[s, :] = (knas[i] * tail).astype(BF16)
            kv_ref[s, :] = pick(kv_full[0:c, :], kv_full[c:, :])
            pc_ref[s, :] = pick(pc_full[0:c, :], pc_full[c:, :])
        return carry

    lax.fori_loop(0, n_chunks // group, prepare, 0)


def _rwkv_prep(lyr, proj, bsz, seq, mu, w0, wup, a0, aup, kk, ka, rk, group=16):
    dw = LANE
    np_ = RWKV_PAIRS
    col = lambda base: pl.BlockSpec((seq, dw), lambda b, p: (b, base + p))
    fixed_col = lambda idx: pl.BlockSpec((seq, dw), lambda b, p: (b, idx))
    vec = lambda base: _layer_spec(lyr, (1, dw), lambda b, p: (0, base + p))
    fixed_vec = lambda idx: _layer_spec(lyr, (1, dw), lambda b, p: (0, idx))
    lora = lambda: _layer_spec(lyr, (dw, dw), lambda b, p: (0, p))
    out = lambda: pl.BlockSpec((seq, dw), lambda b, p: (b, p))
    full = lambda: pltpu.VMEM((seq, dw), F32)
    sds = lambda dt: jax.ShapeDtypeStruct((bsz * seq, RWKV_WIDTH), dt)
    return pl.pallas_call(
        functools.partial(_rwkv_prep_kernel, group=group),
        grid=(bsz, np_),
        in_specs=[col(COL_CR), col(COL_CK), col(COL_CV), fixed_col(COL_CWL), fixed_col(COL_CAL),
                  vec(0), vec(np_), vec(2 * np_), fixed_vec(3 * np_), fixed_vec(3 * np_ + 1),
                  vec(0), lora(), vec(0), lora(), vec(0), vec(0), vec(0)],
        out_specs=[out() for _ in range(9)],
        out_shape=[sds(F32), sds(F32), sds(BF16), sds(BF16), sds(BF16), sds(BF16),
                   sds(F32), sds(F32), sds(F32)],
        scratch_shapes=[full(), full(), full(), full(), full(), full()],
        compiler_params=pltpu.CompilerParams(
            dimension_semantics=("parallel", "parallel"), vmem_limit_bytes=VMEM_LIMIT),
        name="rwkv_prep",
    )(proj, proj, proj, proj, proj, mu, mu, mu, mu, mu, w0, wup, a0, aup, kk, ka, rk)


def _rwkv_scan_kernel(u0_ref, o0_ref, wt_ref, rt_ref, mrb_ref, b2_ref, kv_ref, pc_ref, bonus_ref,
                      gate_ref, gnw_ref, gnb_ref, y_ref, state_ref, o_ref):
    rows = u0_ref.shape[0]
    c = CHUNK
    dw = LANE
    pair = _pair_mask()

    @pl.when(pl.program_id(1) == 0)
    def _():
        state_ref[...] = jnp.zeros_like(state_ref)

    def block_diag(x):
        return jnp.where(pair, jnp.concatenate([x, x], axis=0), 0.0)

    pairs = range(RWKV_PAIRS)
    lanes = [slice(p * dw, (p + 1) * dw) for p in pairs]

    def step(n, carry):
        sl = pl.ds(pl.multiple_of(n * c, c), c)
        hms = [state_ref[p] for p in pairs]
        whs = [jnp.dot(jnp.concatenate([wt_ref[sl, ls], rt_ref[sl, ls]], axis=0), hm.astype(BF16),
                       preferred_element_type=F32) for ls, hm in zip(lanes, hms)]
        us = [u0_ref[sl, ls] + wh[0:c, :] for ls, wh in zip(lanes, whs)]
        upds = [lax.dot_general(b2_ref[sl, ls], u.astype(BF16), (((0,), (0,)), ((), ())),
                                preferred_element_type=F32) for ls, u in zip(lanes, us)]
        outs = [jnp.dot(mrb_ref[sl, ls], block_diag(u).astype(BF16), preferred_element_type=F32)
                for ls, u in zip(lanes, us)]
        for p in pairs:
            ls = lanes[p]
            pc = pc_ref[sl, ls]
            state_ref[p] = (jnp.concatenate([pc, pc], axis=0) * hms[p]
                            + jnp.where(pair, upds[p], 0.0) + block_diag(kv_ref[sl, ls]))
            o_ref[sl, ls] = whs[p][c:, :] + outs[p] + o0_ref[sl, ls]
        return carry

    lax.fori_loop(0, rows // c, step, 0)

    seg = pair.astype(BF16)
    seg2 = jnp.concatenate([seg, seg], axis=0)
    inv_n = 1.0 / RWKV_HEAD
    for p in range(RWKV_PAIRS):
        ls = slice(p * dw, (p + 1) * dw)
        o = o_ref[:, ls]
        cen = o - _segsum(o, seg2) * inv_n
        var = _segsum(cen * cen, seg2) * inv_n
        yn = cen * lax.rsqrt(var + RWKV_GN_EPS) * gnw_ref[:, ls] + gnb_ref[:, ls]
        y_ref[:, ls] = ((yn + bonus_ref[:, ls]) * _silu(gate_ref[:, ls])).astype(y_ref.dtype)


def _rwkv_scan(lyr, prep, proj, bsz, seq, gnw, gnb):
    rows = min(SCAN_ROWS, seq)
    ns = seq // rows
    wd = RWKV_WIDTH
    blk = lambda: pl.BlockSpec((rows, wd), lambda b, s: (b * ns + s, 0))
    vec = lambda: _layer_spec(lyr, (1, wd), lambda b, s: (0, 0))
    return pl.pallas_call(
        _rwkv_scan_kernel,
        grid=(bsz, ns),
        in_specs=[blk() for _ in range(9)]
        + [pl.BlockSpec((rows, wd), lambda b, s: (b * ns + s, COL_CG * LANE // wd)), vec(), vec()],
        out_specs=blk(),
        out_shape=jax.ShapeDtypeStruct((bsz * seq, wd), BF16),
        scratch_shapes=[pltpu.VMEM((RWKV_PAIRS, LANE, LANE), F32),
                        pltpu.VMEM((rows, wd), F32)],
        compiler_params=pltpu.CompilerParams(
            dimension_semantics=("parallel", "arbitrary"), vmem_limit_bytes=VMEM_LIMIT),
        name="rwkv_scan",
    )(*prep, proj, gnw, gnb)


def _pack_source(j):
    a1 = 2 * LRU_WIDTH
    b1 = a1 + 4 * GDN_WIDTH
    c0 = b1 + 2 * GDN_HEADS
    c1 = c0 + 3 * RWKV_WIDTH
    c2 = c1 + 2 * LORA
    start, valid = jnp.int32(0), jnp.int32(0)
    for first, count, src, width in ((COL_BQ, COL_CR - COL_BQ, a1, LANE),
                                     (COL_CR, COL_CG - COL_CR, c0, LANE),
                                     (COL_CG, COL_AX - COL_CG, c2, LANE),
                                     (COL_AX, COL_BBA - COL_AX, 0, LANE),
                                     (COL_BBA, 1, b1, 2 * GDN_HEADS),
                                     (COL_CWL, 1, c1, LORA),
                                     (COL_CAL, 1, c1 + LORA, LORA)):
        inside = (j >= first) & (j < first + count)
        start = jnp.where(inside, src + (j - first) * LANE, start)
        valid = jnp.where(inside, width, valid)
    return start, valid


def _pack_w_in_kernel(w_ref, o_ref):
    _, valid = _pack_source(pl.program_id(0))
    keep = lax.broadcasted_iota(jnp.int32, (w_ref.shape[0], w_ref.shape[2]), 0) < valid
    for lyr in range(w_ref.shape[1]):
        o_ref[lyr] = jnp.where(keep, w_ref[:, lyr, :], 0.0).astype(BF16)


def _pack_w_in(w_in):
    lyr, d, n = w_in.shape
    w_t = jnp.transpose(w_in, (2, 0, 1))
    return pl.pallas_call(
        _pack_w_in_kernel,
        grid=(N_PACKED // LANE,),
        in_specs=[pl.BlockSpec((pl.Element(LANE), pl.Element(lyr), pl.Element(d)),
                               lambda j: (_pack_source(j)[0], 0, 0))],
        out_specs=pl.BlockSpec((lyr, LANE, d), lambda j: (0, j, 0)),
        out_shape=jax.ShapeDtypeStruct((lyr, N_PACKED, d), BF16),
        compiler_params=pltpu.CompilerParams(
            dimension_semantics=("parallel",), vmem_limit_bytes=VMEM_LIMIT),
        name="pack_w_in",
    )(w_t)


def _pack_mu(mu):
    lyr = mu.shape[0]
    z = jnp.zeros((lyr, LANE - LORA), mu.dtype)
    c1 = 3 * RWKV_WIDTH
    return jnp.concatenate([mu[:, :c1], mu[:, c1:c1 + LORA], z, mu[:, c1 + LORA:], z], axis=-1)


def _block_diag(w):
    lyr, nb, bi, bj = w.shape
    eye = jnp.eye(nb, dtype=w.dtype)
    return jnp.einsum('lnij,nm->lnimj', w, eye).reshape(lyr, nb * bi, nb * bj)


def kernel(x, norm_g, w_in, w_out, lru_conv_w, lru_conv_b, lru_wx, lru_bx, lru_wa, lru_ba, lru_lambda,
           gdn_conv_w, gdn_a_log, gdn_dt_bias, gdn_norm_g, rwkv_mu, rwkv_w0, rwkv_w_up, rwkv_a0,
           rwkv_a_up, rwkv_k_k, rwkv_k_a, rwkv_r_k, rwkv_gn_w, rwkv_gn_b, final_norm_g):
    bsz, seq, d = x.shape
    depth = w_in.shape[0]
    x2 = x.reshape(bsz * seq, d)

    w_in_p = _pack_w_in(w_in)
    w_out_b = w_out.astype(BF16)
    wx_bd = _block_diag(lru_wx).astype(BF16)
    wa_bd = _block_diag(lru_wa).astype(BF16)
    mu_p = _pack_mu(rwkv_mu)
    pad_rows = lambda w: jnp.pad(w, ((0, 0), (0, LANE - LORA), (0, 0))).astype(BF16)
    wup_p = pad_rows(rwkv_w_up)
    aup_p = pad_rows(rwkv_a_up)
    gp = jnp.zeros((depth, 2, LANE), F32)
    gp = gp.at[:, 0, GDN_HEADS:2 * GDN_HEADS].set(gdn_a_log)
    gp = gp.at[:, 1, GDN_HEADS:2 * GDN_HEADS].set(gdn_dt_bias)
    rows = lambda t: t.reshape(depth, 1, -1)
    norm_g, lru_conv_b, lru_bx, lru_ba, lru_lambda, gdn_norm_g = map(
        rows, (norm_g, lru_conv_b, lru_bx, lru_ba, lru_lambda, gdn_norm_g))
    mu_p, rwkv_w0, rwkv_a0, rwkv_k_k, rwkv_k_a, rwkv_r_k, rwkv_gn_w, rwkv_gn_b = map(
        rows, (mu_p, rwkv_w0, rwkv_a0, rwkv_k_k, rwkv_k_a, rwkv_r_k, rwkv_gn_w, rwkv_gn_b))
    final_g = final_norm_g.reshape(1, -1)

    for l in range(depth):
        proj = _inproj(l, x2, norm_g, w_in_p)
        ya = _lru(l, proj, bsz, seq, lru_conv_w, lru_conv_b, wx_bd, lru_bx, wa_bd, lru_ba, lru_lambda)
        gdn_ops = _gdn_prep(l, proj, bsz, seq, gdn_conv_w, gp)
        yb = _gdn_scan(l, *gdn_ops, proj, bsz, seq, gdn_norm_g)
        rwkv_ops = _rwkv_prep(l, proj, bsz, seq, mu_p, rwkv_w0, wup_p, rwkv_a0, aup_p,
                              rwkv_k_k, rwkv_k_a, rwkv_r_k)
        yc = _rwkv_scan(l, rwkv_ops, proj, bsz, seq, rwkv_gn_w, rwkv_gn_b)
        x2 = _outproj(l, ya, yb, yc, x2, w_out_b, final_g, final_norm=(l == depth - 1))
    return x2.reshape(bsz, seq, d)
```

```python
import functools

import jax
import jax.numpy as jnp
from jax import lax
from jax.experimental import pallas as pl
from jax.experimental.pallas import tpu as pltpu

F32 = jnp.float32
BF16 = jnp.bfloat16

NORM_EPS = 1e-6
CONV_W = 4
CHUNK = 64
LRU_WIDTH = 512
LRU_C = 8.0
GDN_HEAD = 128
GDN_HEADS = 6
GDN_WIDTH = GDN_HEADS * GDN_HEAD
RWKV_HEAD = 64
RWKV_WIDTH = 768
LORA = 96
RWKV_GN_EPS = 64e-5

LANE = 128
RWKV_PAIRS = RWKV_WIDTH // LANE
N_PACKED = 7680
COL_BQ, COL_BK, COL_BV, COL_BG = 0, 6, 12, 18
COL_CR, COL_CK, COL_CV, COL_CG = 24, 30, 36, 42
COL_AX, COL_AG = 48, 52
COL_BBA, COL_CWL, COL_CAL = 56, 57, 58
SCAN_ROWS = 512

VMEM_LIMIT = 52 * 1024 * 1024


def _dot(a, b):
    return jnp.dot(a.astype(BF16), b.astype(BF16), preferred_element_type=F32)


def _dot_nt(a, b):
    return lax.dot_general(a.astype(BF16), b.astype(BF16), (((1,), (1,)), ((), ())),
                           preferred_element_type=F32)


def _dot_tn(a, b):
    return lax.dot_general(a.astype(BF16), b.astype(BF16), (((0,), (0,)), ((), ())),
                           preferred_element_type=F32)


def _split2(x):
    hi = x.astype(BF16)
    lo = (x - hi.astype(F32)).astype(BF16)
    return hi, lo


def _split3(x):
    hi = x.astype(BF16)
    r1 = x - hi.astype(F32)
    mid = r1.astype(BF16)
    lo = (r1 - mid.astype(F32)).astype(BF16)
    return hi, mid, lo


def _cumsum_rows(ltri3, x):
    return jnp.dot(ltri3, jnp.concatenate(_split3(x), axis=0), preferred_element_type=F32)


def _segsum(x, seg2):
    return jnp.dot(jnp.concatenate(_split2(x), axis=1), seg2, preferred_element_type=F32)


def _tri_inverse_t(a_twins, eye_twin):
    c, w = eye_twin.shape
    first_half = lax.broadcasted_iota(jnp.int32, (c, w), 1) < c
    xs = [jnp.where(first_half, a, eye_twin) for a in a_twins]
    for _ in range(6):
        xbs = [x.astype(BF16) for x in xs]
        outs = [jnp.dot(xb[:, 0:c], xb, preferred_element_type=F32) for xb in xbs]
        xs = [jnp.where(first_half, o, x + o) for x, o in zip(xs, outs)]
    return xs


def _chunk_masks(width=CHUNK):
    row = lax.broadcasted_iota(jnp.int32, (CHUNK, width), 0)
    col = lax.broadcasted_iota(jnp.int32, (CHUNK, width), 1) % CHUNK
    return row >= col, row > col, row <= col, row < col, (row == col).astype(F32)


def _softplus(x):
    return jnp.maximum(x, 0.0) + jnp.log1p(jnp.exp(-jnp.abs(x)))


def _sigmoid(x):
    return 0.5 * jnp.tanh(0.5 * x) + 0.5


def _silu(x):
    return x * _sigmoid(x)


def _inproj_kernel(x_ref, g_ref, w_ref, o_ref, h_ref):
    @pl.when(pl.program_id(1) == 0)
    def _():
        x = x_ref[...]
        ms = jnp.mean(x * x, axis=-1, keepdims=True)
        h_ref[...] = (x * lax.rsqrt(ms + NORM_EPS) * g_ref[...]).astype(BF16)

    o_ref[...] = lax.dot_general(h_ref[...], w_ref[...], (((1,), (1,)), ((), ())),
                                 preferred_element_type=F32)


def _layer_spec(lyr, block, index):
    return pl.BlockSpec((None,) + block, lambda *grid_idx: (lyr,) + index(*grid_idx))


def _inproj(lyr, x2, g, w, tm=1024, tn=1536):
    m, d = x2.shape
    n = w.shape[1]
    tm = min(tm, m)
    return pl.pallas_call(
        _inproj_kernel,
        grid=(m // tm, n // tn),
        in_specs=[pl.BlockSpec((tm, d), lambda i, j: (i, 0)),
                  _layer_spec(lyr, (1, d), lambda i, j: (0, 0)),
                  _layer_spec(lyr, (tn, d), lambda i, j: (j, 0))],
        out_specs=pl.BlockSpec((tm, tn), lambda i, j: (i, j)),
        out_shape=jax.ShapeDtypeStruct((m, n), F32),
        scratch_shapes=[pltpu.VMEM((tm, d), BF16)],
        compiler_params=pltpu.CompilerParams(
            dimension_semantics=("parallel", "arbitrary"), vmem_limit_bytes=VMEM_LIMIT),
        name="inproj",
    )(x2, g, w)


def _outproj_kernel(ya_ref, yb_ref, yc_ref, x_ref, w_ref, fg_ref, o_ref, *, final_norm):
    wa = LRU_WIDTH
    wb = LRU_WIDTH + GDN_WIDTH
    acc = jnp.dot(ya_ref[...], w_ref[0:wa, :], preferred_element_type=F32)
    acc += jnp.dot(yb_ref[...], w_ref[wa:wb, :], preferred_element_type=F32)
    acc += jnp.dot(yc_ref[...], w_ref[wb:, :], preferred_element_type=F32)
    xn = x_ref[...] + acc
    if final_norm:
        ms = jnp.mean(xn * xn, axis=-1, keepdims=True)
        xn = xn * lax.rsqrt(ms + NORM_EPS) * fg_ref[...]
    o_ref[...] = xn


def _outproj(lyr, ya, yb, yc, x2, w, fg, final_norm, tm=512):
    m, d = x2.shape
    tm = min(tm, m)
    return pl.pallas_call(
        functools.partial(_outproj_kernel, final_norm=final_norm),
        grid=(m // tm,),
        in_specs=[pl.BlockSpec((tm, LRU_WIDTH), lambda i: (i, 0)),
                  pl.BlockSpec((tm, GDN_WIDTH), lambda i: (i, 0)),
                  pl.BlockSpec((tm, RWKV_WIDTH), lambda i: (i, 0)),
                  pl.BlockSpec((tm, d), lambda i: (i, 0)),
                  _layer_spec(lyr, (d, d), lambda i: (0, 0)),
                  pl.BlockSpec((1, d), lambda i: (0, 0))],
        out_specs=pl.BlockSpec((tm, d), lambda i: (i, 0)),
        out_shape=jax.ShapeDtypeStruct((m, d), F32),
        compiler_params=pltpu.CompilerParams(
            dimension_semantics=("parallel",), vmem_limit_bytes=VMEM_LIMIT),
        name="outproj",
    )(ya, yb, yc, x2, w, fg)


def _lru_kernel(x_ref, g_ref, cw_ref, cb_ref, wx_ref, bx_ref, wa_ref, ba_ref, lam_ref,
                y_ref, ext_ref, h_ref):
    ts, w = x_ref.shape

    @pl.when(pl.program_id(1) == 0)
    def _():
        ext_ref[0:8, :] = jnp.zeros((8, w), F32)
        h_ref[...] = jnp.zeros_like(h_ref)

    ext_ref[8:, :] = x_ref[...]
    cw = cw_ref[...]
    xc = cb_ref[...] + cw[0:1, :] * ext_ref[pl.ds(5, ts), :]
    for j in range(1, CONV_W):
        xc = xc + cw[j:j + 1, :] * ext_ref[pl.ds(5 + j, ts), :]
    ext_ref[0:8, :] = ext_ref[pl.ds(ts, 8), :]

    gate_x = _sigmoid(_dot(xc, wx_ref[...]) + bx_ref[...])
    gate_a = _sigmoid(_dot(xc, wa_ref[...]) + ba_ref[...])
    log_a = -LRU_C * gate_a * _softplus(-lam_ref[...])
    a = jnp.exp(log_a)
    mult = jnp.sqrt(jnp.maximum(-jnp.tanh(log_a) * (a * a + 1.0), 0.0))
    u = mult * (gate_x * xc)

    row = lax.broadcasted_iota(jnp.int32, (ts, w), 0)
    d = 1
    while d < ts:
        keep = row >= d
        u = jnp.where(keep, a * pltpu.roll(u, d, axis=0) + u, u)
        a = jnp.where(keep, a * pltpu.roll(a, d, axis=0), a)
        d *= 2
    h = u + a * h_ref[0:1, :]
    h_ref[0:1, :] = h[ts - 1:ts, :]
    y_ref[...] = (h * _silu(g_ref[...])).astype(y_ref.dtype)


def _lru(lyr, proj, bsz, seq, cw, cb, wx, bx, wa, ba, lam, ts=256):
    ns = seq // ts
    w = LRU_WIDTH
    vec = lambda: _layer_spec(lyr, (1, w), lambda b, s: (0, 0))
    return pl.pallas_call(
        _lru_kernel,
        grid=(bsz, ns),
        in_specs=[pl.BlockSpec((ts, w), lambda b, s: (b * ns + s, COL_AX * LANE // w)),
                  pl.BlockSpec((ts, w), lambda b, s: (b * ns + s, COL_AG * LANE // w)),
                  _layer_spec(lyr, (CONV_W, w), lambda b, s: (0, 0)),
                  vec(),
                  _layer_spec(lyr, (w, w), lambda b, s: (0, 0)),
                  vec(),
                  _layer_spec(lyr, (w, w), lambda b, s: (0, 0)),
                  vec(), vec()],
        out_specs=pl.BlockSpec((ts, w), lambda b, s: (b * ns + s, 0)),
        out_shape=jax.ShapeDtypeStruct((bsz * seq, w), BF16),
        scratch_shapes=[pltpu.VMEM((ts + 8, w), F32), pltpu.VMEM((8, w), F32)],
        compiler_params=pltpu.CompilerParams(
            dimension_semantics=("parallel", "arbitrary"), vmem_limit_bytes=VMEM_LIMIT),
        name="lru",
    )(proj, proj, cw, cb, wx, bx, wa, ba, lam)


def _gdn_prep_kernel(q_ref, k_ref, v_ref, ba_ref, cwq_ref, cwk_ref, cwv_ref, gp_ref,
                     u_ref, w_ref, qd_ref, kd_ref, qk_ref, gl_ref,
                     ext_ref, qs_ref, ks_ref, vs_ref, bt_ref, gg_ref, *, group):
    head = pl.program_id(1)
    seq, dh = q_ref.shape
    c = CHUNK
    n_chunks = seq // c
    group = min(group, n_chunks)

    def conv_silu(x_ref, cw_ref):
        ext_ref[0:8, :] = jnp.zeros((8, dh), F32)
        ext_ref[8:, :] = x_ref[...]
        cw = cw_ref[...]
        xc = cw[0:1, :] * ext_ref[pl.ds(5, seq), :]
        for j in range(1, CONV_W):
            xc = xc + cw[j:j + 1, :] * ext_ref[pl.ds(5 + j, seq), :]
        return _silu(xc)

    def l2norm(t):
        return t * lax.rsqrt(jnp.sum(t * t, axis=-1, keepdims=True) + 1e-6)

    qs_ref[...] = l2norm(conv_silu(q_ref, cwq_ref)) * (GDN_HEAD ** -0.5)
    ks_ref[...] = l2norm(conv_silu(k_ref, cwk_ref))
    vs_ref[...] = conv_silu(v_ref, cwv_ref)

    ba = ba_ref[...]
    lane = lax.broadcasted_iota(jnp.int32, ba.shape, 1)
    gp = gp_ref[...]
    beta_all = _sigmoid(ba)
    g_all = -jnp.exp(gp[0:1, :]) * _softplus(ba + gp[1:2, :])
    beta_col = jnp.sum(jnp.where(lane == head, beta_all, 0.0), axis=1, keepdims=True)
    g_col = jnp.sum(jnp.where(lane == head + GDN_HEADS, g_all, 0.0), axis=1, keepdims=True)
    bt_ref[...] = jnp.broadcast_to(beta_col, (seq, dh))
    gg_ref[...] = jnp.broadcast_to(g_col, (seq, dh))

    incl = _chunk_masks()[0]
    incl2, _, upper2, strict_upper2, eye2 = _chunk_masks(2 * c)
    first_copy = lax.broadcasted_iota(jnp.int32, (c, dh), 1) < c
    ltri = incl.astype(BF16)
    ltri3 = jnp.concatenate([ltri, ltri, ltri], axis=1)

    def prepare(n, carry):
        rng = range(group)
        sls = [pl.ds(pl.multiple_of((n * group + i) * c, c), c) for i in rng]
        qs = [qs_ref[s, :] for s in sls]
        ks = [ks_ref[s, :] for s in sls]
        betas = [bt_ref[s, :] for s in sls]
        decs = [_cumsum_rows(ltri3, gg_ref[s, :]) for s in sls]
        kbs = [k * b for k, b in zip(ks, betas)]
        gram_ts = [_dot_nt(k, jnp.concatenate([kb, kb], axis=0)) for k, kb in zip(ks, kbs)]
        gram_qs = [_dot_nt(q, jnp.concatenate([k, k], axis=0)) for q, k in zip(qs, ks)]
        lmasks, lmask_ts = [], []
        for dec in decs:
            diff = dec - jnp.concatenate([dec, dec], axis=0).T[0:c, :]
            lmasks.append(jnp.where(incl2, jnp.exp(jnp.where(incl2, diff, 0.0)), 0.0))
            lmask_ts.append(jnp.where(upper2, jnp.exp(jnp.where(upper2, -diff, 0.0)), 0.0))
        t_mat_ts = _tri_inverse_t([jnp.where(strict_upper2, -(g * m), 0.0)
                                   for g, m in zip(gram_ts, lmask_ts)], eye2)
        edecs = [jnp.exp(dec) for dec in decs]
        uws = [_dot_tn(x, jnp.concatenate([vs_ref[s, :] * b, kb * e], axis=1))[c:, :]
               for x, s, b, kb, e in zip(t_mat_ts, sls, betas, kbs, edecs)]
        for i in rng:
            s = sls[i]
            dec = decs[i]
            dlast = dec[c - 1:c, :]
            u_ref[s, :] = uws[i][:, 0:dh]
            w_ref[s, :] = uws[i][:, dh:].astype(BF16)
            qk_ref[s, :] = jnp.where(first_copy, gram_qs[i] * lmasks[i], 0.0).astype(BF16)
            qd_ref[s, :] = (qs[i] * edecs[i]).astype(BF16)
            kd_ref[s, :] = (ks[i] * jnp.exp(dlast - dec)).astype(BF16)
            gl_ref[pl.ds(pl.multiple_of((n * group + i) * 8, 8), 8), :] = jnp.broadcast_to(
                jnp.exp(dlast), (8, dh))
        return carry

    lax.fori_loop(0, n_chunks // group, prepare, 0)


def _gdn_prep(lyr, proj, bsz, seq, cw, gp, group=32):
    dh = GDN_HEAD
    n8 = seq // CHUNK * 8
    col = lambda base: pl.BlockSpec((seq, dh), lambda b, h: (b, base + h))
    cwspec = lambda base: _layer_spec(lyr, (CONV_W, dh), lambda b, h: (0, base + h))
    out = lambda: pl.BlockSpec((seq, dh), lambda b, h: (b, h))
    full = lambda: pltpu.VMEM((seq, dh), F32)
    sds = lambda dt: jax.ShapeDtypeStruct((bsz * seq, GDN_WIDTH), dt)
    return pl.pallas_call(
        functools.partial(_gdn_prep_kernel, group=group),
        grid=(bsz, GDN_HEADS),
        in_specs=[col(COL_BQ), col(COL_BK), col(COL_BV),
                  pl.BlockSpec((seq, dh), lambda b, h: (b, COL_BBA)),
                  cwspec(0), cwspec(GDN_HEADS), cwspec(2 * GDN_HEADS),
                  _layer_spec(lyr, (2, dh), lambda b, h: (0, 0))],
        out_specs=[out(), out(), out(), out(), out(),
                   pl.BlockSpec((n8, dh), lambda b, h: (b, h))],
        out_shape=[sds(F32), sds(BF16), sds(BF16), sds(BF16), sds(BF16),
                   jax.ShapeDtypeStruct((bsz * n8, GDN_WIDTH), F32)],
        scratch_shapes=[pltpu.VMEM((seq + 8, dh), F32), full(), full(), full(), full(), full()],
        compiler_params=pltpu.CompilerParams(
            dimension_semantics=("parallel", "parallel"), vmem_limit_bytes=VMEM_LIMIT),
        name="gdn_prep",
    )(proj, proj, proj, proj, cw, cw, cw, gp)


def _gdn_scan_kernel(u_ref, w_ref, qd_ref, kd_ref, qk_ref, gl_ref, gate_ref, ng_ref,
                     y_ref, state_ref, o_ref):
    rows = u_ref.shape[0]
    c = CHUNK
    dh = GDN_HEAD

    @pl.when(pl.program_id(1) == 0)
    def _():
        state_ref[...] = jnp.zeros_like(state_ref)

    heads = range(GDN_HEADS)
    lanes = [slice(h * dh, (h + 1) * dh) for h in heads]

    def step(n, carry):
        sl = pl.ds(pl.multiple_of(n * c, c), c)
        gsl = pl.ds(pl.multiple_of(n * 8, 8), 8)
        states = [state_ref[h] for h in heads]
        wss = [jnp.dot(jnp.concatenate([w_ref[sl, ls], qd_ref[sl, ls]], axis=0), st.astype(BF16),
                       preferred_element_type=F32) for ls, st in zip(lanes, states)]
        v_news = [(u_ref[sl, ls] - ws[0:c, :]).astype(BF16) for ls, ws in zip(lanes, wss)]
        upds = [lax.dot_general(kd_ref[sl, ls], vn, (((0,), (0,)), ((), ())), preferred_element_type=F32)
                for ls, vn in zip(lanes, v_news)]
        outs = [jnp.dot(qk_ref[sl, ls][:, 0:c], vn, preferred_element_type=F32)
                for ls, vn in zip(lanes, v_news)]
        for h in heads:
            ls = lanes[h]
            state_ref[h] = states[h] * gl_ref[gsl, ls][0:1, :] + upds[h]
            o_ref[sl, ls] = wss[h][c:, :] + outs[h]
        return carry

    lax.fori_loop(0, rows // c, step, 0)

    ng = ng_ref[...]
    for h in range(GDN_HEADS):
        ls = slice(h * dh, (h + 1) * dh)
        o = o_ref[:, ls]
        o = o * lax.rsqrt(jnp.mean(o * o, axis=-1, keepdims=True) + NORM_EPS) * ng
        y_ref[:, ls] = (o * _silu(gate_ref[:, ls])).astype(y_ref.dtype)


def _gdn_scan(lyr, u, w, qd, kd, qk, gl, proj, bsz, seq, ng):
    rows = min(SCAN_ROWS, seq)
    ns = seq // rows
    g8 = rows // CHUNK * 8
    wd = GDN_WIDTH
    blk = lambda: pl.BlockSpec((rows, wd), lambda b, s: (b * ns + s, 0))
    return pl.pallas_call(
        _gdn_scan_kernel,
        grid=(bsz, ns),
        in_specs=[blk(), blk(), blk(), blk(), blk(),
                  pl.BlockSpec((g8, wd), lambda b, s: (b * ns + s, 0)),
                  pl.BlockSpec((rows, wd), lambda b, s: (b * ns + s, COL_BG * LANE // wd)),
                  _layer_spec(lyr, (1, GDN_HEAD), lambda b, s: (0, 0))],
        out_specs=blk(),
        out_shape=jax.ShapeDtypeStruct((bsz * seq, wd), BF16),
        scratch_shapes=[pltpu.VMEM((GDN_HEADS, GDN_HEAD, GDN_HEAD), F32),
                        pltpu.VMEM((rows, wd), F32)],
        compiler_params=pltpu.CompilerParams(
            dimension_semantics=("parallel", "arbitrary"), vmem_limit_bytes=VMEM_LIMIT),
        name="gdn_scan",
    )(u, w, qd, kd, qk, gl, proj, ng)


def _pair_mask():
    r = lax.broadcasted_iota(jnp.int32, (LANE, LANE), 0) // RWKV_HEAD
    c = lax.broadcasted_iota(jnp.int32, (LANE, LANE), 1) // RWKV_HEAD
    return r == c


def _rwkv_prep_kernel(r_ref, k_ref, v_ref, wl_ref, al_ref,
                      mur_ref, muk_ref, muv_ref, muwl_ref, mual_ref,
                      w0_ref, wup_ref, a0_ref, aup_ref, kk_ref, ka_ref, rk_ref,
                      u0_ref, o0_ref, wt_ref, rt_ref, mrb_ref, b2_ref, kv_ref, pc_ref, bonus_ref,
                      rs_ref, k2_ref, vs_ref, kn_ref, kna_ref, lw_ref, *, group):
    seq, dw = r_ref.shape
    c = CHUNK
    hd = RWKV_HEAD
    n_chunks = seq // c
    group = min(group, n_chunks)

    seg = _pair_mask().astype(BF16)
    seg2 = jnp.concatenate([seg, seg], axis=0)

    def gates(r0, r1):
        rows = slice(r0, r1)
        first_row = lax.broadcasted_iota(jnp.int32, (r1 - r0, dw), 0) == 0

        def shift(x_ref, mu_ref):
            x = x_ref[rows, :]
            before = jnp.zeros((1, dw), F32) if r0 == 0 else x_ref[r0 - 8:r0, :][7:8, :]
            prev = jnp.where(first_row, before, pltpu.roll(x, 1, axis=0))
            return x + (prev - x) * mu_ref[...]

        r = shift(r_ref, mur_ref)
        k = shift(k_ref, muk_ref)
        v = shift(v_ref, muv_ref)
        wl = shift(wl_ref, muwl_ref)
        al = shift(al_ref, mual_ref)
        w_log = -_softplus(-(w0_ref[...] + _dot(jnp.tanh(wl), wup_ref[...]))) - 0.5
        lw_ref[rows, :] = -jnp.exp(w_log)
        a = _sigmoid(a0_ref[...] + _dot(al, aup_ref[...]))
        kn = k * kk_ref[...]
        kn = kn * lax.rsqrt(_segsum(kn * kn, seg2) + 1e-6)
        k2 = k * (1.0 + (a - 1.0) * ka_ref[...])
        rs_ref[rows, :] = r
        k2_ref[rows, :] = k2
        vs_ref[rows, :] = v
        kn_ref[rows, :] = kn
        kna_ref[rows, :] = kn * a
        bonus_ref[rows, :] = _segsum(r * k2 * rk_ref[...], seg2) * v

    incl = _chunk_masks()[0]
    incl2, strict2, _, strict_upper2, eye2 = _chunk_masks(2 * c)
    ltri = incl.astype(BF16)
    ltri3 = jnp.concatenate([ltri, ltri, ltri], axis=1)
    head0w = lax.broadcasted_iota(jnp.int32, (1, 2 * dw), 1) % dw < hd
    head0 = head0w[:, 0:dw]
    head_masks = [head0, jnp.logical_not(head0)]

    def prepare(n):
        sls = [slice((n * group + i) * c, (n * group + i + 1) * c) for i in range(group)]
        lws = [lw_ref[s, :] for s in sls]
        cums = [_cumsum_rows(ltri3, lw) for lw in lws]
        knas = [kna_ref[s, :] for s in sls]
        k2s = [k2_ref[s, :] for s in sls]
        vvs = [vs_ref[s, :] for s in sls]
        einvs = [jnp.exp(-cum) for cum in cums]
        a_ts = [-kn_ref[s, :] * jnp.exp(cum - lw) for s, cum, lw in zip(sls, cums, lws)]
        r_ts = [rs_ref[s, :] * jnp.exp(cum) for s, cum in zip(sls, cums)]
        b_ts = [(kna * e).astype(BF16) for kna, e in zip(knas, einvs)]
        a_bs = [a.astype(BF16) for a in a_ts]
        r_bs = [r.astype(BF16) for r in r_ts]
        zero = jnp.zeros((), BF16)

        def by_head(x):
            zx = jnp.zeros((), x.dtype)
            return jnp.concatenate([jnp.where(head0w[:, 0:x.shape[1]], x, zx),
                                    jnp.where(head0w[:, 0:x.shape[1]], zx, x)], axis=0)

        aas = [jnp.concatenate([a, a], axis=0) for a in a_bs]
        prob = [(i, m) for i in range(group) for m in head_masks]
        a_ab_ts = [_dot_nt(jnp.where(m, b_ts[i], zero), aas[i]) for i, m in prob]
        m_rbs = [_dot_nt(r, by_head(b)) for r, b in zip(r_bs, b_ts)]
        g_ks = [_dot_nt(jnp.concatenate([a, r], axis=0), by_head((k2c * e).astype(BF16)))
                for a, r, k2c, e in zip(a_bs, r_bs, k2s, einvs)]
        t_mat_ts = _tri_inverse_t([jnp.where(strict_upper2, g, 0.0) for g in a_ab_ts], eye2)
        vbds = [by_head(v.astype(BF16)) for v in vvs]
        akvs = [_dot(jnp.where(strict2, g[0:c, :], 0.0), vbd) for g, vbd in zip(g_ks, vbds)]
        o0s = [_dot(jnp.where(incl2, g[c:, :], 0.0), vbd) for g, vbd in zip(g_ks, vbds)]
        tws = [_dot_tn(jnp.concatenate([t_mat_ts[2 * i], t_mat_ts[2 * i + 1]], axis=0),
                       by_head(jnp.concatenate([a_ts[i], akvs[i]], axis=1)))[c:, :]
               for i in range(group)]
        for i in range(group):
            s = sls[i]
            pick = lambda x0, x1: jnp.where(head0, x0, x1)
            clast = cums[i][c - 1:c, :]
            tail = jnp.exp(clast - cums[i])
            kv_full = _dot_tn(k2s[i] * tail, vvs[i])
            pc_full = jnp.broadcast_to(jnp.exp(clast), (dw, dw)).T
            u0_ref[s, :] = tws[i][:, dw:]
            o0_ref[s, :] = o0s[i]
            wt_ref[s, :] = tws[i][:, 0:dw].astype(BF16)
            rt_ref[s, :] = r_bs[i]
            mrb_ref[s, :] = jnp.where(incl2, m_rbs[i], 0.0).astype(BF16)
            b2_ref[s, :] = (knas[i] * tail).astype(BF16)
            kv_ref[s, :] = pick(kv_full[0:c, :], kv_full[c:, :])
            pc_ref[s, :] = pick(pc_full[0:c, :], pc_full[c:, :])

    for part in range(n_chunks // group):
        gates(part * group * c, (part + 1) * group * c)
        prepare(part)


def _rwkv_prep(lyr, proj, bsz, seq, mu, w0, wup, a0, aup, kk, ka, rk, group=16):
    dw = LANE
    np_ = RWKV_PAIRS
    col = lambda base: pl.BlockSpec((seq, dw), lambda b, p: (b, base + p))
    fixed_col = lambda idx: pl.BlockSpec((seq, dw), lambda b, p: (b, idx))
    vec = lambda base: _layer_spec(lyr, (1, dw), lambda b, p: (0, base + p))
    fixed_vec = lambda idx: _layer_spec(lyr, (1, dw), lambda b, p: (0, idx))
    lora = lambda: _layer_spec(lyr, (dw, dw), lambda b, p: (0, p))
    out = lambda: pl.BlockSpec((seq, dw), lambda b, p: (b, p))
    full = lambda: pltpu.VMEM((seq, dw), F32)
    sds = lambda dt: jax.ShapeDtypeStruct((bsz * seq, RWKV_WIDTH), dt)
    return pl.pallas_call(
        functools.partial(_rwkv_prep_kernel, group=group),
        grid=(bsz, np_),
        in_specs=[col(COL_CR), col(COL_CK), col(COL_CV), fixed_col(COL_CWL), fixed_col(COL_CAL),
                  vec(0), vec(np_), vec(2 * np_), fixed_vec(3 * np_), fixed_vec(3 * np_ + 1),
                  vec(0), lora(), vec(0), lora(), vec(0), vec(0), vec(0)],
        out_specs=[out() for _ in range(9)],
        out_shape=[sds(F32), sds(F32), sds(BF16), sds(BF16), sds(BF16), sds(BF16),
                   sds(F32), sds(F32), sds(F32)],
        scratch_shapes=[full(), full(), full(), full(), full(), full()],
        compiler_params=pltpu.CompilerParams(
            dimension_semantics=("parallel", "parallel"), vmem_limit_bytes=VMEM_LIMIT),
        name="rwkv_prep",
    )(proj, proj, proj, proj, proj, mu, mu, mu, mu, mu, w0, wup, a0, aup, kk, ka, rk)


def _rwkv_scan_kernel(u0_ref, o0_ref, wt_ref, rt_ref, mrb_ref, b2_ref, kv_ref, pc_ref, bonus_ref,
                      gate_ref, gnw_ref, gnb_ref, y_ref, state_ref, o_ref):
    rows = u0_ref.shape[0]
    c = CHUNK
    dw = LANE
    pair = _pair_mask()

    @pl.when(pl.program_id(1) == 0)
    def _():
        state_ref[...] = jnp.zeros_like(state_ref)

    def block_diag(x):
        return jnp.where(pair, jnp.concatenate([x, x], axis=0), 0.0)

    pairs = range(RWKV_PAIRS)
    lanes = [slice(p * dw, (p + 1) * dw) for p in pairs]

    def step(n, carry):
        sl = pl.ds(pl.multiple_of(n * c, c), c)
        hms = [state_ref[p] for p in pairs]
        whs = [jnp.dot(jnp.concatenate([wt_ref[sl, ls], rt_ref[sl, ls]], axis=0), hm.astype(BF16),
                       preferred_element_type=F32) for ls, hm in zip(lanes, hms)]
        us = [u0_ref[sl, ls] + wh[0:c, :] for ls, wh in zip(lanes, whs)]
        upds = [lax.dot_general(b2_ref[sl, ls], u.astype(BF16), (((0,), (0,)), ((), ())),
                                preferred_element_type=F32) for ls, u in zip(lanes, us)]
        outs = [jnp.dot(mrb_ref[sl, ls], block_diag(u).astype(BF16), preferred_element_type=F32)
                for ls, u in zip(lanes, us)]
        for p in pairs:
            ls = lanes[p]
            pc = pc_ref[sl, ls]
            state_ref[p] = (jnp.concatenate([pc, pc], axis=0) * hms[p]
                            + jnp.where(pair, upds[p], 0.0) + block_diag(kv_ref[sl, ls]))
            o_ref[sl, ls] = whs[p][c:, :] + outs[p] + o0_ref[sl, ls]
        return carry

    lax.fori_loop(0, rows // c, step, 0)

    seg = pair.astype(BF16)
    seg2 = jnp.concatenate([seg, seg], axis=0)
    inv_n = 1.0 / RWKV_HEAD
    for p in range(RWKV_PAIRS):
        ls = slice(p * dw, (p + 1) * dw)
        o = o_ref[:, ls]
        cen = o - _segsum(o, seg2) * inv_n
        var = _segsum(cen * cen, seg2) * inv_n
        yn = cen * lax.rsqrt(var + RWKV_GN_EPS) * gnw_ref[:, ls] + gnb_ref[:, ls]
        y_ref[:, ls] = ((yn + bonus_ref[:, ls]) * _silu(gate_ref[:, ls])).astype(y_ref.dtype)


def _rwkv_scan(lyr, prep, proj, bsz, seq, gnw, gnb):
    rows = min(SCAN_ROWS, seq)
    ns = seq // rows
    wd = RWKV_WIDTH
    blk = lambda: pl.BlockSpec((rows, wd), lambda b, s: (b * ns + s, 0))
    vec = lambda: _layer_spec(lyr, (1, wd), lambda b, s: (0, 0))
    return pl.pallas_call(
        _rwkv_scan_kernel,
        grid=(bsz, ns),
        in_specs=[blk() for _ in range(9)]
        + [pl.BlockSpec((rows, wd), lambda b, s: (b * ns + s, COL_CG * LANE // wd)), vec(), vec()],
        out_specs=blk(),
        out_shape=jax.ShapeDtypeStruct((bsz * seq, wd), BF16),
        scratch_shapes=[pltpu.VMEM((RWKV_PAIRS, LANE, LANE), F32),
                        pltpu.VMEM((rows, wd), F32)],
        compiler_params=pltpu.CompilerParams(
            dimension_semantics=("parallel", "arbitrary"), vmem_limit_bytes=VMEM_LIMIT),
        name="rwkv_scan",
    )(*prep, proj, gnw, gnb)


def _pack_source(j):
    a1 = 2 * LRU_WIDTH
    b1 = a1 + 4 * GDN_WIDTH
    c0 = b1 + 2 * GDN_HEADS
    c1 = c0 + 3 * RWKV_WIDTH
    c2 = c1 + 2 * LORA
    start, valid = jnp.int32(0), jnp.int32(0)
    for first, count, src, width in ((COL_BQ, COL_CR - COL_BQ, a1, LANE),
                                     (COL_CR, COL_CG - COL_CR, c0, LANE),
                                     (COL_CG, COL_AX - COL_CG, c2, LANE),
                                     (COL_AX, COL_BBA - COL_AX, 0, LANE),
                                     (COL_BBA, 1, b1, 2 * GDN_HEADS),
                                     (COL_CWL, 1, c1, LORA),
                                     (COL_CAL, 1, c1 + LORA, LORA)):
        inside = (j >= first) & (j < first + count)
        start = jnp.where(inside, src + (j - first) * LANE, start)
        valid = jnp.where(inside, width, valid)
    return start, valid


def _pack_w_in_kernel(w_ref, o_ref):
    _, valid = _pack_source(pl.program_id(0))
    keep = lax.broadcasted_iota(jnp.int32, (w_ref.shape[0], w_ref.shape[2]), 0) < valid
    for lyr in range(w_ref.shape[1]):
        o_ref[lyr] = jnp.where(keep, w_ref[:, lyr, :], 0.0).astype(BF16)


def _pack_w_in(w_in):
    lyr, d, n = w_in.shape
    w_t = jnp.transpose(w_in, (2, 0, 1))
    return pl.pallas_call(
        _pack_w_in_kernel,
        grid=(N_PACKED // LANE,),
        in_specs=[pl.BlockSpec((pl.Element(LANE), pl.Element(lyr), pl.Element(d)),
                               lambda j: (_pack_source(j)[0], 0, 0))],
        out_specs=pl.BlockSpec((lyr, LANE, d), lambda j: (0, j, 0)),
        out_shape=jax.ShapeDtypeStruct((lyr, N_PACKED, d), BF16),
        compiler_params=pltpu.CompilerParams(
            dimension_semantics=("parallel",), vmem_limit_bytes=VMEM_LIMIT),
        name="pack_w_in",
    )(w_t)


def _pack_mu(mu):
    lyr = mu.shape[0]
    z = jnp.zeros((lyr, LANE - LORA), mu.dtype)
    c1 = 3 * RWKV_WIDTH
    return jnp.concatenate([mu[:, :c1], mu[:, c1:c1 + LORA], z, mu[:, c1 + LORA:], z], axis=-1)


def _block_diag(w):
    lyr, nb, bi, bj = w.shape
    eye = jnp.eye(nb, dtype=w.dtype)
    return jnp.einsum('lnij,nm->lnimj', w, eye).reshape(lyr, nb * bi, nb * bj)


def kernel(x, norm_g, w_in, w_out, lru_conv_w, lru_conv_b, lru_wx, lru_bx, lru_wa, lru_ba, lru_lambda,
           gdn_conv_w, gdn_a_log, gdn_dt_bias, gdn_norm_g, rwkv_mu, rwkv_w0, rwkv_w_up, rwkv_a0,
           rwkv_a_up, rwkv_k_k, rwkv_k_a, rwkv_r_k, rwkv_gn_w, rwkv_gn_b, final_norm_g):
    bsz, seq, d = x.shape
    depth = w_in.shape[0]
    x2 = x.reshape(bsz * seq, d)

    w_in_p = _pack_w_in(w_in)
    w_out_b = w_out.astype(BF16)
    wx_bd = _block_diag(lru_wx).astype(BF16)
    wa_bd = _block_diag(lru_wa).astype(BF16)
    mu_p = _pack_mu(rwkv_mu)
    pad_rows = lambda w: jnp.pad(w, ((0, 0), (0, LANE - LORA), (0, 0))).astype(BF16)
    wup_p = pad_rows(rwkv_w_up)
    aup_p = pad_rows(rwkv_a_up)
    gp = jnp.zeros((depth, 2, LANE), F32)
    gp = gp.at[:, 0, GDN_HEADS:2 * GDN_HEADS].set(gdn_a_log)
    gp = gp.at[:, 1, GDN_HEADS:2 * GDN_HEADS].set(gdn_dt_bias)
    rows = lambda t: t.reshape(depth, 1, -1)
    norm_g, lru_conv_b, lru_bx, lru_ba, lru_lambda, gdn_norm_g = map(
        rows, (norm_g, lru_conv_b, lru_bx, lru_ba, lru_lambda, gdn_norm_g))
    mu_p, rwkv_w0, rwkv_a0, rwkv_k_k, rwkv_k_a, rwkv_r_k, rwkv_gn_w, rwkv_gn_b = map(
        rows, (mu_p, rwkv_w0, rwkv_a0, rwkv_k_k, rwkv_k_a, rwkv_r_k, rwkv_gn_w, rwkv_gn_b))
    final_g = final_norm_g.reshape(1, -1)

    for l in range(depth):
        proj = _inproj(l, x2, norm_g, w_in_p)
        ya = _lru(l, proj, bsz, seq, lru_conv_w, lru_conv_b, wx_bd, lru_bx, wa_bd, lru_ba, lru_lambda)
        gdn_ops = _gdn_prep(l, proj, bsz, seq, gdn_conv_w, gp)
        yb = _gdn_scan(l, *gdn_ops, proj, bsz, seq, gdn_norm_g)
        rwkv_ops = _rwkv_prep(l, proj, bsz, seq, mu_p, rwkv_w0, wup_p, rwkv_a0, aup_p,
                              rwkv_k_k, rwkv_k_a, rwkv_r_k)
        yc = _rwkv_scan(l, rwkv_ops, proj, bsz, seq, rwkv_gn_w, rwkv_gn_b)
        x2 = _outproj(l, ya, yb, yc, x2, w_out_b, final_g, final_norm=(l == depth - 1))
    return x2.reshape(bsz, seq, d)
```

```python
import functools
import math

import jax
import jax.numpy as jnp
from jax import lax
from jax.experimental import pallas as pl
from jax.experimental.pallas import tpu as pltpu

F32 = jnp.float32
BF16 = jnp.bfloat16

NORM_EPS = 1e-6
CONV_W = 4
CHUNK = 64
LRU_WIDTH = 512
LRU_C = 8.0
GDN_HEAD = 128
GDN_HEADS = 6
GDN_WIDTH = GDN_HEADS * GDN_HEAD
RWKV_HEAD = 64
RWKV_WIDTH = 768
LORA = 96
RWKV_GN_EPS = 64e-5

LANE = 128
RWKV_PAIRS = RWKV_WIDTH // LANE
N_PACKED = 7680
COL_BQ, COL_BK, COL_BV, COL_BG = 0, 6, 12, 18
COL_CR, COL_CK, COL_CV, COL_CG = 24, 30, 36, 42
COL_AX, COL_AG = 48, 52
COL_BBA, COL_CWL, COL_CAL = 56, 57, 58
SCAN_ROWS = 512

VMEM_LIMIT = 52 * 1024 * 1024


def _dot(a, b):
    return jnp.dot(a.astype(BF16), b.astype(BF16), preferred_element_type=F32)


def _dot_nt(a, b):
    return lax.dot_general(a.astype(BF16), b.astype(BF16), (((1,), (1,)), ((), ())),
                           preferred_element_type=F32)


def _dot_tn(a, b):
    return lax.dot_general(a.astype(BF16), b.astype(BF16), (((0,), (0,)), ((), ())),
                           preferred_element_type=F32)


def _split2(x):
    hi = x.astype(BF16)
    lo = (x - hi.astype(F32)).astype(BF16)
    return hi, lo


def _split3(x):
    hi = x.astype(BF16)
    r1 = x - hi.astype(F32)
    mid = r1.astype(BF16)
    lo = (r1 - mid.astype(F32)).astype(BF16)
    return hi, mid, lo


def _cumsum_rows(ltri3, x):
    return jnp.dot(ltri3, jnp.concatenate(_split3(x), axis=0), preferred_element_type=F32)


def _segsum(x, seg2):
    return jnp.dot(jnp.concatenate(_split2(x), axis=1), seg2, preferred_element_type=F32)


def _tri_inverse_t(a_twins, eye_twin):
    c, w = eye_twin.shape
    first_half = lax.broadcasted_iota(jnp.int32, (c, w), 1) < c
    xs = [jnp.where(first_half, a, eye_twin) for a in a_twins]
    for _ in range(6):
        xbs = [x.astype(BF16) for x in xs]
        outs = [jnp.dot(xb[:, 0:c], xb, preferred_element_type=F32) for xb in xbs]
        xs = [jnp.where(first_half, o, x + o) for x, o in zip(xs, outs)]
    return xs


def _chunk_masks(width=CHUNK):
    row = lax.broadcasted_iota(jnp.int32, (CHUNK, width), 0)
    col = lax.broadcasted_iota(jnp.int32, (CHUNK, width), 1) % CHUNK
    return row >= col, row > col, row <= col, row < col, (row == col).astype(F32)


def _softplus(x):
    return jnp.maximum(x, 0.0) + jnp.log1p(jnp.exp(-jnp.abs(x)))


def _sigmoid(x):
    return 0.5 * jnp.tanh(0.5 * x) + 0.5


def _silu(x):
    h = 0.5 * x
    return h * jnp.tanh(h) + h


def _inproj_kernel(x_ref, g_ref, w_ref, o_ref, h_ref):
    @pl.when(pl.program_id(1) == 0)
    def _():
        x = x_ref[...]
        ms = jnp.mean(x * x, axis=-1, keepdims=True)
        h_ref[...] = (x * lax.rsqrt(ms + NORM_EPS) * g_ref[...]).astype(BF16)

    o_ref[...] = lax.dot_general(h_ref[...], w_ref[...], (((1,), (1,)), ((), ())),
                                 preferred_element_type=F32)


def _layer_spec(lyr, block, index):
    return pl.BlockSpec((None,) + block, lambda *grid_idx: (lyr,) + index(*grid_idx))


def _inproj(lyr, x2, g, w, tm=1024, tn=1536):
    m, d = x2.shape
    n = w.shape[1]
    tm = min(tm, m)
    return pl.pallas_call(
        _inproj_kernel,
        grid=(m // tm, n // tn),
        in_specs=[pl.BlockSpec((tm, d), lambda i, j: (i, 0)),
                  _layer_spec(lyr, (1, d), lambda i, j: (0, 0)),
                  _layer_spec(lyr, (tn, d), lambda i, j: (j, 0))],
        out_specs=pl.BlockSpec((tm, tn), lambda i, j: (i, j)),
        out_shape=jax.ShapeDtypeStruct((m, n), F32),
        scratch_shapes=[pltpu.VMEM((tm, d), BF16)],
        compiler_params=pltpu.CompilerParams(
            dimension_semantics=("parallel", "arbitrary"), vmem_limit_bytes=VMEM_LIMIT),
        name="inproj",
    )(x2, g, w)


def _outproj_kernel(ya_ref, yb_ref, yc_ref, x_ref, w_ref, fg_ref, o_ref, *, final_norm):
    wa = LRU_WIDTH
    wb = LRU_WIDTH + GDN_WIDTH
    acc = jnp.dot(ya_ref[...], w_ref[0:wa, :], preferred_element_type=F32)
    acc += jnp.dot(yb_ref[...], w_ref[wa:wb, :], preferred_element_type=F32)
    acc += jnp.dot(yc_ref[...], w_ref[wb:, :], preferred_element_type=F32)
    xn = x_ref[...] + acc
    if final_norm:
        ms = jnp.mean(xn * xn, axis=-1, keepdims=True)
        xn = xn * lax.rsqrt(ms + NORM_EPS) * fg_ref[...]
    o_ref[...] = xn


def _outproj(lyr, ya, yb, yc, x2, w, fg, final_norm, tm=512):
    m, d = x2.shape
    tm = min(tm, m)
    return pl.pallas_call(
        functools.partial(_outproj_kernel, final_norm=final_norm),
        grid=(m // tm,),
        in_specs=[pl.BlockSpec((tm, LRU_WIDTH), lambda i: (i, 0)),
                  pl.BlockSpec((tm, GDN_WIDTH), lambda i: (i, 0)),
                  pl.BlockSpec((tm, RWKV_WIDTH), lambda i: (i, 0)),
                  pl.BlockSpec((tm, d), lambda i: (i, 0)),
                  _layer_spec(lyr, (d, d), lambda i: (0, 0)),
                  pl.BlockSpec((1, d), lambda i: (0, 0))],
        out_specs=pl.BlockSpec((tm, d), lambda i: (i, 0)),
        out_shape=jax.ShapeDtypeStruct((m, d), F32),
        compiler_params=pltpu.CompilerParams(
            dimension_semantics=("parallel",), vmem_limit_bytes=VMEM_LIMIT),
        name="outproj",
    )(ya, yb, yc, x2, w, fg)


def _lru_kernel(x_ref, g_ref, cw_ref, cb_ref, wx_ref, bx_ref, wa_ref, ba_ref, lam_ref,
                y_ref, ext_ref, h_ref):
    ts, w = x_ref.shape

    @pl.when(pl.program_id(1) == 0)
    def _():
        ext_ref[0:8, :] = jnp.zeros((8, w), F32)
        h_ref[...] = jnp.zeros_like(h_ref)

    ext_ref[8:, :] = x_ref[...]
    cw = cw_ref[...]
    xc = cb_ref[...] + cw[0:1, :] * ext_ref[pl.ds(5, ts), :]
    for j in range(1, CONV_W):
        xc = xc + cw[j:j + 1, :] * ext_ref[pl.ds(5 + j, ts), :]
    ext_ref[0:8, :] = ext_ref[pl.ds(ts, 8), :]

    gate_x = _sigmoid(_dot(xc, wx_ref[...]) + bx_ref[...])
    gate_a = _sigmoid(_dot(xc, wa_ref[...]) + ba_ref[...])
    log_a = -LRU_C * gate_a * _softplus(-lam_ref[...])
    a = jnp.exp(log_a)
    mult = jnp.sqrt(jnp.maximum(-jnp.tanh(log_a) * (a * a + 1.0), 0.0))
    u = mult * (gate_x * xc)

    row = lax.broadcasted_iota(jnp.int32, (ts, w), 0)
    d = 1
    while d < ts:
        keep = row >= d
        u = jnp.where(keep, a * pltpu.roll(u, d, axis=0) + u, u)
        a = jnp.where(keep, a * pltpu.roll(a, d, axis=0), a)
        d *= 2
    h = u + a * h_ref[0:1, :]
    h_ref[0:1, :] = h[ts - 1:ts, :]
    y_ref[...] = (h * _silu(g_ref[...])).astype(y_ref.dtype)


def _lru(lyr, proj, bsz, seq, cw, cb, wx, bx, wa, ba, lam, ts=512):
    ns = seq // ts
    w = LRU_WIDTH
    vec = lambda: _layer_spec(lyr, (1, w), lambda b, s: (0, 0))
    return pl.pallas_call(
        _lru_kernel,
        grid=(bsz, ns),
        in_specs=[pl.BlockSpec((ts, w), lambda b, s: (b * ns + s, COL_AX * LANE // w)),
                  pl.BlockSpec((ts, w), lambda b, s: (b * ns + s, COL_AG * LANE // w)),
                  _layer_spec(lyr, (CONV_W, w), lambda b, s: (0, 0)),
                  vec(),
                  _layer_spec(lyr, (w, w), lambda b, s: (0, 0)),
                  vec(),
                  _layer_spec(lyr, (w, w), lambda b, s: (0, 0)),
                  vec(), vec()],
        out_specs=pl.BlockSpec((ts, w), lambda b, s: (b * ns + s, 0)),
        out_shape=jax.ShapeDtypeStruct((bsz * seq, w), BF16),
        scratch_shapes=[pltpu.VMEM((ts + 8, w), F32), pltpu.VMEM((8, w), F32)],
        compiler_params=pltpu.CompilerParams(
            dimension_semantics=("parallel", "arbitrary"), vmem_limit_bytes=VMEM_LIMIT),
        name="lru",
    )(proj, proj, cw, cb, wx, bx, wa, ba, lam)


def _gdn_prep_kernel(q_ref, k_ref, v_ref, ba_ref, cwq_ref, cwk_ref, cwv_ref, gp_ref,
                     u_ref, w_ref, qd_ref, kd_ref, qk_ref, gl_ref,
                     ext_ref, qs_ref, ks_ref, vs_ref, bt_ref, gg_ref, *, group):
    head = pl.program_id(1)
    seq, dh = q_ref.shape
    c = CHUNK
    n_chunks = seq // c
    group = min(group, n_chunks)

    def conv_silu(x_ref, cw_ref):
        ext_ref[0:8, :] = jnp.zeros((8, dh), F32)
        ext_ref[8:, :] = x_ref[...]
        cw = cw_ref[...]
        xc = cw[0:1, :] * ext_ref[pl.ds(5, seq), :]
        for j in range(1, CONV_W):
            xc = xc + cw[j:j + 1, :] * ext_ref[pl.ds(5 + j, seq), :]
        return _silu(xc)

    def l2norm(t):
        return t * lax.rsqrt(jnp.sum(t * t, axis=-1, keepdims=True) + 1e-6)

    qs_ref[...] = l2norm(conv_silu(q_ref, cwq_ref)) * (GDN_HEAD ** -0.5)
    ks_ref[...] = l2norm(conv_silu(k_ref, cwk_ref))
    vs_ref[...] = conv_silu(v_ref, cwv_ref)

    ba = ba_ref[...]
    lane = lax.broadcasted_iota(jnp.int32, ba.shape, 1)
    gp = gp_ref[...]
    beta_all = _sigmoid(ba)
    g_all = -jnp.exp(gp[0:1, :]) * _softplus(ba + gp[1:2, :])
    beta_col = jnp.sum(jnp.where(lane == head, beta_all, 0.0), axis=1, keepdims=True)
    g_col = jnp.sum(jnp.where(lane == head + GDN_HEADS, g_all, 0.0), axis=1, keepdims=True)
    bt_ref[...] = jnp.broadcast_to(beta_col, (seq, dh))
    gg_ref[...] = jnp.broadcast_to(g_col, (seq, dh))

    incl = _chunk_masks()[0]
    incl2, _, upper2, strict_upper2, eye2 = _chunk_masks(2 * c)
    first_copy = lax.broadcasted_iota(jnp.int32, (c, dh), 1) < c
    ltri = incl.astype(BF16)
    ltri3 = jnp.concatenate([ltri, ltri, ltri], axis=1)

    def prepare(n, carry):
        rng = range(group)
        sls = [pl.ds(pl.multiple_of((n * group + i) * c, c), c) for i in rng]
        qs = [qs_ref[s, :] for s in sls]
        ks = [ks_ref[s, :] for s in sls]
        betas = [bt_ref[s, :] for s in sls]
        decs = [_cumsum_rows(ltri3, gg_ref[s, :]) for s in sls]
        kbs = [k * b for k, b in zip(ks, betas)]
        gram_ts = [_dot_nt(k, jnp.concatenate([kb, kb], axis=0)) for k, kb in zip(ks, kbs)]
        gram_qs = [_dot_nt(q, jnp.concatenate([k, k], axis=0)) for q, k in zip(qs, ks)]
        lmasks, lmask_ts = [], []
        for dec in decs:
            diff = dec - jnp.concatenate([dec, dec], axis=0).T[0:c, :]
            lmasks.append(jnp.where(incl2, jnp.exp(jnp.where(incl2, diff, 0.0)), 0.0))
            lmask_ts.append(jnp.where(upper2, jnp.exp(jnp.where(upper2, -diff, 0.0)), 0.0))
        t_mat_ts = _tri_inverse_t([jnp.where(strict_upper2, -(g * m), 0.0)
                                   for g, m in zip(gram_ts, lmask_ts)], eye2)
        edecs = [jnp.exp(dec) for dec in decs]
        uws = [_dot_tn(x, jnp.concatenate([vs_ref[s, :] * b, kb * e], axis=1))[c:, :]
               for x, s, b, kb, e in zip(t_mat_ts, sls, betas, kbs, edecs)]
        for i in rng:
            s = sls[i]
            dec = decs[i]
            dlast = dec[c - 1:c, :]
            u_ref[s, :] = uws[i][:, 0:dh]
            w_ref[s, :] = uws[i][:, dh:].astype(BF16)
            qk_ref[s, :] = jnp.where(first_copy, gram_qs[i] * lmasks[i], 0.0).astype(BF16)
            qd_ref[s, :] = (qs[i] * edecs[i]).astype(BF16)
            kd_ref[s, :] = (ks[i] * jnp.exp(dlast - dec)).astype(BF16)
            gl_ref[pl.ds(pl.multiple_of((n * group + i) * 8, 8), 8), :] = jnp.broadcast_to(
                jnp.exp(dlast), (8, dh))
        return carry

    lax.fori_loop(0, n_chunks // group, prepare, 0)


def _gdn_prep(lyr, proj, bsz, seq, cw, gp, group=32):
    dh = GDN_HEAD
    n8 = seq // CHUNK * 8
    col = lambda base: pl.BlockSpec((seq, dh), lambda b, h: (b, base + h))
    cwspec = lambda base: _layer_spec(lyr, (CONV_W, dh), lambda b, h: (0, base + h))
    out = lambda: pl.BlockSpec((seq, dh), lambda b, h: (b, h))
    full = lambda: pltpu.VMEM((seq, dh), F32)
    sds = lambda dt: jax.ShapeDtypeStruct((bsz * seq, GDN_WIDTH), dt)
    return pl.pallas_call(
        functools.partial(_gdn_prep_kernel, group=group),
        grid=(bsz, GDN_HEADS),
        in_specs=[col(COL_BQ), col(COL_BK), col(COL_BV),
                  pl.BlockSpec((seq, dh), lambda b, h: (b, COL_BBA)),
                  cwspec(0), cwspec(GDN_HEADS), cwspec(2 * GDN_HEADS),
                  _layer_spec(lyr, (2, dh), lambda b, h: (0, 0))],
        out_specs=[out(), out(), out(), out(), out(),
                   pl.BlockSpec((n8, dh), lambda b, h: (b, h))],
        out_shape=[sds(F32), sds(BF16), sds(BF16), sds(BF16), sds(BF16),
                   jax.ShapeDtypeStruct((bsz * n8, GDN_WIDTH), F32)],
        scratch_shapes=[pltpu.VMEM((seq + 8, dh), F32), full(), full(), full(), full(), full()],
        compiler_params=pltpu.CompilerParams(
            dimension_semantics=("parallel", "parallel"), vmem_limit_bytes=VMEM_LIMIT),
        name="gdn_prep",
    )(proj, proj, proj, proj, cw, cw, cw, gp)


def _gdn_scan_kernel(u_ref, w_ref, qd_ref, kd_ref, qk_ref, gl_ref, gate_ref, ng_ref,
                     y_ref, state_ref, o_ref):
    rows = u_ref.shape[0]
    c = CHUNK
    dh = GDN_HEAD

    @pl.when(pl.program_id(1) == 0)
    def _():
        state_ref[...] = jnp.zeros_like(state_ref)

    heads = range(GDN_HEADS)
    lanes = [slice(h * dh, (h + 1) * dh) for h in heads]

    def step(n, carry):
        sl = pl.ds(pl.multiple_of(n * c, c), c)
        gsl = pl.ds(pl.multiple_of(n * 8, 8), 8)
        states = [state_ref[h] for h in heads]
        wss = [jnp.dot(jnp.concatenate([w_ref[sl, ls], qd_ref[sl, ls]], axis=0), st.astype(BF16),
                       preferred_element_type=F32) for ls, st in zip(lanes, states)]
        v_news = [(u_ref[sl, ls] - ws[0:c, :]).astype(BF16) for ls, ws in zip(lanes, wss)]
        upds = [lax.dot_general(kd_ref[sl, ls], vn, (((0,), (0,)), ((), ())), preferred_element_type=F32)
                for ls, vn in zip(lanes, v_news)]
        outs = [jnp.dot(qk_ref[sl, ls][:, 0:c], vn, preferred_element_type=F32)
                for ls, vn in zip(lanes, v_news)]
        for h in heads:
            ls = lanes[h]
            state_ref[h] = states[h] * gl_ref[gsl, ls][0:1, :] + upds[h]
            o_ref[sl, ls] = wss[h][c:, :] + outs[h]
        return carry

    lax.fori_loop(0, rows // c, step, 0)

    ng = ng_ref[...]
    for h in range(GDN_HEADS):
        ls = slice(h * dh, (h + 1) * dh)
        o = o_ref[:, ls]
        o = o * lax.rsqrt(jnp.mean(o * o, axis=-1, keepdims=True) + NORM_EPS) * ng
        y_ref[:, ls] = (o * _silu(gate_ref[:, ls])).astype(y_ref.dtype)


def _gdn_scan(lyr, u, w, qd, kd, qk, gl, proj, bsz, seq, ng):
    rows = min(SCAN_ROWS, seq)
    ns = seq // rows
    g8 = rows // CHUNK * 8
    wd = GDN_WIDTH
    blk = lambda: pl.BlockSpec((rows, wd), lambda b, s: (b * ns + s, 0))
    return pl.pallas_call(
        _gdn_scan_kernel,
        grid=(bsz, ns),
        in_specs=[blk(), blk(), blk(), blk(), blk(),
                  pl.BlockSpec((g8, wd), lambda b, s: (b * ns + s, 0)),
                  pl.BlockSpec((rows, wd), lambda b, s: (b * ns + s, COL_BG * LANE // wd)),
                  _layer_spec(lyr, (1, GDN_HEAD), lambda b, s: (0, 0))],
        out_specs=blk(),
        out_shape=jax.ShapeDtypeStruct((bsz * seq, wd), BF16),
        scratch_shapes=[pltpu.VMEM((GDN_HEADS, GDN_HEAD, GDN_HEAD), F32),
                        pltpu.VMEM((rows, wd), F32)],
        compiler_params=pltpu.CompilerParams(
            dimension_semantics=("parallel", "arbitrary"), vmem_limit_bytes=VMEM_LIMIT),
        name="gdn_scan",
    )(u, w, qd, kd, qk, gl, proj, ng)


def _pair_mask():
    r = lax.broadcasted_iota(jnp.int32, (LANE, LANE), 0) // RWKV_HEAD
    c = lax.broadcasted_iota(jnp.int32, (LANE, LANE), 1) // RWKV_HEAD
    return r == c


def _rwkv_prep_kernel(r_ref, k_ref, v_ref, wl_ref, al_ref,
                      mur_ref, muk_ref, muv_ref, muwl_ref, mual_ref,
                      w0_ref, wup_ref, a0_ref, aup_ref, kk_ref, ka_ref, rk_ref,
                      u0_ref, o0_ref, wt_ref, rt_ref, mrb_ref, b2_ref, kv_ref, pc_ref, bonus_ref,
                      rs_ref, k2_ref, vs_ref, kn_ref, kna_ref, lw_ref, *, group):
    seq, dw = r_ref.shape
    c = CHUNK
    hd = RWKV_HEAD
    n_chunks = seq // c
    group = min(group, n_chunks)

    seg = _pair_mask().astype(BF16)
    seg2 = jnp.concatenate([seg, seg], axis=0)

    def gates(r0, r1):
        rows = slice(r0, r1)
        first_row = lax.broadcasted_iota(jnp.int32, (r1 - r0, dw), 0) == 0

        def shift(x_ref, mu_ref):
            x = x_ref[rows, :]
            before = jnp.zeros((1, dw), F32) if r0 == 0 else x_ref[r0 - 8:r0, :][7:8, :]
            prev = jnp.where(first_row, before, pltpu.roll(x, 1, axis=0))
            return x + (prev - x) * mu_ref[...]

        r = shift(r_ref, mur_ref)
        k = shift(k_ref, muk_ref)
        v = shift(v_ref, muv_ref)
        wl = shift(wl_ref, muwl_ref)
        al = shift(al_ref, mual_ref)
        z = w0_ref[...] + _dot(jnp.tanh(wl), wup_ref[...])
        lw_ref[rows, :] = -math.exp(-0.5) * _sigmoid(z)
        a = _sigmoid(a0_ref[...] + _dot(al, aup_ref[...]))
        kn = k * kk_ref[...]
        kn = kn * lax.rsqrt(_segsum(kn * kn, seg2) + 1e-6)
        k2 = k * (1.0 + (a - 1.0) * ka_ref[...])
        rs_ref[rows, :] = r
        k2_ref[rows, :] = k2
        vs_ref[rows, :] = v
        kn_ref[rows, :] = kn
        kna_ref[rows, :] = kn * a
        bonus_ref[rows, :] = _segsum(r * k2 * rk_ref[...], seg2) * v

    incl = _chunk_masks()[0]
    incl2, strict2, _, strict_upper2, eye2 = _chunk_masks(2 * c)
    ltri = incl.astype(BF16)
    ltri3 = jnp.concatenate([ltri, ltri, ltri], axis=1)
    head0w = lax.broadcasted_iota(jnp.int32, (1, 2 * dw), 1) % dw < hd
    head0 = head0w[:, 0:dw]
    head_masks = [head0, jnp.logical_not(head0)]

    def prepare(n):
        sls = [slice((n * group + i) * c, (n * group + i + 1) * c) for i in range(group)]
        lws = [lw_ref[s, :] for s in sls]
        cums = [_cumsum_rows(ltri3, lw) for lw in lws]
        knas = [kna_ref[s, :] for s in sls]
        k2s = [k2_ref[s, :] for s in sls]
        vvs = [vs_ref[s, :] for s in sls]
        einvs = [jnp.exp(-cum) for cum in cums]
        a_ts = [-kn_ref[s, :] * jnp.exp(cum - lw) for s, cum, lw in zip(sls, cums, lws)]
        r_ts = [rs_ref[s, :] * jnp.exp(cum) for s, cum in zip(sls, cums)]
        b_ts = [(kna * e).astype(BF16) for kna, e in zip(knas, einvs)]
        a_bs = [a.astype(BF16) for a in a_ts]
        r_bs = [r.astype(BF16) for r in r_ts]
        zero = jnp.zeros((), BF16)

        def by_head(x):
            zx = jnp.zeros((), x.dtype)
            return jnp.concatenate([jnp.where(head0w[:, 0:x.shape[1]], x, zx),
                                    jnp.where(head0w[:, 0:x.shape[1]], zx, x)], axis=0)

        aas = [jnp.concatenate([a, a], axis=0) for a in a_bs]
        prob = [(i, m) for i in range(group) for m in head_masks]
        a_ab_ts = [_dot_nt(jnp.where(m, b_ts[i], zero), aas[i]) for i, m in prob]
        m_rbs = [_dot_nt(r, by_head(b)) for r, b in zip(r_bs, b_ts)]
        g_ks = [_dot_nt(jnp.concatenate([a, r], axis=0), by_head((k2c * e).astype(BF16)))
                for a, r, k2c, e in zip(a_bs, r_bs, k2s, einvs)]
        t_mat_ts = _tri_inverse_t([jnp.where(strict_upper2, g, 0.0) for g in a_ab_ts], eye2)
        vbds = [by_head(v.astype(BF16)) for v in vvs]
        akvs = [_dot(jnp.where(strict2, g[0:c, :], 0.0), vbd) for g, vbd in zip(g_ks, vbds)]
        o0s = [_dot(jnp.where(incl2, g[c:, :], 0.0), vbd) for g, vbd in zip(g_ks, vbds)]
        tws = [_dot_tn(jnp.concatenate([t_mat_ts[2 * i], t_mat_ts[2 * i + 1]], axis=0),
                       by_head(jnp.concatenate([a_ts[i], akvs[i]], axis=1)))[c:, :]
               for i in range(group)]
        for i in range(group):
            s = sls[i]
            pick = lambda x0, x1: jnp.where(head0, x0, x1)
            clast = cums[i][c - 1:c, :]
            tail = jnp.exp(clast - cums[i])
            kv_full = _dot_tn(k2s[i] * tail, vvs[i])
            pc_full = jnp.broadcast_to(jnp.exp(clast), (dw, dw)).T
            u0_ref[s, :] = tws[i][:, dw:]
            o0_ref[s, :] = o0s[i]
            wt_ref[s, :] = tws[i][:, 0:dw].astype(BF16)
            rt_ref[s, :] = r_bs[i]
            mrb_ref[s, :] = jnp.where(incl2, m_rbs[i], 0.0).astype(BF16)
            b2_ref[s, :] = (knas[i] * tail).astype(BF16)
            kv_ref[s, :] = pick(kv_full[0:c, :], kv_full[c:, :])
            pc_ref[s, :] = pick(pc_full[0:c, :], pc_full[c:, :])

    for part in range(n_chunks // group):
        gates(part * group * c, (part + 1) * group * c)
        prepare(part)


def _rwkv_prep(lyr, proj, bsz, seq, mu, w0, wup, a0, aup, kk, ka, rk, group=16):
    dw = LANE
    np_ = RWKV_PAIRS
    col = lambda base: pl.BlockSpec((seq, dw), lambda b, p: (b, base + p))
    fixed_col = lambda idx: pl.BlockSpec((seq, dw), lambda b, p: (b, idx))
    vec = lambda base: _layer_spec(lyr, (1, dw), lambda b, p: (0, base + p))
    fixed_vec = lambda idx: _layer_spec(lyr, (1, dw), lambda b, p: (0, idx))
    lora = lambda: _layer_spec(lyr, (dw, dw), lambda b, p: (0, p))
    out = lambda: pl.BlockSpec((seq, dw), lambda b, p: (b, p))
    full = lambda: pltpu.VMEM((seq, dw), F32)
    sds = lambda dt: jax.ShapeDtypeStruct((bsz * seq, RWKV_WIDTH), dt)
    return pl.pallas_call(
        functools.partial(_rwkv_prep_kernel, group=group),
        grid=(bsz, np_),
        in_specs=[col(COL_CR), col(COL_CK), col(COL_CV), fixed_col(COL_CWL), fixed_col(COL_CAL),
                  vec(0), vec(np_), vec(2 * np_), fixed_vec(3 * np_), fixed_vec(3 * np_ + 1),
                  vec(0), lora(), vec(0), lora(), vec(0), vec(0), vec(0)],
        out_specs=[out() for _ in range(9)],
        out_shape=[sds(F32), sds(F32), sds(BF16), sds(BF16), sds(BF16), sds(BF16),
                   sds(F32), sds(F32), sds(F32)],
        scratch_shapes=[full(), full(), full(), full(), full(), full()],
        compiler_params=pltpu.CompilerParams(
            dimension_semantics=("parallel", "parallel"), vmem_limit_bytes=VMEM_LIMIT),
        name="rwkv_prep",
    )(proj, proj, proj, proj, proj, mu, mu, mu, mu, mu, w0, wup, a0, aup, kk, ka, rk)


def _rwkv_scan_kernel(u0_ref, o0_ref, wt_ref, rt_ref, mrb_ref, b2_ref, kv_ref, pc_ref, bonus_ref,
                      gate_ref, gnw_ref, gnb_ref, y_ref, state_ref, o_ref):
    rows = u0_ref.shape[0]
    c = CHUNK
    dw = LANE
    pair = _pair_mask()

    @pl.when(pl.program_id(1) == 0)
    def _():
        state_ref[...] = jnp.zeros_like(state_ref)

    def block_diag(x):
        return jnp.where(pair, jnp.concatenate([x, x], axis=0), 0.0)

    pairs = range(RWKV_PAIRS)
    lanes = [slice(p * dw, (p + 1) * dw) for p in pairs]

    def step(n, carry):
        sl = pl.ds(pl.multiple_of(n * c, c), c)
        hms = [state_ref[p] for p in pairs]
        whs = [jnp.dot(jnp.concatenate([wt_ref[sl, ls], rt_ref[sl, ls]], axis=0), hm.astype(BF16),
                       preferred_element_type=F32) for ls, hm in zip(lanes, hms)]
        us = [u0_ref[sl, ls] + wh[0:c, :] for ls, wh in zip(lanes, whs)]
        upds = [lax.dot_general(b2_ref[sl, ls], u.astype(BF16), (((0,), (0,)), ((), ())),
                                preferred_element_type=F32) for ls, u in zip(lanes, us)]
        outs = [jnp.dot(mrb_ref[sl, ls], block_diag(u).astype(BF16), preferred_element_type=F32)
                for ls, u in zip(lanes, us)]
        for p in pairs:
            ls = lanes[p]
            pc = pc_ref[sl, ls]
            state_ref[p] = (jnp.concatenate([pc, pc], axis=0) * hms[p]
                            + jnp.where(pair, upds[p], 0.0) + block_diag(kv_ref[sl, ls]))
            o_ref[sl, ls] = whs[p][c:, :] + outs[p] + o0_ref[sl, ls]
        return carry

    lax.fori_loop(0, rows // c, step, 0)

    seg = pair.astype(BF16)
    seg2 = jnp.concatenate([seg, seg], axis=0)
    inv_n = 1.0 / RWKV_HEAD
    for p in range(RWKV_PAIRS):
        ls = slice(p * dw, (p + 1) * dw)
        o = o_ref[:, ls]
        cen = o - _segsum(o, seg2) * inv_n
        var = _segsum(cen * cen, seg2) * inv_n
        yn = cen * lax.rsqrt(var + RWKV_GN_EPS) * gnw_ref[:, ls] + gnb_ref[:, ls]
        y_ref[:, ls] = ((yn + bonus_ref[:, ls]) * _silu(gate_ref[:, ls])).astype(y_ref.dtype)


def _rwkv_scan(lyr, prep, proj, bsz, seq, gnw, gnb):
    rows = min(SCAN_ROWS, seq)
    ns = seq // rows
    wd = RWKV_WIDTH
    blk = lambda: pl.BlockSpec((rows, wd), lambda b, s: (b * ns + s, 0))
    vec = lambda: _layer_spec(lyr, (1, wd), lambda b, s: (0, 0))
    return pl.pallas_call(
        _rwkv_scan_kernel,
        grid=(bsz, ns),
        in_specs=[blk() for _ in range(9)]
        + [pl.BlockSpec((rows, wd), lambda b, s: (b * ns + s, COL_CG * LANE // wd)), vec(), vec()],
        out_specs=blk(),
        out_shape=jax.ShapeDtypeStruct((bsz * seq, wd), BF16),
        scratch_shapes=[pltpu.VMEM((RWKV_PAIRS, LANE, LANE), F32),
                        pltpu.VMEM((rows, wd), F32)],
        compiler_params=pltpu.CompilerParams(
            dimension_semantics=("parallel", "arbitrary"), vmem_limit_bytes=VMEM_LIMIT),
        name="rwkv_scan",
    )(*prep, proj, gnw, gnb)


def _pack_source(j):
    a1 = 2 * LRU_WIDTH
    b1 = a1 + 4 * GDN_WIDTH
    c0 = b1 + 2 * GDN_HEADS
    c1 = c0 + 3 * RWKV_WIDTH
    c2 = c1 + 2 * LORA
    start, valid = jnp.int32(0), jnp.int32(0)
    for first, count, src, width in ((COL_BQ, COL_CR - COL_BQ, a1, LANE),
                                     (COL_CR, COL_CG - COL_CR, c0, LANE),
                                     (COL_CG, COL_AX - COL_CG, c2, LANE),
                                     (COL_AX, COL_BBA - COL_AX, 0, LANE),
                                     (COL_BBA, 1, b1, 2 * GDN_HEADS),
                                     (COL_CWL, 1, c1, LORA),
                                     (COL_CAL, 1, c1 + LORA, LORA)):
        inside = (j >= first) & (j < first + count)
        start = jnp.where(inside, src + (j - first) * LANE, start)
        valid = jnp.where(inside, width, valid)
    return start, valid


def _pack_w_in_kernel(w_ref, o_ref):
    _, valid = _pack_source(pl.program_id(0))
    keep = lax.broadcasted_iota(jnp.int32, (w_ref.shape[0], w_ref.shape[2]), 0) < valid
    for lyr in range(w_ref.shape[1]):
        o_ref[lyr] = jnp.where(keep, w_ref[:, lyr, :], 0.0).astype(BF16)


def _pack_w_in(w_in):
    lyr, d, n = w_in.shape
    w_t = jnp.transpose(w_in, (2, 0, 1))
    return pl.pallas_call(
        _pack_w_in_kernel,
        grid=(N_PACKED // LANE,),
        in_specs=[pl.BlockSpec((pl.Element(LANE), pl.Element(lyr), pl.Element(d)),
                               lambda j: (_pack_source(j)[0], 0, 0))],
        out_specs=pl.BlockSpec((lyr, LANE, d), lambda j: (0, j, 0)),
        out_shape=jax.ShapeDtypeStruct((lyr, N_PACKED, d), BF16),
        compiler_params=pltpu.CompilerParams(
            dimension_semantics=("parallel",), vmem_limit_bytes=VMEM_LIMIT),
        name="pack_w_in",
    )(w_t)


def _pack_mu(mu):
    lyr = mu.shape[0]
    z = jnp.zeros((lyr, LANE - LORA), mu.dtype)
    c1 = 3 * RWKV_WIDTH
    return jnp.concatenate([mu[:, :c1], mu[:, c1:c1 + LORA], z, mu[:, c1 + LORA:], z], axis=-1)


def _block_diag(w):
    lyr, nb, bi, bj = w.shape
    eye = jnp.eye(nb, dtype=w.dtype)
    return jnp.einsum('lnij,nm->lnimj', w, eye).reshape(lyr, nb * bi, nb * bj)


def kernel(x, norm_g, w_in, w_out, lru_conv_w, lru_conv_b, lru_wx, lru_bx, lru_wa, lru_ba, lru_lambda,
           gdn_conv_w, gdn_a_log, gdn_dt_bias, gdn_norm_g, rwkv_mu, rwkv_w0, rwkv_w_up, rwkv_a0,
           rwkv_a_up, rwkv_k_k, rwkv_k_a, rwkv_r_k, rwkv_gn_w, rwkv_gn_b, final_norm_g):
    bsz, seq, d = x.shape
    depth = w_in.shape[0]
    x2 = x.reshape(bsz * seq, d)

    w_in_p = _pack_w_in(w_in)
    w_out_b = w_out.astype(BF16)
    wx_bd = _block_diag(lru_wx).astype(BF16)
    wa_bd = _block_diag(lru_wa).astype(BF16)
    mu_p = _pack_mu(rwkv_mu)
    pad_rows = lambda w: jnp.pad(w, ((0, 0), (0, LANE - LORA), (0, 0))).astype(BF16)
    wup_p = pad_rows(rwkv_w_up)
    aup_p = pad_rows(rwkv_a_up)
    gp = jnp.zeros((depth, 2, LANE), F32)
    gp = gp.at[:, 0, GDN_HEADS:2 * GDN_HEADS].set(gdn_a_log)
    gp = gp.at[:, 1, GDN_HEADS:2 * GDN_HEADS].set(gdn_dt_bias)
    rows = lambda t: t.reshape(depth, 1, -1)
    norm_g, lru_conv_b, lru_bx, lru_ba, lru_lambda, gdn_norm_g = map(
        rows, (norm_g, lru_conv_b, lru_bx, lru_ba, lru_lambda, gdn_norm_g))
    mu_p, rwkv_w0, rwkv_a0, rwkv_k_k, rwkv_k_a, rwkv_r_k, rwkv_gn_w, rwkv_gn_b = map(
        rows, (mu_p, rwkv_w0, rwkv_a0, rwkv_k_k, rwkv_k_a, rwkv_r_k, rwkv_gn_w, rwkv_gn_b))
    final_g = final_norm_g.reshape(1, -1)

    for l in range(depth):
        proj = _inproj(l, x2, norm_g, w_in_p)
        ya = _lru(l, proj, bsz, seq, lru_conv_w, lru_conv_b, wx_bd, lru_bx, wa_bd, lru_ba, lru_lambda)
        gdn_ops = _gdn_prep(l, proj, bsz, seq, gdn_conv_w, gp)
        yb = _gdn_scan(l, *gdn_ops, proj, bsz, seq, gdn_norm_g)
        rwkv_ops = _rwkv_prep(l, proj, bsz, seq, mu_p, rwkv_w0, wup_p, rwkv_a0, aup_p,
                              rwkv_k_k, rwkv_k_a, rwkv_r_k)
        yc = _rwkv_scan(l, rwkv_ops, proj, bsz, seq, rwkv_gn_w, rwkv_gn_b)
        x2 = _outproj(l, ya, yb, yc, x2, w_out_b, final_g, final_norm=(l == depth - 1))
    return x2.reshape(bsz, seq, d)
```

```python
import functools
import math

import jax
import jax.numpy as jnp
from jax import lax
from jax.experimental import pallas as pl
from jax.experimental.pallas import tpu as pltpu

F32 = jnp.float32
BF16 = jnp.bfloat16

NORM_EPS = 1e-6
CONV_W = 4
CHUNK = 64
LRU_WIDTH = 512
LRU_C = 8.0
GDN_HEAD = 128
GDN_HEADS = 6
GDN_WIDTH = GDN_HEADS * GDN_HEAD
RWKV_HEAD = 64
RWKV_WIDTH = 768
LORA = 96
RWKV_GN_EPS = 64e-5

LANE = 128
RWKV_PAIRS = RWKV_WIDTH // LANE
N_PACKED = 7680
COL_BQ, COL_BK, COL_BV, COL_BG = 0, 6, 12, 18
COL_CR, COL_CK, COL_CV, COL_CG = 24, 30, 36, 42
COL_AX, COL_AG = 48, 52
COL_BBA, COL_CWL, COL_CAL = 56, 57, 58
SCAN_ROWS = 512

VMEM_LIMIT = 52 * 1024 * 1024


def _dot(a, b):
    return jnp.dot(a.astype(BF16), b.astype(BF16), preferred_element_type=F32)


def _dot_nt(a, b):
    return lax.dot_general(a.astype(BF16), b.astype(BF16), (((1,), (1,)), ((), ())),
                           preferred_element_type=F32)


def _dot_tn(a, b):
    return lax.dot_general(a.astype(BF16), b.astype(BF16), (((0,), (0,)), ((), ())),
                           preferred_element_type=F32)


def _split2(x):
    hi = x.astype(BF16)
    lo = (x - hi.astype(F32)).astype(BF16)
    return hi, lo


def _split3(x):
    hi = x.astype(BF16)
    r1 = x - hi.astype(F32)
    mid = r1.astype(BF16)
    lo = (r1 - mid.astype(F32)).astype(BF16)
    return hi, mid, lo


def _cumsum_rows(ltri3, x):
    return jnp.dot(ltri3, jnp.concatenate(_split3(x), axis=0), preferred_element_type=F32)


def _segsum(x, seg2):
    return jnp.dot(jnp.concatenate(_split2(x), axis=1), seg2, preferred_element_type=F32)


def _tri_inverse_t(a_twins, eye_twin):
    c, w = eye_twin.shape
    first_half = lax.broadcasted_iota(jnp.int32, (c, w), 1) < c
    xs = [jnp.where(first_half, a, eye_twin) for a in a_twins]
    for _ in range(6):
        xbs = [x.astype(BF16) for x in xs]
        outs = [jnp.dot(xb[:, 0:c], xb, preferred_element_type=F32) for xb in xbs]
        xs = [jnp.where(first_half, o, x + o) for x, o in zip(xs, outs)]
    return xs


def _chunk_masks(width=CHUNK):
    row = lax.broadcasted_iota(jnp.int32, (CHUNK, width), 0)
    col = lax.broadcasted_iota(jnp.int32, (CHUNK, width), 1) % CHUNK
    return row >= col, row > col, row <= col, row < col, (row == col).astype(F32)


def _softplus(x):
    return jnp.maximum(x, 0.0) + jnp.log1p(jnp.exp(-jnp.abs(x)))


def _sigmoid(x):
    return 0.5 * jnp.tanh(0.5 * x) + 0.5


def _silu(x):
    h = 0.5 * x
    return h * jnp.tanh(h) + h


def _inproj_kernel(x_ref, g_ref, w_ref, o_ref, h_ref):
    @pl.when(pl.program_id(1) == 0)
    def _():
        x = x_ref[...]
        ms = jnp.mean(x * x, axis=-1, keepdims=True)
        h_ref[...] = (x * lax.rsqrt(ms + NORM_EPS) * g_ref[...]).astype(BF16)

    o_ref[...] = lax.dot_general(h_ref[...], w_ref[...], (((1,), (1,)), ((), ())),
                                 preferred_element_type=F32)


def _layer_spec(lyr, block, index):
    return pl.BlockSpec((None,) + block, lambda *grid_idx: (lyr,) + index(*grid_idx))


def _inproj(lyr, x2, g, w, tm=1024, tn=1536):
    m, d = x2.shape
    n = w.shape[1]
    tm = min(tm, m)
    assert m % tm == 0 and n % tn == 0 and w.shape[2] == d, (x2.shape, w.shape)
    return pl.pallas_call(
        _inproj_kernel,
        grid=(m // tm, n // tn),
        in_specs=[pl.BlockSpec((tm, d), lambda i, j: (i, 0)),
                  _layer_spec(lyr, (1, d), lambda i, j: (0, 0)),
                  _layer_spec(lyr, (tn, d), lambda i, j: (j, 0))],
        out_specs=pl.BlockSpec((tm, tn), lambda i, j: (i, j)),
        out_shape=jax.ShapeDtypeStruct((m, n), F32),
        scratch_shapes=[pltpu.VMEM((tm, d), BF16)],
        compiler_params=pltpu.CompilerParams(
            dimension_semantics=("parallel", "arbitrary"), vmem_limit_bytes=VMEM_LIMIT),
        name="inproj",
    )(x2, g, w)


def _outproj_kernel(ya_ref, yb_ref, yc_ref, x_ref, w_ref, fg_ref, o_ref, *, final_norm):
    wa = LRU_WIDTH
    wb = LRU_WIDTH + GDN_WIDTH
    acc = jnp.dot(ya_ref[...], w_ref[0:wa, :], preferred_element_type=F32)
    acc += jnp.dot(yb_ref[...], w_ref[wa:wb, :], preferred_element_type=F32)
    acc += jnp.dot(yc_ref[...], w_ref[wb:, :], preferred_element_type=F32)
    xn = x_ref[...] + acc
    if final_norm:
        ms = jnp.mean(xn * xn, axis=-1, keepdims=True)
        xn = xn * lax.rsqrt(ms + NORM_EPS) * fg_ref[...]
    o_ref[...] = xn


def _outproj(lyr, ya, yb, yc, x2, w, fg, final_norm, tm=512):
    m, d = x2.shape
    tm = min(tm, m)
    assert m % tm == 0 and d == LRU_WIDTH + GDN_WIDTH + RWKV_WIDTH, x2.shape
    return pl.pallas_call(
        functools.partial(_outproj_kernel, final_norm=final_norm),
        grid=(m // tm,),
        in_specs=[pl.BlockSpec((tm, LRU_WIDTH), lambda i: (i, 0)),
                  pl.BlockSpec((tm, GDN_WIDTH), lambda i: (i, 0)),
                  pl.BlockSpec((tm, RWKV_WIDTH), lambda i: (i, 0)),
                  pl.BlockSpec((tm, d), lambda i: (i, 0)),
                  _layer_spec(lyr, (d, d), lambda i: (0, 0)),
                  pl.BlockSpec((1, d), lambda i: (0, 0))],
        out_specs=pl.BlockSpec((tm, d), lambda i: (i, 0)),
        out_shape=jax.ShapeDtypeStruct((m, d), F32),
        compiler_params=pltpu.CompilerParams(
            dimension_semantics=("parallel",), vmem_limit_bytes=VMEM_LIMIT),
        name="outproj",
    )(ya, yb, yc, x2, w, fg)


def _lru_kernel(x_ref, g_ref, cw_ref, cb_ref, wx_ref, bx_ref, wa_ref, ba_ref, lam_ref,
                y_ref, ext_ref, h_ref):
    ts, w = x_ref.shape

    @pl.when(pl.program_id(1) == 0)
    def _():
        ext_ref[0:8, :] = jnp.zeros((8, w), F32)
        h_ref[...] = jnp.zeros_like(h_ref)

    ext_ref[8:, :] = x_ref[...]
    cw = cw_ref[...]
    xc = cb_ref[...] + cw[0:1, :] * ext_ref[pl.ds(5, ts), :]
    for j in range(1, CONV_W):
        xc = xc + cw[j:j + 1, :] * ext_ref[pl.ds(5 + j, ts), :]
    ext_ref[0:8, :] = ext_ref[pl.ds(ts, 8), :]

    gate_x = _sigmoid(_dot(xc, wx_ref[...]) + bx_ref[...])
    gate_a = _sigmoid(_dot(xc, wa_ref[...]) + ba_ref[...])
    log_a = -LRU_C * gate_a * _softplus(-lam_ref[...])
    a = jnp.exp(log_a)
    mult = jnp.sqrt(jnp.maximum(-jnp.tanh(log_a) * (a * a + 1.0), 0.0))
    u = mult * (gate_x * xc)

    row = lax.broadcasted_iota(jnp.int32, (ts, w), 0)
    d = 1
    while d < ts:
        keep = row >= d
        u = jnp.where(keep, a * pltpu.roll(u, d, axis=0) + u, u)
        a = jnp.where(keep, a * pltpu.roll(a, d, axis=0), a)
        d *= 2
    h = u + a * h_ref[0:1, :]
    h_ref[0:1, :] = h[ts - 1:ts, :]
    y_ref[...] = (h * _silu(g_ref[...])).astype(y_ref.dtype)


def _lru(lyr, proj, bsz, seq, cw, cb, wx, bx, wa, ba, lam, ts=512):
    ts = min(ts, seq)
    assert seq % ts == 0, seq
    ns = seq // ts
    w = LRU_WIDTH
    vec = lambda: _layer_spec(lyr, (1, w), lambda b, s: (0, 0))
    return pl.pallas_call(
        _lru_kernel,
        grid=(bsz, ns),
        in_specs=[pl.BlockSpec((ts, w), lambda b, s: (b * ns + s, COL_AX * LANE // w)),
                  pl.BlockSpec((ts, w), lambda b, s: (b * ns + s, COL_AG * LANE // w)),
                  _layer_spec(lyr, (CONV_W, w), lambda b, s: (0, 0)),
                  vec(),
                  _layer_spec(lyr, (w, w), lambda b, s: (0, 0)),
                  vec(),
                  _layer_spec(lyr, (w, w), lambda b, s: (0, 0)),
                  vec(), vec()],
        out_specs=pl.BlockSpec((ts, w), lambda b, s: (b * ns + s, 0)),
        out_shape=jax.ShapeDtypeStruct((bsz * seq, w), BF16),
        scratch_shapes=[pltpu.VMEM((ts + 8, w), F32), pltpu.VMEM((8, w), F32)],
        compiler_params=pltpu.CompilerParams(
            dimension_semantics=("parallel", "arbitrary"), vmem_limit_bytes=VMEM_LIMIT),
        name="lru",
    )(proj, proj, cw, cb, wx, bx, wa, ba, lam)


def _gdn_prep_kernel(q_ref, k_ref, v_ref, ba_ref, cwq_ref, cwk_ref, cwv_ref, gp_ref,
                     u_ref, w_ref, qd_ref, kd_ref, qk_ref, gl_ref,
                     ext_ref, qs_ref, ks_ref, vs_ref, bt_ref, gg_ref, *, group):
    head = pl.program_id(1)
    seq, dh = q_ref.shape
    c = CHUNK
    n_chunks = seq // c
    group = min(group, n_chunks)

    def conv_silu(x_ref, cw_ref):
        ext_ref[0:8, :] = jnp.zeros((8, dh), F32)
        ext_ref[8:, :] = x_ref[...]
        cw = cw_ref[...]
        xc = cw[0:1, :] * ext_ref[pl.ds(5, seq), :]
        for j in range(1, CONV_W):
            xc = xc + cw[j:j + 1, :] * ext_ref[pl.ds(5 + j, seq), :]
        return _silu(xc)

    def l2norm(t):
        return t * lax.rsqrt(jnp.sum(t * t, axis=-1, keepdims=True) + 1e-6)

    qs_ref[...] = l2norm(conv_silu(q_ref, cwq_ref)) * (GDN_HEAD ** -0.5)
    ks_ref[...] = l2norm(conv_silu(k_ref, cwk_ref))
    vs_ref[...] = conv_silu(v_ref, cwv_ref)

    ba = ba_ref[...]
    lane = lax.broadcasted_iota(jnp.int32, ba.shape, 1)
    gp = gp_ref[...]
    beta_all = _sigmoid(ba)
    g_all = -jnp.exp(gp[0:1, :]) * _softplus(ba + gp[1:2, :])
    beta_col = jnp.sum(jnp.where(lane == head, beta_all, 0.0), axis=1, keepdims=True)
    g_col = jnp.sum(jnp.where(lane == head + GDN_HEADS, g_all, 0.0), axis=1, keepdims=True)
    bt_ref[...] = jnp.broadcast_to(beta_col, (seq, dh))
    gg_ref[...] = jnp.broadcast_to(g_col, (seq, dh))

    incl = _chunk_masks()[0]
    incl2, _, upper2, strict_upper2, eye2 = _chunk_masks(2 * c)
    first_copy = lax.broadcasted_iota(jnp.int32, (c, dh), 1) < c
    ltri = incl.astype(BF16)
    ltri3 = jnp.concatenate([ltri, ltri, ltri], axis=1)

    def prepare(n, carry):
        rng = range(group)
        sls = [pl.ds(pl.multiple_of((n * group + i) * c, c), c) for i in rng]
        qs = [qs_ref[s, :] for s in sls]
        ks = [ks_ref[s, :] for s in sls]
        betas = [bt_ref[s, :] for s in sls]
        decs = [_cumsum_rows(ltri3, gg_ref[s, :]) for s in sls]
        kbs = [k * b for k, b in zip(ks, betas)]
        gram_ts = [_dot_nt(k, jnp.concatenate([kb, kb], axis=0)) for k, kb in zip(ks, kbs)]
        gram_qs = [_dot_nt(q, jnp.concatenate([k, k], axis=0)) for q, k in zip(qs, ks)]
        lmasks, lmask_ts = [], []
        for dec in decs:
            diff = dec - jnp.concatenate([dec, dec], axis=0).T[0:c, :]
            lmasks.append(jnp.where(incl2, jnp.exp(jnp.where(incl2, diff, 0.0)), 0.0))
            lmask_ts.append(jnp.where(upper2, jnp.exp(jnp.where(upper2, -diff, 0.0)), 0.0))
        t_mat_ts = _tri_inverse_t([jnp.where(strict_upper2, -(g * m), 0.0)
                                   for g, m in zip(gram_ts, lmask_ts)], eye2)
        edecs = [jnp.exp(dec) for dec in decs]
        uws = [_dot_tn(x, jnp.concatenate([vs_ref[s, :] * b, kb * e], axis=1))[c:, :]
               for x, s, b, kb, e in zip(t_mat_ts, sls, betas, kbs, edecs)]
        for i in rng:
            s = sls[i]
            dec = decs[i]
            dlast = dec[c - 1:c, :]
            u_ref[s, :] = uws[i][:, 0:dh]
            w_ref[s, :] = uws[i][:, dh:].astype(BF16)
            qk_ref[s, :] = jnp.where(first_copy, gram_qs[i] * lmasks[i], 0.0).astype(BF16)
            qd_ref[s, :] = (qs[i] * edecs[i]).astype(BF16)
            kd_ref[s, :] = (ks[i] * jnp.exp(dlast - dec)).astype(BF16)
            gl_ref[pl.ds(pl.multiple_of((n * group + i) * 8, 8), 8), :] = jnp.broadcast_to(
                jnp.exp(dlast), (8, dh))
        return carry

    lax.fori_loop(0, n_chunks // group, prepare, 0)


def _gdn_prep(lyr, proj, bsz, seq, cw, gp, group=32):
    group = min(group, seq // CHUNK)
    assert seq % (CHUNK * group) == 0, seq
    dh = GDN_HEAD
    n8 = seq // CHUNK * 8
    col = lambda base: pl.BlockSpec((seq, dh), lambda b, h: (b, base + h))
    cwspec = lambda base: _layer_spec(lyr, (CONV_W, dh), lambda b, h: (0, base + h))
    out = lambda: pl.BlockSpec((seq, dh), lambda b, h: (b, h))
    full = lambda: pltpu.VMEM((seq, dh), F32)
    sds = lambda dt: jax.ShapeDtypeStruct((bsz * seq, GDN_WIDTH), dt)
    return pl.pallas_call(
        functools.partial(_gdn_prep_kernel, group=group),
        grid=(bsz, GDN_HEADS),
        in_specs=[col(COL_BQ), col(COL_BK), col(COL_BV),
                  pl.BlockSpec((seq, dh), lambda b, h: (b, COL_BBA)),
                  cwspec(0), cwspec(GDN_HEADS), cwspec(2 * GDN_HEADS),
                  _layer_spec(lyr, (2, dh), lambda b, h: (0, 0))],
        out_specs=[out(), out(), out(), out(), out(),
                   pl.BlockSpec((n8, dh), lambda b, h: (b, h))],
        out_shape=[sds(F32), sds(BF16), sds(BF16), sds(BF16), sds(BF16),
                   jax.ShapeDtypeStruct((bsz * n8, GDN_WIDTH), F32)],
        scratch_shapes=[pltpu.VMEM((seq + 8, dh), F32), full(), full(), full(), full(), full()],
        compiler_params=pltpu.CompilerParams(
            dimension_semantics=("parallel", "parallel"), vmem_limit_bytes=VMEM_LIMIT),
        name="gdn_prep",
    )(proj, proj, proj, proj, cw, cw, cw, gp)


def _gdn_scan_kernel(u_ref, w_ref, qd_ref, kd_ref, qk_ref, gl_ref, gate_ref, ng_ref,
                     y_ref, state_ref, o_ref):
    rows = u_ref.shape[0]
    c = CHUNK
    dh = GDN_HEAD

    @pl.when(pl.program_id(1) == 0)
    def _():
        state_ref[...] = jnp.zeros_like(state_ref)

    heads = range(GDN_HEADS)
    lanes = [slice(h * dh, (h + 1) * dh) for h in heads]

    def step(n, carry):
        sl = pl.ds(pl.multiple_of(n * c, c), c)
        gsl = pl.ds(pl.multiple_of(n * 8, 8), 8)
        states = [state_ref[h] for h in heads]
        wss = [jnp.dot(jnp.concatenate([w_ref[sl, ls], qd_ref[sl, ls]], axis=0), st.astype(BF16),
                       preferred_element_type=F32) for ls, st in zip(lanes, states)]
        v_news = [(u_ref[sl, ls] - ws[0:c, :]).astype(BF16) for ls, ws in zip(lanes, wss)]
        upds = [lax.dot_general(kd_ref[sl, ls], vn, (((0,), (0,)), ((), ())), preferred_element_type=F32)
                for ls, vn in zip(lanes, v_news)]
        outs = [jnp.dot(qk_ref[sl, ls][:, 0:c], vn, preferred_element_type=F32)
                for ls, vn in zip(lanes, v_news)]
        for h in heads:
            ls = lanes[h]
            state_ref[h] = states[h] * gl_ref[gsl, ls][0:1, :] + upds[h]
            o_ref[sl, ls] = wss[h][c:, :] + outs[h]
        return carry

    lax.fori_loop(0, rows // c, step, 0)

    ng = ng_ref[...]
    for h in range(GDN_HEADS):
        ls = slice(h * dh, (h + 1) * dh)
        o = o_ref[:, ls]
        o = o * lax.rsqrt(jnp.mean(o * o, axis=-1, keepdims=True) + NORM_EPS) * ng
        y_ref[:, ls] = (o * _silu(gate_ref[:, ls])).astype(y_ref.dtype)


def _gdn_scan(lyr, u, w, qd, kd, qk, gl, proj, bsz, seq, ng):
    rows = min(SCAN_ROWS, seq)
    assert seq % rows == 0 and rows % CHUNK == 0, seq
    ns = seq // rows
    g8 = rows // CHUNK * 8
    wd = GDN_WIDTH
    blk = lambda: pl.BlockSpec((rows, wd), lambda b, s: (b * ns + s, 0))
    return pl.pallas_call(
        _gdn_scan_kernel,
        grid=(bsz, ns),
        in_specs=[blk(), blk(), blk(), blk(), blk(),
                  pl.BlockSpec((g8, wd), lambda b, s: (b * ns + s, 0)),
                  pl.BlockSpec((rows, wd), lambda b, s: (b * ns + s, COL_BG * LANE // wd)),
                  _layer_spec(lyr, (1, GDN_HEAD), lambda b, s: (0, 0))],
        out_specs=blk(),
        out_shape=jax.ShapeDtypeStruct((bsz * seq, wd), BF16),
        scratch_shapes=[pltpu.VMEM((GDN_HEADS, GDN_HEAD, GDN_HEAD), F32),
                        pltpu.VMEM((rows, wd), F32)],
        compiler_params=pltpu.CompilerParams(
            dimension_semantics=("parallel", "arbitrary"), vmem_limit_bytes=VMEM_LIMIT),
        name="gdn_scan",
    )(u, w, qd, kd, qk, gl, proj, ng)


def _pair_mask():
    r = lax.broadcasted_iota(jnp.int32, (LANE, LANE), 0) // RWKV_HEAD
    c = lax.broadcasted_iota(jnp.int32, (LANE, LANE), 1) // RWKV_HEAD
    return r == c


def _rwkv_prep_kernel(r_ref, k_ref, v_ref, wl_ref, al_ref,
                      mur_ref, muk_ref, muv_ref, muwl_ref, mual_ref,
                      w0_ref, wup_ref, a0_ref, aup_ref, kk_ref, ka_ref, rk_ref,
                      u0_ref, o0_ref, wt_ref, rt_ref, mrb_ref, b2_ref, kv_ref, pc_ref, bonus_ref,
                      rs_ref, k2_ref, vs_ref, kn_ref, kna_ref, lw_ref, *, group):
    seq, dw = r_ref.shape
    c = CHUNK
    hd = RWKV_HEAD
    n_chunks = seq // c
    group = min(group, n_chunks)

    seg = _pair_mask().astype(BF16)
    seg2 = jnp.concatenate([seg, seg], axis=0)

    def gates(r0, r1):
        rows = slice(r0, r1)
        first_row = lax.broadcasted_iota(jnp.int32, (r1 - r0, dw), 0) == 0

        def shift(x_ref, mu_ref):
            x = x_ref[rows, :]
            before = jnp.zeros((1, dw), F32) if r0 == 0 else x_ref[r0 - 8:r0, :][7:8, :]
            prev = jnp.where(first_row, before, pltpu.roll(x, 1, axis=0))
            return x + (prev - x) * mu_ref[...]

        r = shift(r_ref, mur_ref)
        k = shift(k_ref, muk_ref)
        v = shift(v_ref, muv_ref)
        wl = shift(wl_ref, muwl_ref)
        al = shift(al_ref, mual_ref)
        z = w0_ref[...] + _dot(jnp.tanh(wl), wup_ref[...])
        lw_ref[rows, :] = -math.exp(-0.5) * _sigmoid(z)
        a = _sigmoid(a0_ref[...] + _dot(al, aup_ref[...]))
        kn = k * kk_ref[...]
        kn = kn * lax.rsqrt(_segsum(kn * kn, seg2) + 1e-6)
        k2 = k * (1.0 + (a - 1.0) * ka_ref[...])
        rs_ref[rows, :] = r
        k2_ref[rows, :] = k2
        vs_ref[rows, :] = v
        kn_ref[rows, :] = kn
        kna_ref[rows, :] = kn * a
        bonus_ref[rows, :] = _segsum(r * k2 * rk_ref[...], seg2) * v

    incl = _chunk_masks()[0]
    incl2, strict2, _, strict_upper2, eye2 = _chunk_masks(2 * c)
    ltri = incl.astype(BF16)
    ltri3 = jnp.concatenate([ltri, ltri, ltri], axis=1)
    head0w = lax.broadcasted_iota(jnp.int32, (1, 2 * dw), 1) % dw < hd
    head0 = head0w[:, 0:dw]
    head_masks = [head0, jnp.logical_not(head0)]

    def prepare(n):
        sls = [slice((n * group + i) * c, (n * group + i + 1) * c) for i in range(group)]
        lws = [lw_ref[s, :] for s in sls]
        cums = [_cumsum_rows(ltri3, lw) for lw in lws]
        knas = [kna_ref[s, :] for s in sls]
        k2s = [k2_ref[s, :] for s in sls]
        vvs = [vs_ref[s, :] for s in sls]
        einvs = [jnp.exp(-cum) for cum in cums]
        a_ts = [-kn_ref[s, :] * jnp.exp(cum - lw) for s, cum, lw in zip(sls, cums, lws)]
        r_ts = [rs_ref[s, :] * jnp.exp(cum) for s, cum in zip(sls, cums)]
        b_ts = [(kna * e).astype(BF16) for kna, e in zip(knas, einvs)]
        a_bs = [a.astype(BF16) for a in a_ts]
        r_bs = [r.astype(BF16) for r in r_ts]
        zero = jnp.zeros((), BF16)

        def by_head(x):
            zx = jnp.zeros((), x.dtype)
            return jnp.concatenate([jnp.where(head0w[:, 0:x.shape[1]], x, zx),
                                    jnp.where(head0w[:, 0:x.shape[1]], zx, x)], axis=0)

        aas = [jnp.concatenate([a, a], axis=0) for a in a_bs]
        prob = [(i, m) for i in range(group) for m in head_masks]
        a_ab_ts = [_dot_nt(jnp.where(m, b_ts[i], zero), aas[i]) for i, m in prob]
        m_rbs = [_dot_nt(r, by_head(b)) for r, b in zip(r_bs, b_ts)]
        g_ks = [_dot_nt(jnp.concatenate([a, r], axis=0), by_head((k2c * e).astype(BF16)))
                for a, r, k2c, e in zip(a_bs, r_bs, k2s, einvs)]
        t_mat_ts = _tri_inverse_t([jnp.where(strict_upper2, g, 0.0) for g in a_ab_ts], eye2)
        vbds = [by_head(v.astype(BF16)) for v in vvs]
        akvs = [_dot(jnp.where(strict2, g[0:c, :], 0.0), vbd) for g, vbd in zip(g_ks, vbds)]
        o0s = [_dot(jnp.where(incl2, g[c:, :], 0.0), vbd) for g, vbd in zip(g_ks, vbds)]
        tws = [_dot_tn(jnp.concatenate([t_mat_ts[2 * i], t_mat_ts[2 * i + 1]], axis=0),
                       by_head(jnp.concatenate([a_ts[i], akvs[i]], axis=1)))[c:, :]
               for i in range(group)]
        for i in range(group):
            s = sls[i]
            pick = lambda x0, x1: jnp.where(head0, x0, x1)
            clast = cums[i][c - 1:c, :]
            tail = jnp.exp(clast - cums[i])
            kv_full = _dot_tn(k2s[i] * tail, vvs[i])
            pc_full = jnp.broadcast_to(jnp.exp(clast), (dw, dw)).T
            u0_ref[s, :] = tws[i][:, dw:]
            o0_ref[s, :] = o0s[i]
            wt_ref[s, :] = tws[i][:, 0:dw].astype(BF16)
            rt_ref[s, :] = r_bs[i]
            mrb_ref[s, :] = jnp.where(incl2, m_rbs[i], 0.0).astype(BF16)
            b2_ref[s, :] = (knas[i] * tail).astype(BF16)
            kv_ref[s, :] = pick(kv_full[0:c, :], kv_full[c:, :])
            pc_ref[s, :] = pick(pc_full[0:c, :], pc_full[c:, :])

    for part in range(n_chunks // group):
        gates(part * group * c, (part + 1) * group * c)
        prepare(part)


def _rwkv_prep(lyr, proj, bsz, seq, mu, w0, wup, a0, aup, kk, ka, rk, group=16):
    group = min(group, seq // CHUNK)
    assert seq % (CHUNK * group) == 0, seq
    dw = LANE
    np_ = RWKV_PAIRS
    col = lambda base: pl.BlockSpec((seq, dw), lambda b, p: (b, base + p))
    fixed_col = lambda idx: pl.BlockSpec((seq, dw), lambda b, p: (b, idx))
    vec = lambda base: _layer_spec(lyr, (1, dw), lambda b, p: (0, base + p))
    fixed_vec = lambda idx: _layer_spec(lyr, (1, dw), lambda b, p: (0, idx))
    lora = lambda: _layer_spec(lyr, (dw, dw), lambda b, p: (0, p))
    out = lambda: pl.BlockSpec((seq, dw), lambda b, p: (b, p))
    full = lambda: pltpu.VMEM((seq, dw), F32)
    sds = lambda dt: jax.ShapeDtypeStruct((bsz * seq, RWKV_WIDTH), dt)
    return pl.pallas_call(
        functools.partial(_rwkv_prep_kernel, group=group),
        grid=(bsz, np_),
        in_specs=[col(COL_CR), col(COL_CK), col(COL_CV), fixed_col(COL_CWL), fixed_col(COL_CAL),
                  vec(0), vec(np_), vec(2 * np_), fixed_vec(3 * np_), fixed_vec(3 * np_ + 1),
                  vec(0), lora(), vec(0), lora(), vec(0), vec(0), vec(0)],
        out_specs=[out() for _ in range(9)],
        out_shape=[sds(F32), sds(F32), sds(BF16), sds(BF16), sds(BF16), sds(BF16),
                   sds(F32), sds(F32), sds(F32)],
        scratch_shapes=[full(), full(), full(), full(), full(), full()],
        compiler_params=pltpu.CompilerParams(
            dimension_semantics=("parallel", "parallel"), vmem_limit_bytes=VMEM_LIMIT),
        name="rwkv_prep",
    )(proj, proj, proj, proj, proj, mu, mu, mu, mu, mu, w0, wup, a0, aup, kk, ka, rk)


def _rwkv_scan_kernel(u0_ref, o0_ref, wt_ref, rt_ref, mrb_ref, b2_ref, kv_ref, pc_ref, bonus_ref,
                      gate_ref, gnw_ref, gnb_ref, y_ref, state_ref, o_ref):
    rows = u0_ref.shape[0]
    c = CHUNK
    dw = LANE
    pair = _pair_mask()

    @pl.when(pl.program_id(1) == 0)
    def _():
        state_ref[...] = jnp.zeros_like(state_ref)

    def block_diag(x):
        return jnp.where(pair, jnp.concatenate([x, x], axis=0), 0.0)

    pairs = range(RWKV_PAIRS)
    lanes = [slice(p * dw, (p + 1) * dw) for p in pairs]

    def step(n, carry):
        sl = pl.ds(pl.multiple_of(n * c, c), c)
        hms = [state_ref[p] for p in pairs]
        whs = [jnp.dot(jnp.concatenate([wt_ref[sl, ls], rt_ref[sl, ls]], axis=0), hm.astype(BF16),
                       preferred_element_type=F32) for ls, hm in zip(lanes, hms)]
        us = [u0_ref[sl, ls] + wh[0:c, :] for ls, wh in zip(lanes, whs)]
        upds = [lax.dot_general(b2_ref[sl, ls], u.astype(BF16), (((0,), (0,)), ((), ())),
                                preferred_element_type=F32) for ls, u in zip(lanes, us)]
        outs = [jnp.dot(mrb_ref[sl, ls], block_diag(u).astype(BF16), preferred_element_type=F32)
                for ls, u in zip(lanes, us)]
        for p in pairs:
            ls = lanes[p]
            pc = pc_ref[sl, ls]
            state_ref[p] = (jnp.concatenate([pc, pc], axis=0) * hms[p]
                            + jnp.where(pair, upds[p], 0.0) + block_diag(kv_ref[sl, ls]))
            o_ref[sl, ls] = whs[p][c:, :] + outs[p] + o0_ref[sl, ls]
        return carry

    lax.fori_loop(0, rows // c, step, 0)

    seg = pair.astype(BF16)
    seg2 = jnp.concatenate([seg, seg], axis=0)
    inv_n = 1.0 / RWKV_HEAD
    for p in range(RWKV_PAIRS):
        ls = slice(p * dw, (p + 1) * dw)
        o = o_ref[:, ls]
        cen = o - _segsum(o, seg2) * inv_n
        var = _segsum(cen * cen, seg2) * inv_n
        yn = cen * lax.rsqrt(var + RWKV_GN_EPS) * gnw_ref[:, ls] + gnb_ref[:, ls]
        y_ref[:, ls] = ((yn + bonus_ref[:, ls]) * _silu(gate_ref[:, ls])).astype(y_ref.dtype)


def _rwkv_scan(lyr, prep, proj, bsz, seq, gnw, gnb):
    rows = min(SCAN_ROWS, seq)
    assert seq % rows == 0 and rows % CHUNK == 0, seq
    ns = seq // rows
    wd = RWKV_WIDTH
    blk = lambda: pl.BlockSpec((rows, wd), lambda b, s: (b * ns + s, 0))
    vec = lambda: _layer_spec(lyr, (1, wd), lambda b, s: (0, 0))
    return pl.pallas_call(
        _rwkv_scan_kernel,
        grid=(bsz, ns),
        in_specs=[blk() for _ in range(9)]
        + [pl.BlockSpec((rows, wd), lambda b, s: (b * ns + s, COL_CG * LANE // wd)), vec(), vec()],
        out_specs=blk(),
        out_shape=jax.ShapeDtypeStruct((bsz * seq, wd), BF16),
        scratch_shapes=[pltpu.VMEM((RWKV_PAIRS, LANE, LANE), F32),
                        pltpu.VMEM((rows, wd), F32)],
        compiler_params=pltpu.CompilerParams(
            dimension_semantics=("parallel", "arbitrary"), vmem_limit_bytes=VMEM_LIMIT),
        name="rwkv_scan",
    )(*prep, proj, gnw, gnb)


def _pack_source(j):
    a1 = 2 * LRU_WIDTH
    b1 = a1 + 4 * GDN_WIDTH
    c0 = b1 + 2 * GDN_HEADS
    c1 = c0 + 3 * RWKV_WIDTH
    c2 = c1 + 2 * LORA
    start, valid = jnp.int32(0), jnp.int32(0)
    for first, count, src, width in ((COL_BQ, COL_CR - COL_BQ, a1, LANE),
                                     (COL_CR, COL_CG - COL_CR, c0, LANE),
                                     (COL_CG, COL_AX - COL_CG, c2, LANE),
                                     (COL_AX, COL_BBA - COL_AX, 0, LANE),
                                     (COL_BBA, 1, b1, 2 * GDN_HEADS),
                                     (COL_CWL, 1, c1, LORA),
                                     (COL_CAL, 1, c1 + LORA, LORA)):
        inside = (j >= first) & (j < first + count)
        start = jnp.where(inside, src + (j - first) * LANE, start)
        valid = jnp.where(inside, width, valid)
    return start, valid


def _pack_w_in_kernel(w_ref, o_ref):
    _, valid = _pack_source(pl.program_id(0))
    keep = lax.broadcasted_iota(jnp.int32, (w_ref.shape[0], w_ref.shape[2]), 0) < valid
    for lyr in range(w_ref.shape[1]):
        o_ref[lyr] = jnp.where(keep, w_ref[:, lyr, :], 0.0).astype(BF16)


def _pack_w_in(w_in):
    lyr, d, n = w_in.shape
    w_t = jnp.transpose(w_in, (2, 0, 1))
    return pl.pallas_call(
        _pack_w_in_kernel,
        grid=(N_PACKED // LANE,),
        in_specs=[pl.BlockSpec((pl.Element(LANE), pl.Element(lyr), pl.Element(d)),
                               lambda j: (_pack_source(j)[0], 0, 0))],
        out_specs=pl.BlockSpec((lyr, LANE, d), lambda j: (0, j, 0)),
        out_shape=jax.ShapeDtypeStruct((lyr, N_PACKED, d), BF16),
        compiler_params=pltpu.CompilerParams(
            dimension_semantics=("parallel",), vmem_limit_bytes=VMEM_LIMIT),
        name="pack_w_in",
    )(w_t)


def _pack_mu(mu):
    lyr = mu.shape[0]
    z = jnp.zeros((lyr, LANE - LORA), mu.dtype)
    c1 = 3 * RWKV_WIDTH
    return jnp.concatenate([mu[:, :c1], mu[:, c1:c1 + LORA], z, mu[:, c1 + LORA:], z], axis=-1)


def _block_diag(w):
    lyr, nb, bi, bj = w.shape
    eye = jnp.eye(nb, dtype=w.dtype)
    return jnp.einsum('lnij,nm->lnimj', w, eye).reshape(lyr, nb * bi, nb * bj)


def kernel(x, norm_g, w_in, w_out, lru_conv_w, lru_conv_b, lru_wx, lru_bx, lru_wa, lru_ba, lru_lambda,
           gdn_conv_w, gdn_a_log, gdn_dt_bias, gdn_norm_g, rwkv_mu, rwkv_w0, rwkv_w_up, rwkv_a0,
           rwkv_a_up, rwkv_k_k, rwkv_k_a, rwkv_r_k, rwkv_gn_w, rwkv_gn_b, final_norm_g):
    bsz, seq, d = x.shape
    depth = w_in.shape[0]
    x2 = x.reshape(bsz * seq, d)

    w_in_p = _pack_w_in(w_in)
    w_out_b = w_out.astype(BF16)
    wx_bd = _block_diag(lru_wx).astype(BF16)
    wa_bd = _block_diag(lru_wa).astype(BF16)
    mu_p = _pack_mu(rwkv_mu)
    pad_rows = lambda w: jnp.pad(w, ((0, 0), (0, LANE - LORA), (0, 0))).astype(BF16)
    wup_p = pad_rows(rwkv_w_up)
    aup_p = pad_rows(rwkv_a_up)
    gp = jnp.zeros((depth, 2, LANE), F32)
    gp = gp.at[:, 0, GDN_HEADS:2 * GDN_HEADS].set(gdn_a_log)
    gp = gp.at[:, 1, GDN_HEADS:2 * GDN_HEADS].set(gdn_dt_bias)
    rows = lambda t: t.reshape(depth, 1, -1)
    norm_g, lru_conv_b, lru_bx, lru_ba, lru_lambda, gdn_norm_g = map(
        rows, (norm_g, lru_conv_b, lru_bx, lru_ba, lru_lambda, gdn_norm_g))
    mu_p, rwkv_w0, rwkv_a0, rwkv_k_k, rwkv_k_a, rwkv_r_k, rwkv_gn_w, rwkv_gn_b = map(
        rows, (mu_p, rwkv_w0, rwkv_a0, rwkv_k_k, rwkv_k_a, rwkv_r_k, rwkv_gn_w, rwkv_gn_b))
    final_g = final_norm_g.reshape(1, -1)

    for l in range(depth):
        proj = _inproj(l, x2, norm_g, w_in_p)
        ya = _lru(l, proj, bsz, seq, lru_conv_w, lru_conv_b, wx_bd, lru_bx, wa_bd, lru_ba, lru_lambda)
        gdn_ops = _gdn_prep(l, proj, bsz, seq, gdn_conv_w, gp)
        yb = _gdn_scan(l, *gdn_ops, proj, bsz, seq, gdn_norm_g)
        rwkv_ops = _rwkv_prep(l, proj, bsz, seq, mu_p, rwkv_w0, wup_p, rwkv_a0, aup_p,
                              rwkv_k_k, rwkv_k_a, rwkv_r_k)
        yc = _rwkv_scan(l, rwkv_ops, proj, bsz, seq, rwkv_gn_w, rwkv_gn_b)
        x2 = _outproj(l, ya, yb, yc, x2, w_out_b, final_g, final_norm=(l == depth - 1))
    return x2.reshape(bsz, seq, d)
```

```python
import functools
import math

import jax
import jax.numpy as jnp
from jax import lax
from jax.experimental import pallas as pl
from jax.experimental.pallas import tpu as pltpu

F32 = jnp.float32
BF16 = jnp.bfloat16

NORM_EPS = 1e-6
CONV_W = 4
CHUNK = 64
LRU_WIDTH = 512
LRU_C = 8.0
GDN_HEAD = 128
GDN_HEADS = 6
GDN_WIDTH = GDN_HEADS * GDN_HEAD
RWKV_HEAD = 64
RWKV_WIDTH = 768
LORA = 96
RWKV_GN_EPS = 64e-5

LANE = 128
RWKV_PAIRS = RWKV_WIDTH // LANE
N_PACKED = 7680
COL_BQ, COL_BK, COL_BV, COL_BG = 0, 6, 12, 18
COL_CR, COL_CK, COL_CV, COL_CG = 24, 30, 36, 42
COL_AX, COL_AG = 48, 52
COL_BBA, COL_CWL, COL_CAL = 56, 57, 58
SCAN_ROWS = 512

VMEM_LIMIT = 52 * 1024 * 1024


def _dot(a, b):
    return jnp.dot(a.astype(BF16), b.astype(BF16), preferred_element_type=F32)


def _dot_nt(a, b):
    return lax.dot_general(a.astype(BF16), b.astype(BF16), (((1,), (1,)), ((), ())),
                           preferred_element_type=F32)


def _dot_tn(a, b):
    return lax.dot_general(a.astype(BF16), b.astype(BF16), (((0,), (0,)), ((), ())),
                           preferred_element_type=F32)


def _split2(x):
    hi = x.astype(BF16)
    lo = (x - hi.astype(F32)).astype(BF16)
    return hi, lo


def _split3(x):
    hi = x.astype(BF16)
    r1 = x - hi.astype(F32)
    mid = r1.astype(BF16)
    lo = (r1 - mid.astype(F32)).astype(BF16)
    return hi, mid, lo


def _cumsum_rows(ltri3, x):
    return jnp.dot(ltri3, jnp.concatenate(_split3(x), axis=0), preferred_element_type=F32)


def _segsum(x, seg2):
    return jnp.dot(jnp.concatenate(_split2(x), axis=1), seg2, preferred_element_type=F32)


def _tri_inverse_t(a_twins, eye_twin):
    c, w = eye_twin.shape
    first_half = lax.broadcasted_iota(jnp.int32, (c, w), 1) < c
    xs = [jnp.where(first_half, a, eye_twin) for a in a_twins]
    for _ in range(6):
        xbs = [x.astype(BF16) for x in xs]
        outs = [jnp.dot(xb[:, 0:c], xb, preferred_element_type=F32) for xb in xbs]
        xs = [jnp.where(first_half, o, x + o) for x, o in zip(xs, outs)]
    return xs


def _chunk_masks(width=CHUNK):
    row = lax.broadcasted_iota(jnp.int32, (CHUNK, width), 0)
    col = lax.broadcasted_iota(jnp.int32, (CHUNK, width), 1) % CHUNK
    return row >= col, row > col, row <= col, row < col, (row == col).astype(F32)


def _softplus(x):
    return jnp.maximum(x, 0.0) + jnp.log1p(jnp.exp(-jnp.abs(x)))


def _sigmoid(x):
    return 0.5 * jnp.tanh(0.5 * x) + 0.5


def _silu(x):
    h = 0.5 * x
    return h * jnp.tanh(h) + h


def _inproj_kernel(x_ref, g_ref, w_ref, o_ref, h_ref):
    @pl.when(pl.program_id(1) == 0)
    def _():
        x = x_ref[...]
        ms = jnp.mean(x * x, axis=-1, keepdims=True)
        h_ref[...] = (x * lax.rsqrt(ms + NORM_EPS) * g_ref[...]).astype(BF16)

    o_ref[...] = lax.dot_general(h_ref[...], w_ref[...], (((1,), (1,)), ((), ())),
                                 preferred_element_type=F32)


def _layer_spec(lyr, block, index):
    return pl.BlockSpec((None,) + block, lambda *grid_idx: (lyr,) + index(*grid_idx))


def _inproj(lyr, x2, g, w, tm=1024, tn=1536):
    m, d = x2.shape
    n = w.shape[1]
    tm = min(tm, m)
    assert m % tm == 0 and n % tn == 0 and w.shape[2] == d, (x2.shape, w.shape)
    return pl.pallas_call(
        _inproj_kernel,
        grid=(m // tm, n // tn),
        in_specs=[pl.BlockSpec((tm, d), lambda i, j: (i, 0)),
                  _layer_spec(lyr, (1, d), lambda i, j: (0, 0)),
                  _layer_spec(lyr, (tn, d), lambda i, j: (j, 0))],
        out_specs=pl.BlockSpec((tm, tn), lambda i, j: (i, j)),
        out_shape=jax.ShapeDtypeStruct((m, n), F32),
        scratch_shapes=[pltpu.VMEM((tm, d), BF16)],
        compiler_params=pltpu.CompilerParams(
            dimension_semantics=("parallel", "arbitrary"), vmem_limit_bytes=VMEM_LIMIT),
        name="inproj",
    )(x2, g, w)


def _outproj_kernel(ya_ref, yb_ref, yc_ref, x_ref, w_ref, fg_ref, o_ref, *, final_norm):
    wa = LRU_WIDTH
    wb = LRU_WIDTH + GDN_WIDTH
    acc = jnp.dot(ya_ref[...], w_ref[0:wa, :], preferred_element_type=F32)
    acc += jnp.dot(yb_ref[...], w_ref[wa:wb, :], preferred_element_type=F32)
    acc += jnp.dot(yc_ref[...], w_ref[wb:, :], preferred_element_type=F32)
    xn = x_ref[...] + acc
    if final_norm:
        ms = jnp.mean(xn * xn, axis=-1, keepdims=True)
        xn = xn * lax.rsqrt(ms + NORM_EPS) * fg_ref[...]
    o_ref[...] = xn


def _outproj(lyr, ya, yb, yc, x2, w, fg, final_norm, tm=512):
    m, d = x2.shape
    tm = min(tm, m)
    assert m % tm == 0 and d == LRU_WIDTH + GDN_WIDTH + RWKV_WIDTH, x2.shape
    return pl.pallas_call(
        functools.partial(_outproj_kernel, final_norm=final_norm),
        grid=(m // tm,),
        in_specs=[pl.BlockSpec((tm, LRU_WIDTH), lambda i: (i, 0)),
                  pl.BlockSpec((tm, GDN_WIDTH), lambda i: (i, 0)),
                  pl.BlockSpec((tm, RWKV_WIDTH), lambda i: (i, 0)),
                  pl.BlockSpec((tm, d), lambda i: (i, 0)),
                  _layer_spec(lyr, (d, d), lambda i: (0, 0)),
                  pl.BlockSpec((1, d), lambda i: (0, 0))],
        out_specs=pl.BlockSpec((tm, d), lambda i: (i, 0)),
        out_shape=jax.ShapeDtypeStruct((m, d), F32),
        compiler_params=pltpu.CompilerParams(
            dimension_semantics=("parallel",), vmem_limit_bytes=VMEM_LIMIT),
        name="outproj",
    )(ya, yb, yc, x2, w, fg)


def _lru_kernel(x_ref, g_ref, cw_ref, cb_ref, wx_ref, bx_ref, wa_ref, ba_ref, lam_ref,
                y_ref, ext_ref, h_ref):
    ts, w = x_ref.shape

    @pl.when(pl.program_id(1) == 0)
    def _():
        ext_ref[0:8, :] = jnp.zeros((8, w), F32)
        h_ref[...] = jnp.zeros_like(h_ref)

    ext_ref[8:, :] = x_ref[...]
    cw = cw_ref[...]
    xc = cb_ref[...] + cw[0:1, :] * ext_ref[pl.ds(5, ts), :]
    for j in range(1, CONV_W):
        xc = xc + cw[j:j + 1, :] * ext_ref[pl.ds(5 + j, ts), :]
    ext_ref[0:8, :] = ext_ref[pl.ds(ts, 8), :]

    gate_x = _sigmoid(_dot(xc, wx_ref[...]) + bx_ref[...])
    gate_a = _sigmoid(_dot(xc, wa_ref[...]) + ba_ref[...])
    log_a = -LRU_C * gate_a * _softplus(-lam_ref[...])
    a = jnp.exp(log_a)
    mult = jnp.sqrt(jnp.maximum(-jnp.tanh(log_a) * (a * a + 1.0), 0.0))
    u = mult * (gate_x * xc)

    row = lax.broadcasted_iota(jnp.int32, (ts, w), 0)
    d = 1
    while d < ts:
        keep = row >= d
        u = jnp.where(keep, a * pltpu.roll(u, d, axis=0) + u, u)
        a = jnp.where(keep, a * pltpu.roll(a, d, axis=0), a)
        d *= 2
    h = u + a * h_ref[0:1, :]
    h_ref[0:1, :] = h[ts - 1:ts, :]
    y_ref[...] = (h * _silu(g_ref[...])).astype(y_ref.dtype)


def _lru(lyr, proj, bsz, seq, cw, cb, wx, bx, wa, ba, lam, ts=512):
    ts = min(ts, seq)
    assert seq % ts == 0, seq
    ns = seq // ts
    w = LRU_WIDTH
    vec = lambda: _layer_spec(lyr, (1, w), lambda b, s: (0, 0))
    return pl.pallas_call(
        _lru_kernel,
        grid=(bsz, ns),
        in_specs=[pl.BlockSpec((ts, w), lambda b, s: (b * ns + s, COL_AX * LANE // w)),
                  pl.BlockSpec((ts, w), lambda b, s: (b * ns + s, COL_AG * LANE // w)),
                  _layer_spec(lyr, (CONV_W, w), lambda b, s: (0, 0)),
                  vec(),
                  _layer_spec(lyr, (w, w), lambda b, s: (0, 0)),
                  vec(),
                  _layer_spec(lyr, (w, w), lambda b, s: (0, 0)),
                  vec(), vec()],
        out_specs=pl.BlockSpec((ts, w), lambda b, s: (b * ns + s, 0)),
        out_shape=jax.ShapeDtypeStruct((bsz * seq, w), BF16),
        scratch_shapes=[pltpu.VMEM((ts + 8, w), F32), pltpu.VMEM((8, w), F32)],
        compiler_params=pltpu.CompilerParams(
            dimension_semantics=("parallel", "arbitrary"), vmem_limit_bytes=VMEM_LIMIT),
        name="lru",
    )(proj, proj, cw, cb, wx, bx, wa, ba, lam)


def _gdn_prep_kernel(q_ref, k_ref, v_ref, ba_ref, cwq_ref, cwk_ref, cwv_ref, gp_ref,
                     u_ref, w_ref, qd_ref, kd_ref, qk_ref, gl_ref,
                     ext_ref, qs_ref, ks_ref, vs_ref, bt_ref, gg_ref, *, group):
    head = pl.program_id(1)
    seq, dh = q_ref.shape
    c = CHUNK
    n_chunks = seq // c
    group = min(group, n_chunks)

    def conv_silu(x_ref, cw_ref):
        ext_ref[0:8, :] = jnp.zeros((8, dh), F32)
        ext_ref[8:, :] = x_ref[...]
        cw = cw_ref[...]
        xc = cw[0:1, :] * ext_ref[pl.ds(5, seq), :]
        for j in range(1, CONV_W):
            xc = xc + cw[j:j + 1, :] * ext_ref[pl.ds(5 + j, seq), :]
        return _silu(xc)

    def l2norm(t):
        return t * lax.rsqrt(jnp.sum(t * t, axis=-1, keepdims=True) + 1e-6)

    qs_ref[...] = l2norm(conv_silu(q_ref, cwq_ref)) * (GDN_HEAD ** -0.5)
    ks_ref[...] = l2norm(conv_silu(k_ref, cwk_ref))
    vs_ref[...] = conv_silu(v_ref, cwv_ref)

    ba = ba_ref[...]
    lane = lax.broadcasted_iota(jnp.int32, ba.shape, 1)
    gp = gp_ref[...]
    beta_all = _sigmoid(ba)
    g_all = -jnp.exp(gp[0:1, :]) * _softplus(ba + gp[1:2, :])
    beta_col = jnp.sum(jnp.where(lane == head, beta_all, 0.0), axis=1, keepdims=True)
    g_col = jnp.sum(jnp.where(lane == head + GDN_HEADS, g_all, 0.0), axis=1, keepdims=True)
    bt_ref[...] = jnp.broadcast_to(beta_col, (seq, dh))
    gg_ref[...] = jnp.broadcast_to(g_col, (seq, dh))

    incl = _chunk_masks()[0]
    incl2, _, upper2, strict_upper2, eye2 = _chunk_masks(2 * c)
    first_copy = lax.broadcasted_iota(jnp.int32, (c, dh), 1) < c
    ltri = incl.astype(BF16)
    ltri3 = jnp.concatenate([ltri, ltri, ltri], axis=1)

    def prepare(n, carry):
        rng = range(group)
        sls = [pl.ds(pl.multiple_of((n * group + i) * c, c), c) for i in rng]
        qs = [qs_ref[s, :] for s in sls]
        ks = [ks_ref[s, :] for s in sls]
        betas = [bt_ref[s, :] for s in sls]
        decs = [_cumsum_rows(ltri3, gg_ref[s, :]) for s in sls]
        kbs = [k * b for k, b in zip(ks, betas)]
        gram_ts = [_dot_nt(k, jnp.concatenate([kb, kb], axis=0)) for k, kb in zip(ks, kbs)]
        gram_qs = [_dot_nt(q, jnp.concatenate([k, k], axis=0)) for q, k in zip(qs, ks)]
        lmasks, lmask_ts = [], []
        for dec in decs:
            diff = dec - jnp.concatenate([dec, dec], axis=0).T[0:c, :]
            lmasks.append(jnp.where(incl2, jnp.exp(jnp.where(incl2, diff, 0.0)), 0.0))
            lmask_ts.append(jnp.where(upper2, jnp.exp(jnp.where(upper2, -diff, 0.0)), 0.0))
        t_mat_ts = _tri_inverse_t([jnp.where(strict_upper2, -(g * m), 0.0)
                                   for g, m in zip(gram_ts, lmask_ts)], eye2)
        edecs = [jnp.exp(dec) for dec in decs]
        uws = [_dot_tn(x, jnp.concatenate([vs_ref[s, :] * b, kb * e], axis=1))[c:, :]
               for x, s, b, kb, e in zip(t_mat_ts, sls, betas, kbs, edecs)]
        for i in rng:
            s = sls[i]
            dec = decs[i]
            dlast = dec[c - 1:c, :]
            u_ref[s, :] = uws[i][:, 0:dh]
            w_ref[s, :] = uws[i][:, dh:].astype(BF16)
            qk_ref[s, :] = jnp.where(first_copy, gram_qs[i] * lmasks[i], 0.0).astype(BF16)
            qd_ref[s, :] = (qs[i] * edecs[i]).astype(BF16)
            kd_ref[s, :] = (ks[i] * jnp.exp(dlast - dec)).astype(BF16)
            gl_ref[pl.ds(pl.multiple_of((n * group + i) * 8, 8), 8), :] = jnp.broadcast_to(
                jnp.exp(dlast), (8, dh))
        return carry

    lax.fori_loop(0, n_chunks // group, prepare, 0)


def _gdn_prep(lyr, proj, bsz, seq, cw, gp, group=32):
    group = min(group, seq // CHUNK)
    assert seq % (CHUNK * group) == 0, seq
    dh = GDN_HEAD
    n8 = seq // CHUNK * 8
    col = lambda base: pl.BlockSpec((seq, dh), lambda b, h: (b, base + h))
    cwspec = lambda base: _layer_spec(lyr, (CONV_W, dh), lambda b, h: (0, base + h))
    out = lambda: pl.BlockSpec((seq, dh), lambda b, h: (b, h))
    full = lambda: pltpu.VMEM((seq, dh), F32)
    sds = lambda dt: jax.ShapeDtypeStruct((bsz * seq, GDN_WIDTH), dt)
    return pl.pallas_call(
        functools.partial(_gdn_prep_kernel, group=group),
        grid=(bsz, GDN_HEADS),
        in_specs=[col(COL_BQ), col(COL_BK), col(COL_BV),
                  pl.BlockSpec((seq, dh), lambda b, h: (b, COL_BBA)),
                  cwspec(0), cwspec(GDN_HEADS), cwspec(2 * GDN_HEADS),
                  _layer_spec(lyr, (2, dh), lambda b, h: (0, 0))],
        out_specs=[out(), out(), out(), out(), out(),
                   pl.BlockSpec((n8, dh), lambda b, h: (b, h))],
        out_shape=[sds(F32), sds(BF16), sds(BF16), sds(BF16), sds(BF16),
                   jax.ShapeDtypeStruct((bsz * n8, GDN_WIDTH), F32)],
        scratch_shapes=[pltpu.VMEM((seq + 8, dh), F32), full(), full(), full(), full(), full()],
        compiler_params=pltpu.CompilerParams(
            dimension_semantics=("parallel", "parallel"), vmem_limit_bytes=VMEM_LIMIT),
        name="gdn_prep",
    )(proj, proj, proj, proj, cw, cw, cw, gp)


def _gdn_scan_kernel(u_ref, w_ref, qd_ref, kd_ref, qk_ref, gl_ref, gate_ref, ng_ref,
                     y_ref, state_ref, o_ref):
    rows = u_ref.shape[0]
    c = CHUNK
    dh = GDN_HEAD

    @pl.when(pl.program_id(1) == 0)
    def _():
        state_ref[...] = jnp.zeros_like(state_ref)

    heads = range(GDN_HEADS)
    lanes = [slice(h * dh, (h + 1) * dh) for h in heads]

    def step(n, carry):
        sl = pl.ds(pl.multiple_of(n * c, c), c)
        gsl = pl.ds(pl.multiple_of(n * 8, 8), 8)
        states = [state_ref[h] for h in heads]
        wss = [jnp.dot(jnp.concatenate([w_ref[sl, ls], qd_ref[sl, ls]], axis=0), st.astype(BF16),
                       preferred_element_type=F32) for ls, st in zip(lanes, states)]
        v_news = [(u_ref[sl, ls] - ws[0:c, :]).astype(BF16) for ls, ws in zip(lanes, wss)]
        upds = [lax.dot_general(kd_ref[sl, ls], vn, (((0,), (0,)), ((), ())), preferred_element_type=F32)
                for ls, vn in zip(lanes, v_news)]
        outs = [jnp.dot(qk_ref[sl, ls][:, 0:c], vn, preferred_element_type=F32)
                for ls, vn in zip(lanes, v_news)]
        for h in heads:
            ls = lanes[h]
            state_ref[h] = states[h] * gl_ref[gsl, ls][0:1, :] + upds[h]
            o_ref[sl, ls] = wss[h][c:, :] + outs[h]
        return carry

    lax.fori_loop(0, rows // c, step, 0)

    ng = ng_ref[...]
    for h in range(GDN_HEADS):
        ls = slice(h * dh, (h + 1) * dh)
        o = o_ref[:, ls]
        o = o * lax.rsqrt(jnp.mean(o * o, axis=-1, keepdims=True) + NORM_EPS) * ng
        y_ref[:, ls] = (o * _silu(gate_ref[:, ls])).astype(y_ref.dtype)


def _gdn_scan(lyr, u, w, qd, kd, qk, gl, proj, bsz, seq, ng):
    rows = min(SCAN_ROWS, seq)
    assert seq % rows == 0 and rows % CHUNK == 0, seq
    ns = seq // rows
    g8 = rows // CHUNK * 8
    wd = GDN_WIDTH
    blk = lambda: pl.BlockSpec((rows, wd), lambda b, s: (b * ns + s, 0))
    return pl.pallas_call(
        _gdn_scan_kernel,
        grid=(bsz, ns),
        in_specs=[blk(), blk(), blk(), blk(), blk(),
                  pl.BlockSpec((g8, wd), lambda b, s: (b * ns + s, 0)),
                  pl.BlockSpec((rows, wd), lambda b, s: (b * ns + s, COL_BG * LANE // wd)),
                  _layer_spec(lyr, (1, GDN_HEAD), lambda b, s: (0, 0))],
        out_specs=blk(),
        out_shape=jax.ShapeDtypeStruct((bsz * seq, wd), BF16),
        scratch_shapes=[pltpu.VMEM((GDN_HEADS, GDN_HEAD, GDN_HEAD), F32),
                        pltpu.VMEM((rows, wd), F32)],
        compiler_params=pltpu.CompilerParams(
            dimension_semantics=("parallel", "arbitrary"), vmem_limit_bytes=VMEM_LIMIT),
        name="gdn_scan",
    )(u, w, qd, kd, qk, gl, proj, ng)


def _pair_mask():
    r = lax.broadcasted_iota(jnp.int32, (LANE, LANE), 0) // RWKV_HEAD
    c = lax.broadcasted_iota(jnp.int32, (LANE, LANE), 1) // RWKV_HEAD
    return r == c


def _rwkv_prep_kernel(r_ref, k_ref, v_ref, wl_ref, al_ref,
                      mur_ref, muk_ref, muv_ref, muwl_ref, mual_ref,
                      w0_ref, wup_ref, a0_ref, aup_ref, kk_ref, ka_ref, rk_ref,
                      u0_ref, o0_ref, wt_ref, rt_ref, mrb_ref, b2_ref, kv_ref, pc_ref, bonus_ref,
                      rs_ref, k2_ref, vs_ref, kn_ref, kna_ref, lw_ref, *, group):
    seq, dw = r_ref.shape
    c = CHUNK
    hd = RWKV_HEAD
    n_chunks = seq // c
    group = min(group, n_chunks)

    seg = _pair_mask().astype(BF16)
    seg2 = jnp.concatenate([seg, seg], axis=0)

    def gates(r0, r1):
        rows = slice(r0, r1)
        first_row = lax.broadcasted_iota(jnp.int32, (r1 - r0, dw), 0) == 0

        def shift(x_ref, mu_ref):
            x = x_ref[rows, :]
            before = jnp.zeros((1, dw), F32) if r0 == 0 else x_ref[r0 - 8:r0, :][7:8, :]
            prev = jnp.where(first_row, before, pltpu.roll(x, 1, axis=0))
            return x + (prev - x) * mu_ref[...]

        r = shift(r_ref, mur_ref)
        k = shift(k_ref, muk_ref)
        v = shift(v_ref, muv_ref)
        wl = shift(wl_ref, muwl_ref)
        al = shift(al_ref, mual_ref)
        z = w0_ref[...] + _dot(jnp.tanh(wl), wup_ref[...])
        lw_ref[rows, :] = -math.exp(-0.5) * _sigmoid(z)
        a = _sigmoid(a0_ref[...] + _dot(al, aup_ref[...]))
        kn = k * kk_ref[...]
        kn = kn * lax.rsqrt(_segsum(kn * kn, seg2) + 1e-6)
        k2 = k * (1.0 + (a - 1.0) * ka_ref[...])
        rs_ref[rows, :] = r
        k2_ref[rows, :] = k2
        vs_ref[rows, :] = v
        kn_ref[rows, :] = kn
        kna_ref[rows, :] = kn * a
        bonus_ref[rows, :] = _segsum(r * k2 * rk_ref[...], seg2) * v

    incl = _chunk_masks()[0]
    incl2, strict2, _, strict_upper2, eye2 = _chunk_masks(2 * c)
    ltri = incl.astype(BF16)
    ltri3 = jnp.concatenate([ltri, ltri, ltri], axis=1)
    head0w = lax.broadcasted_iota(jnp.int32, (1, 2 * dw), 1) % dw < hd
    head0 = head0w[:, 0:dw]
    head_masks = [head0, jnp.logical_not(head0)]

    def prepare(n):
        sls = [slice((n * group + i) * c, (n * group + i + 1) * c) for i in range(group)]
        lws = [lw_ref[s, :] for s in sls]
        cums = [_cumsum_rows(ltri3, lw) for lw in lws]
        knas = [kna_ref[s, :] for s in sls]
        k2s = [k2_ref[s, :] for s in sls]
        vvs = [vs_ref[s, :] for s in sls]
        einvs = [jnp.exp(-cum) for cum in cums]
        a_ts = [-kn_ref[s, :] * jnp.exp(cum - lw) for s, cum, lw in zip(sls, cums, lws)]
        r_ts = [rs_ref[s, :] * jnp.exp(cum) for s, cum in zip(sls, cums)]
        b_ts = [(kna * e).astype(BF16) for kna, e in zip(knas, einvs)]
        a_bs = [a.astype(BF16) for a in a_ts]
        r_bs = [r.astype(BF16) for r in r_ts]
        zero = jnp.zeros((), BF16)

        def by_head(x):
            zx = jnp.zeros((), x.dtype)
            return jnp.concatenate([jnp.where(head0w[:, 0:x.shape[1]], x, zx),
                                    jnp.where(head0w[:, 0:x.shape[1]], zx, x)], axis=0)

        prob = [(i, m) for i in range(group) for m in head_masks]
        a_ab_ts = [jnp.pad(_dot_nt(jnp.where(m, b_ts[i], zero), a_bs[i]), ((0, 0), (0, c)))
                   for i, m in prob]
        m_rbs = [_dot_nt(r, by_head(b)) for r, b in zip(r_bs, b_ts)]
        g_ks = [_dot_nt(jnp.concatenate([a, r], axis=0), by_head((k2c * e).astype(BF16)))
                for a, r, k2c, e in zip(a_bs, r_bs, k2s, einvs)]
        t_mat_ts = _tri_inverse_t([jnp.where(strict_upper2, g, 0.0) for g in a_ab_ts], eye2)
        vbds = [by_head(v.astype(BF16)) for v in vvs]
        akvs = [_dot(jnp.where(strict2, g[0:c, :], 0.0), vbd) for g, vbd in zip(g_ks, vbds)]
        o0s = [_dot(jnp.where(incl2, g[c:, :], 0.0), vbd) for g, vbd in zip(g_ks, vbds)]
        tws = [_dot_tn(jnp.concatenate([t_mat_ts[2 * i], t_mat_ts[2 * i + 1]], axis=0),
                       by_head(jnp.concatenate([a_ts[i], akvs[i]], axis=1)))[c:, :]
               for i in range(group)]
        for i in range(group):
            s = sls[i]
            pick = lambda x0, x1: jnp.where(head0, x0, x1)
            clast = cums[i][c - 1:c, :]
            tail = jnp.exp(clast - cums[i])
            kv_full = _dot_tn(k2s[i] * tail, vvs[i])
            pc_full = jnp.broadcast_to(jnp.exp(clast), (dw, dw)).T
            u0_ref[s, :] = tws[i][:, dw:]
            o0_ref[s, :] = o0s[i]
            wt_ref[s, :] = tws[i][:, 0:dw].astype(BF16)
            rt_ref[s, :] = r_bs[i]
            mrb_ref[s, :] = jnp.where(incl2, m_rbs[i], 0.0).astype(BF16)
            b2_ref[s, :] = (knas[i] * tail).astype(BF16)
            kv_ref[s, :] = pick(kv_full[0:c, :], kv_full[c:, :])
            pc_ref[s, :] = pick(pc_full[0:c, :], pc_full[c:, :])

    for part in range(n_chunks // group):
        gates(part * group * c, (part + 1) * group * c)
        prepare(part)


def _rwkv_prep(lyr, proj, bsz, seq, mu, w0, wup, a0, aup, kk, ka, rk, group=16):
    group = min(group, seq // CHUNK)
    assert seq % (CHUNK * group) == 0, seq
    dw = LANE
    np_ = RWKV_PAIRS
    col = lambda base: pl.BlockSpec((seq, dw), lambda b, p: (b, base + p))
    fixed_col = lambda idx: pl.BlockSpec((seq, dw), lambda b, p: (b, idx))
    vec = lambda base: _layer_spec(lyr, (1, dw), lambda b, p: (0, base + p))
    fixed_vec = lambda idx: _layer_spec(lyr, (1, dw), lambda b, p: (0, idx))
    lora = lambda: _layer_spec(lyr, (dw, dw), lambda b, p: (0, p))
    out = lambda: pl.BlockSpec((seq, dw), lambda b, p: (b, p))
    full = lambda: pltpu.VMEM((seq, dw), F32)
    sds = lambda dt: jax.ShapeDtypeStruct((bsz * seq, RWKV_WIDTH), dt)
    return pl.pallas_call(
        functools.partial(_rwkv_prep_kernel, group=group),
        grid=(bsz, np_),
        in_specs=[col(COL_CR), col(COL_CK), col(COL_CV), fixed_col(COL_CWL), fixed_col(COL_CAL),
                  vec(0), vec(np_), vec(2 * np_), fixed_vec(3 * np_), fixed_vec(3 * np_ + 1),
                  vec(0), lora(), vec(0), lora(), vec(0), vec(0), vec(0)],
        out_specs=[out() for _ in range(9)],
        out_shape=[sds(F32), sds(F32), sds(BF16), sds(BF16), sds(BF16), sds(BF16),
                   sds(F32), sds(F32), sds(F32)],
        scratch_shapes=[full(), full(), full(), full(), full(), full()],
        compiler_params=pltpu.CompilerParams(
            dimension_semantics=("parallel", "parallel"), vmem_limit_bytes=VMEM_LIMIT),
        name="rwkv_prep",
    )(proj, proj, proj, proj, proj, mu, mu, mu, mu, mu, w0, wup, a0, aup, kk, ka, rk)


def _rwkv_scan_kernel(u0_ref, o0_ref, wt_ref, rt_ref, mrb_ref, b2_ref, kv_ref, pc_ref, bonus_ref,
                      gate_ref, gnw_ref, gnb_ref, y_ref, state_ref, o_ref):
    rows = u0_ref.shape[0]
    c = CHUNK
    dw = LANE
    pair = _pair_mask()

    @pl.when(pl.program_id(1) == 0)
    def _():
        state_ref[...] = jnp.zeros_like(state_ref)

    def block_diag(x):
        return jnp.where(pair, jnp.concatenate([x, x], axis=0), 0.0)

    pairs = range(RWKV_PAIRS)
    lanes = [slice(p * dw, (p + 1) * dw) for p in pairs]

    def step(n, carry):
        sl = pl.ds(pl.multiple_of(n * c, c), c)
        hms = [state_ref[p] for p in pairs]
        whs = [jnp.dot(jnp.concatenate([wt_ref[sl, ls], rt_ref[sl, ls]], axis=0), hm.astype(BF16),
                       preferred_element_type=F32) for ls, hm in zip(lanes, hms)]
        us = [u0_ref[sl, ls] + wh[0:c, :] for ls, wh in zip(lanes, whs)]
        upds = [lax.dot_general(b2_ref[sl, ls], u.astype(BF16), (((0,), (0,)), ((), ())),
                                preferred_element_type=F32) for ls, u in zip(lanes, us)]
        outs = [jnp.dot(mrb_ref[sl, ls], block_diag(u).astype(BF16), preferred_element_type=F32)
                for ls, u in zip(lanes, us)]
        for p in pairs:
            ls = lanes[p]
            pc = pc_ref[sl, ls]
            state_ref[p] = (jnp.concatenate([pc, pc], axis=0) * hms[p]
                            + jnp.where(pair, upds[p], 0.0) + block_diag(kv_ref[sl, ls]))
            o_ref[sl, ls] = whs[p][c:, :] + outs[p] + o0_ref[sl, ls]
        return carry

    lax.fori_loop(0, rows // c, step, 0)

    seg = pair.astype(BF16)
    seg2 = jnp.concatenate([seg, seg], axis=0)
    inv_n = 1.0 / RWKV_HEAD
    for p in range(RWKV_PAIRS):
        ls = slice(p * dw, (p + 1) * dw)
        o = o_ref[:, ls]
        cen = o - _segsum(o, seg2) * inv_n
        var = _segsum(cen * cen, seg2) * inv_n
        yn = cen * lax.rsqrt(var + RWKV_GN_EPS) * gnw_ref[:, ls] + gnb_ref[:, ls]
        y_ref[:, ls] = ((yn + bonus_ref[:, ls]) * _silu(gate_ref[:, ls])).astype(y_ref.dtype)


def _rwkv_scan(lyr, prep, proj, bsz, seq, gnw, gnb):
    rows = min(SCAN_ROWS, seq)
    assert seq % rows == 0 and rows % CHUNK == 0, seq
    ns = seq // rows
    wd = RWKV_WIDTH
    blk = lambda: pl.BlockSpec((rows, wd), lambda b, s: (b * ns + s, 0))
    vec = lambda: _layer_spec(lyr, (1, wd), lambda b, s: (0, 0))
    return pl.pallas_call(
        _rwkv_scan_kernel,
        grid=(bsz, ns),
        in_specs=[blk() for _ in range(9)]
        + [pl.BlockSpec((rows, wd), lambda b, s: (b * ns + s, COL_CG * LANE // wd)), vec(), vec()],
        out_specs=blk(),
        out_shape=jax.ShapeDtypeStruct((bsz * seq, wd), BF16),
        scratch_shapes=[pltpu.VMEM((RWKV_PAIRS, LANE, LANE), F32),
                        pltpu.VMEM((rows, wd), F32)],
        compiler_params=pltpu.CompilerParams(
            dimension_semantics=("parallel", "arbitrary"), vmem_limit_bytes=VMEM_LIMIT),
        name="rwkv_scan",
    )(*prep, proj, gnw, gnb)


def _pack_source(j):
    a1 = 2 * LRU_WIDTH
    b1 = a1 + 4 * GDN_WIDTH
    c0 = b1 + 2 * GDN_HEADS
    c1 = c0 + 3 * RWKV_WIDTH
    c2 = c1 + 2 * LORA
    start, valid = jnp.int32(0), jnp.int32(0)
    for first, count, src, width in ((COL_BQ, COL_CR - COL_BQ, a1, LANE),
                                     (COL_CR, COL_CG - COL_CR, c0, LANE),
                                     (COL_CG, COL_AX - COL_CG, c2, LANE),
                                     (COL_AX, COL_BBA - COL_AX, 0, LANE),
                                     (COL_BBA, 1, b1, 2 * GDN_HEADS),
                                     (COL_CWL, 1, c1, LORA),
                                     (COL_CAL, 1, c1 + LORA, LORA)):
        inside = (j >= first) & (j < first + count)
        start = jnp.where(inside, src + (j - first) * LANE, start)
        valid = jnp.where(inside, width, valid)
    return start, valid


def _pack_w_in_kernel(w_ref, o_ref):
    _, valid = _pack_source(pl.program_id(0))
    keep = lax.broadcasted_iota(jnp.int32, (w_ref.shape[0], w_ref.shape[2]), 0) < valid
    for lyr in range(w_ref.shape[1]):
        o_ref[lyr] = jnp.where(keep, w_ref[:, lyr, :], 0.0).astype(BF16)


def _pack_w_in(w_in):
    lyr, d, n = w_in.shape
    w_t = jnp.transpose(w_in, (2, 0, 1))
    return pl.pallas_call(
        _pack_w_in_kernel,
        grid=(N_PACKED // LANE,),
        in_specs=[pl.BlockSpec((pl.Element(LANE), pl.Element(lyr), pl.Element(d)),
                               lambda j: (_pack_source(j)[0], 0, 0))],
        out_specs=pl.BlockSpec((lyr, LANE, d), lambda j: (0, j, 0)),
        out_shape=jax.ShapeDtypeStruct((lyr, N_PACKED, d), BF16),
        compiler_params=pltpu.CompilerParams(
            dimension_semantics=("parallel",), vmem_limit_bytes=VMEM_LIMIT),
        name="pack_w_in",
    )(w_t)


def _pack_mu(mu):
    lyr = mu.shape[0]
    z = jnp.zeros((lyr, LANE - LORA), mu.dtype)
    c1 = 3 * RWKV_WIDTH
    return jnp.concatenate([mu[:, :c1], mu[:, c1:c1 + LORA], z, mu[:, c1 + LORA:], z], axis=-1)


def _block_diag(w):
    lyr, nb, bi, bj = w.shape
    eye = jnp.eye(nb, dtype=w.dtype)
    return jnp.einsum('lnij,nm->lnimj', w, eye).reshape(lyr, nb * bi, nb * bj)


def kernel(x, norm_g, w_in, w_out, lru_conv_w, lru_conv_b, lru_wx, lru_bx, lru_wa, lru_ba, lru_lambda,
           gdn_conv_w, gdn_a_log, gdn_dt_bias, gdn_norm_g, rwkv_mu, rwkv_w0, rwkv_w_up, rwkv_a0,
           rwkv_a_up, rwkv_k_k, rwkv_k_a, rwkv_r_k, rwkv_gn_w, rwkv_gn_b, final_norm_g):
    bsz, seq, d = x.shape
    depth = w_in.shape[0]
    x2 = x.reshape(bsz * seq, d)

    w_in_p = _pack_w_in(w_in)
    w_out_b = w_out.astype(BF16)
    wx_bd = _block_diag(lru_wx).astype(BF16)
    wa_bd = _block_diag(lru_wa).astype(BF16)
    mu_p = _pack_mu(rwkv_mu)
    pad_rows = lambda w: jnp.pad(w, ((0, 0), (0, LANE - LORA), (0, 0))).astype(BF16)
    wup_p = pad_rows(rwkv_w_up)
    aup_p = pad_rows(rwkv_a_up)
    gp = jnp.zeros((depth, 2, LANE), F32)
    gp = gp.at[:, 0, GDN_HEADS:2 * GDN_HEADS].set(gdn_a_log)
    gp = gp.at[:, 1, GDN_HEADS:2 * GDN_HEADS].set(gdn_dt_bias)
    rows = lambda t: t.reshape(depth, 1, -1)
    norm_g, lru_conv_b, lru_bx, lru_ba, lru_lambda, gdn_norm_g = map(
        rows, (norm_g, lru_conv_b, lru_bx, lru_ba, lru_lambda, gdn_norm_g))
    mu_p, rwkv_w0, rwkv_a0, rwkv_k_k, rwkv_k_a, rwkv_r_k, rwkv_gn_w, rwkv_gn_b = map(
        rows, (mu_p, rwkv_w0, rwkv_a0, rwkv_k_k, rwkv_k_a, rwkv_r_k, rwkv_gn_w, rwkv_gn_b))
    final_g = final_norm_g.reshape(1, -1)

    for l in range(depth):
        proj = _inproj(l, x2, norm_g, w_in_p)
        ya = _lru(l, proj, bsz, seq, lru_conv_w, lru_conv_b, wx_bd, lru_bx, wa_bd, lru_ba, lru_lambda)
        gdn_ops = _gdn_prep(l, proj, bsz, seq, gdn_conv_w, gp)
        yb = _gdn_scan(l, *gdn_ops, proj, bsz, seq, gdn_norm_g)
        rwkv_ops = _rwkv_prep(l, proj, bsz, seq, mu_p, rwkv_w0, wup_p, rwkv_a0, aup_p,
                              rwkv_k_k, rwkv_k_a, rwkv_r_k)
        yc = _rwkv_scan(l, rwkv_ops, proj, bsz, seq, rwkv_gn_w, rwkv_gn_b)
        x2 = _outproj(l, ya, yb, yc, x2, w_out_b, final_g, final_norm=(l == depth - 1))
    return x2.reshape(bsz, seq, d)
```

```python
import functools
import math

import jax
import jax.numpy as jnp
from jax import lax
from jax.experimental import pallas as pl
from jax.experimental.pallas import tpu as pltpu

F32 = jnp.float32
BF16 = jnp.bfloat16

NORM_EPS = 1e-6
CONV_W = 4
CHUNK = 64
LRU_WIDTH = 512
LRU_C = 8.0
GDN_HEAD = 128
GDN_HEADS = 6
GDN_WIDTH = GDN_HEADS * GDN_HEAD
RWKV_HEAD = 64
RWKV_WIDTH = 768
LORA = 96
RWKV_GN_EPS = 64e-5

LANE = 128
RWKV_PAIRS = RWKV_WIDTH // LANE
N_PACKED = 7680
COL_BQ, COL_BK, COL_BV, COL_BG = 0, 6, 12, 18
COL_CR, COL_CK, COL_CV, COL_CG = 24, 30, 36, 42
COL_AX, COL_AG = 48, 52
COL_BBA, COL_CWL, COL_CAL = 56, 57, 58
SCAN_ROWS = 512

VMEM_LIMIT = 52 * 1024 * 1024
OUTPROJ_VMEM_LIMIT = 58 * 1024 * 1024


def _dot(a, b):
    return jnp.dot(a.astype(BF16), b.astype(BF16), preferred_element_type=F32)


def _dot_nt(a, b):
    return lax.dot_general(a.astype(BF16), b.astype(BF16), (((1,), (1,)), ((), ())),
                           preferred_element_type=F32)


def _dot_tn(a, b):
    return lax.dot_general(a.astype(BF16), b.astype(BF16), (((0,), (0,)), ((), ())),
                           preferred_element_type=F32)


def _split2(x):
    hi = x.astype(BF16)
    lo = (x - hi.astype(F32)).astype(BF16)
    return hi, lo


def _split3(x):
    hi = x.astype(BF16)
    r1 = x - hi.astype(F32)
    mid = r1.astype(BF16)
    lo = (r1 - mid.astype(F32)).astype(BF16)
    return hi, mid, lo


def _cumsum_rows(ltri3, x):
    return jnp.dot(ltri3, jnp.concatenate(_split3(x), axis=0), preferred_element_type=F32)


def _segsum(x, seg2):
    return jnp.dot(jnp.concatenate(_split2(x), axis=1), seg2, preferred_element_type=F32)


def _tri_inverse_t(a_twins, eye_twin):
    c, w = eye_twin.shape
    first_half = lax.broadcasted_iota(jnp.int32, (c, w), 1) < c
    xs = [jnp.where(first_half, a, eye_twin) for a in a_twins]
    for _ in range(6):
        xbs = [x.astype(BF16) for x in xs]
        outs = [jnp.dot(xb[:, 0:c], xb, preferred_element_type=F32) for xb in xbs]
        xs = [jnp.where(first_half, o, x + o) for x, o in zip(xs, outs)]
    return xs


def _chunk_masks(width=CHUNK):
    row = lax.broadcasted_iota(jnp.int32, (CHUNK, width), 0)
    col = lax.broadcasted_iota(jnp.int32, (CHUNK, width), 1) % CHUNK
    return row >= col, row > col, row <= col, row < col, (row == col).astype(F32)


def _softplus(x):
    return jnp.maximum(x, 0.0) + jnp.log1p(jnp.exp(-jnp.abs(x)))


def _sigmoid(x):
    return 0.5 * jnp.tanh(0.5 * x) + 0.5


def _silu(x):
    h = 0.5 * x
    return h * jnp.tanh(h) + h


def _inproj_kernel(x_ref, g_ref, w_ref, o_ref, h_ref):
    @pl.when(pl.program_id(1) == 0)
    def _():
        x = x_ref[...]
        ms = jnp.mean(x * x, axis=-1, keepdims=True)
        h_ref[...] = (x * lax.rsqrt(ms + NORM_EPS) * g_ref[...]).astype(BF16)

    o_ref[...] = lax.dot_general(h_ref[...], w_ref[...], (((1,), (1,)), ((), ())),
                                 preferred_element_type=F32)


def _layer_spec(lyr, block, index):
    return pl.BlockSpec((None,) + block, lambda *grid_idx: (lyr,) + index(*grid_idx))


def _inproj(lyr, x2, g, w, tm=1024, tn=1536):
    m, d = x2.shape
    n = w.shape[1]
    tm = min(tm, m)
    assert m % tm == 0 and n % tn == 0 and w.shape[2] == d, (x2.shape, w.shape)
    return pl.pallas_call(
        _inproj_kernel,
        grid=(m // tm, n // tn),
        in_specs=[pl.BlockSpec((tm, d), lambda i, j: (i, 0)),
                  _layer_spec(lyr, (1, d), lambda i, j: (0, 0)),
                  _layer_spec(lyr, (tn, d), lambda i, j: (j, 0))],
        out_specs=pl.BlockSpec((tm, tn), lambda i, j: (i, j)),
        out_shape=jax.ShapeDtypeStruct((m, n), F32),
        scratch_shapes=[pltpu.VMEM((tm, d), BF16)],
        compiler_params=pltpu.CompilerParams(
            dimension_semantics=("parallel", "arbitrary"), vmem_limit_bytes=VMEM_LIMIT),
        name="inproj",
    )(x2, g, w)


def _outproj_kernel(ya_ref, yb_ref, yc_ref, x_ref, w_ref, fg_ref, o_ref, *, final_norm):
    wa = LRU_WIDTH
    wb = LRU_WIDTH + GDN_WIDTH
    acc = jnp.dot(ya_ref[...], w_ref[0:wa, :], preferred_element_type=F32)
    acc += jnp.dot(yb_ref[...], w_ref[wa:wb, :], preferred_element_type=F32)
    acc += jnp.dot(yc_ref[...], w_ref[wb:, :], preferred_element_type=F32)
    xn = x_ref[...] + acc
    if final_norm:
        ms = jnp.mean(xn * xn, axis=-1, keepdims=True)
        xn = xn * lax.rsqrt(ms + NORM_EPS) * fg_ref[...]
    o_ref[...] = xn


def _outproj(lyr, ya, yb, yc, x2, w, fg, final_norm, tm=1024):
    m, d = x2.shape
    tm = min(tm, m)
    assert m % tm == 0 and d == LRU_WIDTH + GDN_WIDTH + RWKV_WIDTH, x2.shape
    return pl.pallas_call(
        functools.partial(_outproj_kernel, final_norm=final_norm),
        grid=(m // tm,),
        in_specs=[pl.BlockSpec((tm, LRU_WIDTH), lambda i: (i, 0)),
                  pl.BlockSpec((tm, GDN_WIDTH), lambda i: (i, 0)),
                  pl.BlockSpec((tm, RWKV_WIDTH), lambda i: (i, 0)),
                  pl.BlockSpec((tm, d), lambda i: (i, 0)),
                  pl.BlockSpec((None, d, d), lambda i: (lyr, 0, 0), pipeline_mode=pl.Buffered(1)),
                  pl.BlockSpec((1, d), lambda i: (0, 0))],
        out_specs=pl.BlockSpec((tm, d), lambda i: (i, 0)),
        out_shape=jax.ShapeDtypeStruct((m, d), F32),
        compiler_params=pltpu.CompilerParams(
            dimension_semantics=("parallel",), vmem_limit_bytes=OUTPROJ_VMEM_LIMIT),
        name="outproj",
    )(ya, yb, yc, x2, w, fg)


def _lru_kernel(x_ref, g_ref, cw_ref, cb_ref, wx_ref, bx_ref, wa_ref, ba_ref, lam_ref,
                y_ref, ext_ref, h_ref):
    ts, w = x_ref.shape

    @pl.when(pl.program_id(1) == 0)
    def _():
        ext_ref[0:8, :] = jnp.zeros((8, w), F32)
        h_ref[...] = jnp.zeros_like(h_ref)

    ext_ref[8:, :] = x_ref[...]
    cw = cw_ref[...]
    xc = cb_ref[...] + cw[0:1, :] * ext_ref[pl.ds(5, ts), :]
    for j in range(1, CONV_W):
        xc = xc + cw[j:j + 1, :] * ext_ref[pl.ds(5 + j, ts), :]
    ext_ref[0:8, :] = ext_ref[pl.ds(ts, 8), :]

    gate_x = _sigmoid(_dot(xc, wx_ref[...]) + bx_ref[...])
    gate_a = _sigmoid(_dot(xc, wa_ref[...]) + ba_ref[...])
    log_a = -LRU_C * gate_a * _softplus(-lam_ref[...])
    a = jnp.exp(log_a)
    mult = jnp.sqrt(jnp.maximum(-jnp.tanh(log_a) * (a * a + 1.0), 0.0))
    u = mult * (gate_x * xc)

    row = lax.broadcasted_iota(jnp.int32, (ts, w), 0)
    d = 1
    while d < ts:
        keep = row >= d
        u = jnp.where(keep, a * pltpu.roll(u, d, axis=0) + u, u)
        a = jnp.where(keep, a * pltpu.roll(a, d, axis=0), a)
        d *= 2
    h = u + a * h_ref[0:1, :]
    h_ref[0:1, :] = h[ts - 1:ts, :]
    y_ref[...] = (h * _silu(g_ref[...])).astype(y_ref.dtype)


def _lru(lyr, proj, bsz, seq, cw, cb, wx, bx, wa, ba, lam, ts=512):
    ts = min(ts, seq)
    assert seq % ts == 0, seq
    ns = seq // ts
    w = LRU_WIDTH
    vec = lambda: _layer_spec(lyr, (1, w), lambda b, s: (0, 0))
    return pl.pallas_call(
        _lru_kernel,
        grid=(bsz, ns),
        in_specs=[pl.BlockSpec((ts, w), lambda b, s: (b * ns + s, COL_AX * LANE // w)),
                  pl.BlockSpec((ts, w), lambda b, s: (b * ns + s, COL_AG * LANE // w)),
                  _layer_spec(lyr, (CONV_W, w), lambda b, s: (0, 0)),
                  vec(),
                  _layer_spec(lyr, (w, w), lambda b, s: (0, 0)),
                  vec(),
                  _layer_spec(lyr, (w, w), lambda b, s: (0, 0)),
                  vec(), vec()],
        out_specs=pl.BlockSpec((ts, w), lambda b, s: (b * ns + s, 0)),
        out_shape=jax.ShapeDtypeStruct((bsz * seq, w), BF16),
        scratch_shapes=[pltpu.VMEM((ts + 8, w), F32), pltpu.VMEM((8, w), F32)],
        compiler_params=pltpu.CompilerParams(
            dimension_semantics=("parallel", "arbitrary"), vmem_limit_bytes=VMEM_LIMIT),
        name="lru",
    )(proj, proj, cw, cb, wx, bx, wa, ba, lam)


def _gdn_prep_kernel(q_ref, k_ref, v_ref, ba_ref, cwq_ref, cwk_ref, cwv_ref, gp_ref,
                     u_ref, w_ref, qd_ref, kd_ref, qk_ref, gl_ref,
                     ext_ref, qs_ref, ks_ref, vs_ref, bt_ref, gg_ref, *, group):
    head = pl.program_id(1)
    seq, dh = q_ref.shape
    c = CHUNK
    n_chunks = seq // c
    group = min(group, n_chunks)

    def conv_silu(x_ref, cw_ref):
        ext_ref[0:8, :] = jnp.zeros((8, dh), F32)
        ext_ref[8:, :] = x_ref[...]
        cw = cw_ref[...]
        xc = cw[0:1, :] * ext_ref[pl.ds(5, seq), :]
        for j in range(1, CONV_W):
            xc = xc + cw[j:j + 1, :] * ext_ref[pl.ds(5 + j, seq), :]
        return _silu(xc)

    def l2norm(t):
        return t * lax.rsqrt(jnp.sum(t * t, axis=-1, keepdims=True) + 1e-6)

    qs_ref[...] = l2norm(conv_silu(q_ref, cwq_ref)) * (GDN_HEAD ** -0.5)
    ks_ref[...] = l2norm(conv_silu(k_ref, cwk_ref))
    vs_ref[...] = conv_silu(v_ref, cwv_ref)

    ba = ba_ref[...]
    lane = lax.broadcasted_iota(jnp.int32, ba.shape, 1)
    gp = gp_ref[...]
    beta_all = _sigmoid(ba)
    g_all = -jnp.exp(gp[0:1, :]) * _softplus(ba + gp[1:2, :])
    beta_col = jnp.sum(jnp.where(lane == head, beta_all, 0.0), axis=1, keepdims=True)
    g_col = jnp.sum(jnp.where(lane == head + GDN_HEADS, g_all, 0.0), axis=1, keepdims=True)
    bt_ref[...] = jnp.broadcast_to(beta_col, (seq, dh))
    gg_ref[...] = jnp.broadcast_to(g_col, (seq, dh))

    incl = _chunk_masks()[0]
    incl2, _, upper2, strict_upper2, eye2 = _chunk_masks(2 * c)
    first_copy = lax.broadcasted_iota(jnp.int32, (c, dh), 1) < c
    ltri = incl.astype(BF16)
    ltri3 = jnp.concatenate([ltri, ltri, ltri], axis=1)

    def prepare(n, carry):
        rng = range(group)
        sls = [pl.ds(pl.multiple_of((n * group + i) * c, c), c) for i in rng]
        qs = [qs_ref[s, :] for s in sls]
        ks = [ks_ref[s, :] for s in sls]
        betas = [bt_ref[s, :] for s in sls]
        decs = [_cumsum_rows(ltri3, gg_ref[s, :]) for s in sls]
        kbs = [k * b for k, b in zip(ks, betas)]
        gram_ts = [_dot_nt(k, jnp.concatenate([kb, kb], axis=0)) for k, kb in zip(ks, kbs)]
        gram_qs = [_dot_nt(q, jnp.concatenate([k, k], axis=0)) for q, k in zip(qs, ks)]
        lmasks, lmask_ts = [], []
        for dec in decs:
            diff = dec - jnp.concatenate([dec, dec], axis=0).T[0:c, :]
            lmasks.append(jnp.where(incl2, jnp.exp(jnp.where(incl2, diff, 0.0)), 0.0))
            lmask_ts.append(jnp.where(upper2, jnp.exp(jnp.where(upper2, -diff, 0.0)), 0.0))
        t_mat_ts = _tri_inverse_t([jnp.where(strict_upper2, -(g * m), 0.0)
                                   for g, m in zip(gram_ts, lmask_ts)], eye2)
        edecs = [jnp.exp(dec) for dec in decs]
        uws = [_dot_tn(x, jnp.concatenate([vs_ref[s, :] * b, kb * e], axis=1))[c:, :]
               for x, s, b, kb, e in zip(t_mat_ts, sls, betas, kbs, edecs)]
        for i in rng:
            s = sls[i]
            dec = decs[i]
            dlast = dec[c - 1:c, :]
            u_ref[s, :] = uws[i][:, 0:dh]
            w_ref[s, :] = uws[i][:, dh:].astype(BF16)
            qk_ref[s, :] = jnp.where(first_copy, gram_qs[i] * lmasks[i], 0.0).astype(BF16)
            qd_ref[s, :] = (qs[i] * edecs[i]).astype(BF16)
            kd_ref[s, :] = (ks[i] * jnp.exp(dlast - dec)).astype(BF16)
            gl_ref[pl.ds(pl.multiple_of((n * group + i) * 8, 8), 8), :] = jnp.broadcast_to(
                jnp.exp(dlast), (8, dh))
        return carry

    lax.fori_loop(0, n_chunks // group, prepare, 0)


def _gdn_prep(lyr, proj, bsz, seq, cw, gp, group=32):
    group = min(group, seq // CHUNK)
    assert seq % (CHUNK * group) == 0, seq
    dh = GDN_HEAD
    n8 = seq // CHUNK * 8
    col = lambda base: pl.BlockSpec((seq, dh), lambda b, h: (b, base + h))
    cwspec = lambda base: _layer_spec(lyr, (CONV_W, dh), lambda b, h: (0, base + h))
    out = lambda: pl.BlockSpec((seq, dh), lambda b, h: (b, h))
    full = lambda: pltpu.VMEM((seq, dh), F32)
    sds = lambda dt: jax.ShapeDtypeStruct((bsz * seq, GDN_WIDTH), dt)
    return pl.pallas_call(
        functools.partial(_gdn_prep_kernel, group=group),
        grid=(bsz, GDN_HEADS),
        in_specs=[col(COL_BQ), col(COL_BK), col(COL_BV),
                  pl.BlockSpec((seq, dh), lambda b, h: (b, COL_BBA)),
                  cwspec(0), cwspec(GDN_HEADS), cwspec(2 * GDN_HEADS),
                  _layer_spec(lyr, (2, dh), lambda b, h: (0, 0))],
        out_specs=[out(), out(), out(), out(), out(),
                   pl.BlockSpec((n8, dh), lambda b, h: (b, h))],
        out_shape=[sds(F32), sds(BF16), sds(BF16), sds(BF16), sds(BF16),
                   jax.ShapeDtypeStruct((bsz * n8, GDN_WIDTH), F32)],
        scratch_shapes=[pltpu.VMEM((seq + 8, dh), F32), full(), full(), full(), full(), full()],
        compiler_params=pltpu.CompilerParams(
            dimension_semantics=("parallel", "parallel"), vmem_limit_bytes=VMEM_LIMIT),
        name="gdn_prep",
    )(proj, proj, proj, proj, cw, cw, cw, gp)


def _gdn_scan_kernel(u_ref, w_ref, qd_ref, kd_ref, qk_ref, gl_ref, gate_ref, ng_ref,
                     y_ref, state_ref, o_ref):
    rows = u_ref.shape[0]
    c = CHUNK
    dh = GDN_HEAD

    @pl.when(pl.program_id(1) == 0)
    def _():
        state_ref[...] = jnp.zeros_like(state_ref)

    heads = range(GDN_HEADS)
    lanes = [slice(h * dh, (h + 1) * dh) for h in heads]

    def step(n, carry):
        sl = pl.ds(pl.multiple_of(n * c, c), c)
        gsl = pl.ds(pl.multiple_of(n * 8, 8), 8)
        states = [state_ref[h] for h in heads]
        wss = [jnp.dot(jnp.concatenate([w_ref[sl, ls], qd_ref[sl, ls]], axis=0), st.astype(BF16),
                       preferred_element_type=F32) for ls, st in zip(lanes, states)]
        v_news = [(u_ref[sl, ls] - ws[0:c, :]).astype(BF16) for ls, ws in zip(lanes, wss)]
        upds = [lax.dot_general(kd_ref[sl, ls], vn, (((0,), (0,)), ((), ())), preferred_element_type=F32)
                for ls, vn in zip(lanes, v_news)]
        outs = [jnp.dot(qk_ref[sl, ls][:, 0:c], vn, preferred_element_type=F32)
                for ls, vn in zip(lanes, v_news)]
        for h in heads:
            ls = lanes[h]
            state_ref[h] = states[h] * gl_ref[gsl, ls][0:1, :] + upds[h]
            o_ref[sl, ls] = wss[h][c:, :] + outs[h]
        return carry

    lax.fori_loop(0, rows // c, step, 0)

    ng = ng_ref[...]
    for h in range(GDN_HEADS):
        ls = slice(h * dh, (h + 1) * dh)
        o = o_ref[:, ls]
        o = o * lax.rsqrt(jnp.mean(o * o, axis=-1, keepdims=True) + NORM_EPS) * ng
        y_ref[:, ls] = (o * _silu(gate_ref[:, ls])).astype(y_ref.dtype)


def _gdn_scan(lyr, u, w, qd, kd, qk, gl, proj, bsz, seq, ng):
    rows = min(SCAN_ROWS, seq)
    assert seq % rows == 0 and rows % CHUNK == 0, seq
    ns = seq // rows
    g8 = rows // CHUNK * 8
    wd = GDN_WIDTH
    blk = lambda: pl.BlockSpec((rows, wd), lambda b, s: (b * ns + s, 0))
    return pl.pallas_call(
        _gdn_scan_kernel,
        grid=(bsz, ns),
        in_specs=[blk(), blk(), blk(), blk(), blk(),
                  pl.BlockSpec((g8, wd), lambda b, s: (b * ns + s, 0)),
                  pl.BlockSpec((rows, wd), lambda b, s: (b * ns + s, COL_BG * LANE // wd)),
                  _layer_spec(lyr, (1, GDN_HEAD), lambda b, s: (0, 0))],
        out_specs=blk(),
        out_shape=jax.ShapeDtypeStruct((bsz * seq, wd), BF16),
        scratch_shapes=[pltpu.VMEM((GDN_HEADS, GDN_HEAD, GDN_HEAD), F32),
                        pltpu.VMEM((rows, wd), F32)],
        compiler_params=pltpu.CompilerParams(
            dimension_semantics=("parallel", "arbitrary"), vmem_limit_bytes=VMEM_LIMIT),
        name="gdn_scan",
    )(u, w, qd, kd, qk, gl, proj, ng)


def _pair_mask():
    r = lax.broadcasted_iota(jnp.int32, (LANE, LANE), 0) // RWKV_HEAD
    c = lax.broadcasted_iota(jnp.int32, (LANE, LANE), 1) // RWKV_HEAD
    return r == c


def _rwkv_prep_kernel(r_ref, k_ref, v_ref, wl_ref, al_ref,
                      mur_ref, muk_ref, muv_ref, muwl_ref, mual_ref,
                      w0_ref, wup_ref, a0_ref, aup_ref, kk_ref, ka_ref, rk_ref,
                      u0_ref, o0_ref, wt_ref, rt_ref, mrb_ref, b2_ref, kv_ref, pc_ref, bonus_ref,
                      rs_ref, k2_ref, vs_ref, kn_ref, kna_ref, lw_ref, *, group):
    seq, dw = r_ref.shape
    c = CHUNK
    hd = RWKV_HEAD
    n_chunks = seq // c
    group = min(group, n_chunks)

    seg = _pair_mask().astype(BF16)
    seg2 = jnp.concatenate([seg, seg], axis=0)

    def gates(r0, r1):
        rows = slice(r0, r1)
        first_row = lax.broadcasted_iota(jnp.int32, (r1 - r0, dw), 0) == 0

        def shift(x_ref, mu_ref):
            x = x_ref[rows, :]
            before = jnp.zeros((1, dw), F32) if r0 == 0 else x_ref[r0 - 8:r0, :][7:8, :]
            prev = jnp.where(first_row, before, pltpu.roll(x, 1, axis=0))
            return x + (prev - x) * mu_ref[...]

        r = shift(r_ref, mur_ref)
        k = shift(k_ref, muk_ref)
        v = shift(v_ref, muv_ref)
        wl = shift(wl_ref, muwl_ref)
        al = shift(al_ref, mual_ref)
        z = w0_ref[...] + _dot(jnp.tanh(wl), wup_ref[...])
        lw_ref[rows, :] = -math.exp(-0.5) * _sigmoid(z)
        a = _sigmoid(a0_ref[...] + _dot(al, aup_ref[...]))
        kn = k * kk_ref[...]
        kn = kn * lax.rsqrt(_segsum(kn * kn, seg2) + 1e-6)
        k2 = k * (1.0 + (a - 1.0) * ka_ref[...])
        rs_ref[rows, :] = r
        k2_ref[rows, :] = k2
        vs_ref[rows, :] = v
        kn_ref[rows, :] = kn
        kna_ref[rows, :] = kn * a
        bonus_ref[rows, :] = _segsum(r * k2 * rk_ref[...], seg2) * v

    incl = _chunk_masks()[0]
    incl2, strict2, _, strict_upper2, eye2 = _chunk_masks(2 * c)
    ltri = incl.astype(BF16)
    ltri3 = jnp.concatenate([ltri, ltri, ltri], axis=1)
    head0w = lax.broadcasted_iota(jnp.int32, (1, 2 * dw), 1) % dw < hd
    head0 = head0w[:, 0:dw]
    head_masks = [head0, jnp.logical_not(head0)]

    def prepare(n):
        sls = [slice((n * group + i) * c, (n * group + i + 1) * c) for i in range(group)]
        lws = [lw_ref[s, :] for s in sls]
        cums = [_cumsum_rows(ltri3, lw) for lw in lws]
        knas = [kna_ref[s, :] for s in sls]
        k2s = [k2_ref[s, :] for s in sls]
        vvs = [vs_ref[s, :] for s in sls]
        einvs = [jnp.exp(-cum) for cum in cums]
        a_ts = [-kn_ref[s, :] * jnp.exp(cum - lw) for s, cum, lw in zip(sls, cums, lws)]
        r_ts = [rs_ref[s, :] * jnp.exp(cum) for s, cum in zip(sls, cums)]
        b_ts = [(kna * e).astype(BF16) for kna, e in zip(knas, einvs)]
        a_bs = [a.astype(BF16) for a in a_ts]
        r_bs = [r.astype(BF16) for r in r_ts]
        zero = jnp.zeros((), BF16)

        def by_head(x):
            zx = jnp.zeros((), x.dtype)
            return jnp.concatenate([jnp.where(head0w[:, 0:x.shape[1]], x, zx),
                                    jnp.where(head0w[:, 0:x.shape[1]], zx, x)], axis=0)

        prob = [(i, m) for i in range(group) for m in head_masks]
        a_ab_ts = [jnp.pad(_dot_nt(jnp.where(m, b_ts[i], zero), a_bs[i]), ((0, 0), (0, c)))
                   for i, m in prob]
        m_rbs = [_dot_nt(r, by_head(b)) for r, b in zip(r_bs, b_ts)]
        g_ks = [_dot_nt(jnp.concatenate([a, r], axis=0), by_head((k2c * e).astype(BF16)))
                for a, r, k2c, e in zip(a_bs, r_bs, k2s, einvs)]
        t_mat_ts = _tri_inverse_t([jnp.where(strict_upper2, g, 0.0) for g in a_ab_ts], eye2)
        vbds = [by_head(v.astype(BF16)) for v in vvs]
        akvs = [_dot(jnp.where(strict2, g[0:c, :], 0.0), vbd) for g, vbd in zip(g_ks, vbds)]
        o0s = [_dot(jnp.where(incl2, g[c:, :], 0.0), vbd) for g, vbd in zip(g_ks, vbds)]
        tws = [_dot_tn(jnp.concatenate([t_mat_ts[2 * i], t_mat_ts[2 * i + 1]], axis=0),
                       by_head(jnp.concatenate([a_ts[i], akvs[i]], axis=1)))[c:, :]
               for i in range(group)]
        for i in range(group):
            s = sls[i]
            pick = lambda x0, x1: jnp.where(head0, x0, x1)
            clast = cums[i][c - 1:c, :]
            tail = jnp.exp(clast - cums[i])
            kv_full = _dot_tn(k2s[i] * tail, vvs[i])
            pc_full = jnp.broadcast_to(jnp.exp(clast), (dw, dw)).T
            u0_ref[s, :] = tws[i][:, dw:]
            o0_ref[s, :] = o0s[i]
            wt_ref[s, :] = tws[i][:, 0:dw].astype(BF16)
            rt_ref[s, :] = r_bs[i]
            mrb_ref[s, :] = jnp.where(incl2, m_rbs[i], 0.0).astype(BF16)
            b2_ref[s, :] = (knas[i] * tail).astype(BF16)
            kv_ref[s, :] = pick(kv_full[0:c, :], kv_full[c:, :])
            pc_ref[s, :] = pick(pc_full[0:c, :], pc_full[c:, :])

    for part in range(n_chunks // group):
        gates(part * group * c, (part + 1) * group * c)
        prepare(part)


def _rwkv_prep(lyr, proj, bsz, seq, mu, w0, wup, a0, aup, kk, ka, rk, group=16):
    group = min(group, seq // CHUNK)
    assert seq % (CHUNK * group) == 0, seq
    dw = LANE
    np_ = RWKV_PAIRS
    col = lambda base: pl.BlockSpec((seq, dw), lambda b, p: (b, base + p))
    fixed_col = lambda idx: pl.BlockSpec((seq, dw), lambda b, p: (b, idx))
    vec = lambda base: _layer_spec(lyr, (1, dw), lambda b, p: (0, base + p))
    fixed_vec = lambda idx: _layer_spec(lyr, (1, dw), lambda b, p: (0, idx))
    lora = lambda: _layer_spec(lyr, (dw, dw), lambda b, p: (0, p))
    out = lambda: pl.BlockSpec((seq, dw), lambda b, p: (b, p))
    full = lambda: pltpu.VMEM((seq, dw), F32)
    sds = lambda dt: jax.ShapeDtypeStruct((bsz * seq, RWKV_WIDTH), dt)
    return pl.pallas_call(
        functools.partial(_rwkv_prep_kernel, group=group),
        grid=(bsz, np_),
        in_specs=[col(COL_CR), col(COL_CK), col(COL_CV), fixed_col(COL_CWL), fixed_col(COL_CAL),
                  vec(0), vec(np_), vec(2 * np_), fixed_vec(3 * np_), fixed_vec(3 * np_ + 1),
                  vec(0), lora(), vec(0), lora(), vec(0), vec(0), vec(0)],
        out_specs=[out() for _ in range(9)],
        out_shape=[sds(F32), sds(F32), sds(BF16), sds(BF16), sds(BF16), sds(BF16),
                   sds(F32), sds(F32), sds(F32)],
        scratch_shapes=[full(), full(), full(), full(), full(), full()],
        compiler_params=pltpu.CompilerParams(
            dimension_semantics=("parallel", "parallel"), vmem_limit_bytes=VMEM_LIMIT),
        name="rwkv_prep",
    )(proj, proj, proj, proj, proj, mu, mu, mu, mu, mu, w0, wup, a0, aup, kk, ka, rk)


def _rwkv_scan_kernel(u0_ref, o0_ref, wt_ref, rt_ref, mrb_ref, b2_ref, kv_ref, pc_ref, bonus_ref,
                      gate_ref, gnw_ref, gnb_ref, y_ref, state_ref, o_ref):
    rows = u0_ref.shape[0]
    c = CHUNK
    dw = LANE
    pair = _pair_mask()

    @pl.when(pl.program_id(1) == 0)
    def _():
        state_ref[...] = jnp.zeros_like(state_ref)

    def block_diag(x):
        return jnp.where(pair, jnp.concatenate([x, x], axis=0), 0.0)

    pairs = range(RWKV_PAIRS)
    lanes = [slice(p * dw, (p + 1) * dw) for p in pairs]

    def step(n, carry):
        sl = pl.ds(pl.multiple_of(n * c, c), c)
        hms = [state_ref[p] for p in pairs]
        whs = [jnp.dot(jnp.concatenate([wt_ref[sl, ls], rt_ref[sl, ls]], axis=0), hm.astype(BF16),
                       preferred_element_type=F32) for ls, hm in zip(lanes, hms)]
        us = [u0_ref[sl, ls] + wh[0:c, :] for ls, wh in zip(lanes, whs)]
        upds = [lax.dot_general(b2_ref[sl, ls], u.astype(BF16), (((0,), (0,)), ((), ())),
                                preferred_element_type=F32) for ls, u in zip(lanes, us)]
        outs = [jnp.dot(mrb_ref[sl, ls], block_diag(u).astype(BF16), preferred_element_type=F32)
                for ls, u in zip(lanes, us)]
        for p in pairs:
            ls = lanes[p]
            pc = pc_ref[sl, ls]
            state_ref[p] = (jnp.concatenate([pc, pc], axis=0) * hms[p]
                            + jnp.where(pair, upds[p], 0.0) + block_diag(kv_ref[sl, ls]))
            o_ref[sl, ls] = whs[p][c:, :] + outs[p] + o0_ref[sl, ls]
        return carry

    lax.fori_loop(0, rows // c, step, 0)

    seg = pair.astype(BF16)
    seg2 = jnp.concatenate([seg, seg], axis=0)
    inv_n = 1.0 / RWKV_HEAD
    for p in range(RWKV_PAIRS):
        ls = slice(p * dw, (p + 1) * dw)
        o = o_ref[:, ls]
        cen = o - _segsum(o, seg2) * inv_n
        var = _segsum(cen * cen, seg2) * inv_n
        yn = cen * lax.rsqrt(var + RWKV_GN_EPS) * gnw_ref[:, ls] + gnb_ref[:, ls]
        y_ref[:, ls] = ((yn + bonus_ref[:, ls]) * _silu(gate_ref[:, ls])).astype(y_ref.dtype)


def _rwkv_scan(lyr, prep, proj, bsz, seq, gnw, gnb):
    rows = min(SCAN_ROWS, seq)
    assert seq % rows == 0 and rows % CHUNK == 0, seq
    ns = seq // rows
    wd = RWKV_WIDTH
    blk = lambda: pl.BlockSpec((rows, wd), lambda b, s: (b * ns + s, 0))
    vec = lambda: _layer_spec(lyr, (1, wd), lambda b, s: (0, 0))
    return pl.pallas_call(
        _rwkv_scan_kernel,
        grid=(bsz, ns),
        in_specs=[blk() for _ in range(9)]
        + [pl.BlockSpec((rows, wd), lambda b, s: (b * ns + s, COL_CG * LANE // wd)), vec(), vec()],
        out_specs=blk(),
        out_shape=jax.ShapeDtypeStruct((bsz * seq, wd), BF16),
        scratch_shapes=[pltpu.VMEM((RWKV_PAIRS, LANE, LANE), F32),
                        pltpu.VMEM((rows, wd), F32)],
        compiler_params=pltpu.CompilerParams(
            dimension_semantics=("parallel", "arbitrary"), vmem_limit_bytes=VMEM_LIMIT),
        name="rwkv_scan",
    )(*prep, proj, gnw, gnb)


def _pack_source(j):
    a1 = 2 * LRU_WIDTH
    b1 = a1 + 4 * GDN_WIDTH
    c0 = b1 + 2 * GDN_HEADS
    c1 = c0 + 3 * RWKV_WIDTH
    c2 = c1 + 2 * LORA
    start, valid = jnp.int32(0), jnp.int32(0)
    for first, count, src, width in ((COL_BQ, COL_CR - COL_BQ, a1, LANE),
                                     (COL_CR, COL_CG - COL_CR, c0, LANE),
                                     (COL_CG, COL_AX - COL_CG, c2, LANE),
                                     (COL_AX, COL_BBA - COL_AX, 0, LANE),
                                     (COL_BBA, 1, b1, 2 * GDN_HEADS),
                                     (COL_CWL, 1, c1, LORA),
                                     (COL_CAL, 1, c1 + LORA, LORA)):
        inside = (j >= first) & (j < first + count)
        start = jnp.where(inside, src + (j - first) * LANE, start)
        valid = jnp.where(inside, width, valid)
    return start, valid


def _pack_w_in_kernel(w_ref, o_ref):
    _, valid = _pack_source(pl.program_id(0))
    keep = lax.broadcasted_iota(jnp.int32, (w_ref.shape[0], w_ref.shape[2]), 0) < valid
    for lyr in range(w_ref.shape[1]):
        o_ref[lyr] = jnp.where(keep, w_ref[:, lyr, :], 0.0).astype(BF16)


def _pack_w_in(w_in):
    lyr, d, n = w_in.shape
    w_t = jnp.transpose(w_in, (2, 0, 1))
    return pl.pallas_call(
        _pack_w_in_kernel,
        grid=(N_PACKED // LANE,),
        in_specs=[pl.BlockSpec((pl.Element(LANE), pl.Element(lyr), pl.Element(d)),
                               lambda j: (_pack_source(j)[0], 0, 0))],
        out_specs=pl.BlockSpec((lyr, LANE, d), lambda j: (0, j, 0)),
        out_shape=jax.ShapeDtypeStruct((lyr, N_PACKED, d), BF16),
        compiler_params=pltpu.CompilerParams(
            dimension_semantics=("parallel",), vmem_limit_bytes=VMEM_LIMIT),
        name="pack_w_in",
    )(w_t)


def _pack_mu(mu):
    lyr = mu.shape[0]
    z = jnp.zeros((lyr, LANE - LORA), mu.dtype)
    c1 = 3 * RWKV_WIDTH
    return jnp.concatenate([mu[:, :c1], mu[:, c1:c1 + LORA], z, mu[:, c1 + LORA:], z], axis=-1)


def _block_diag(w):
    lyr, nb, bi, bj = w.shape
    eye = jnp.eye(nb, dtype=w.dtype)
    return jnp.einsum('lnij,nm->lnimj', w, eye).reshape(lyr, nb * bi, nb * bj)


def kernel(x, norm_g, w_in, w_out, lru_conv_w, lru_conv_b, lru_wx, lru_bx, lru_wa, lru_ba, lru_lambda,
           gdn_conv_w, gdn_a_log, gdn_dt_bias, gdn_norm_g, rwkv_mu, rwkv_w0, rwkv_w_up, rwkv_a0,
           rwkv_a_up, rwkv_k_k, rwkv_k_a, rwkv_r_k, rwkv_gn_w, rwkv_gn_b, final_norm_g):
    bsz, seq, d = x.shape
    depth = w_in.shape[0]
    x2 = x.reshape(bsz * seq, d)

    w_in_p = _pack_w_in(w_in)
    w_out_b = w_out.astype(BF16)
    wx_bd = _block_diag(lru_wx).astype(BF16)
    wa_bd = _block_diag(lru_wa).astype(BF16)
    mu_p = _pack_mu(rwkv_mu)
    pad_rows = lambda w: jnp.pad(w, ((0, 0), (0, LANE - LORA), (0, 0))).astype(BF16)
    wup_p = pad_rows(rwkv_w_up)
    aup_p = pad_rows(rwkv_a_up)
    gp = jnp.zeros((depth, 2, LANE), F32)
    gp = gp.at[:, 0, GDN_HEADS:2 * GDN_HEADS].set(gdn_a_log)
    gp = gp.at[:, 1, GDN_HEADS:2 * GDN_HEADS].set(gdn_dt_bias)
    rows = lambda t: t.reshape(depth, 1, -1)
    norm_g, lru_conv_b, lru_bx, lru_ba, lru_lambda, gdn_norm_g = map(
        rows, (norm_g, lru_conv_b, lru_bx, lru_ba, lru_lambda, gdn_norm_g))
    mu_p, rwkv_w0, rwkv_a0, rwkv_k_k, rwkv_k_a, rwkv_r_k, rwkv_gn_w, rwkv_gn_b = map(
        rows, (mu_p, rwkv_w0, rwkv_a0, rwkv_k_k, rwkv_k_a, rwkv_r_k, rwkv_gn_w, rwkv_gn_b))
    final_g = final_norm_g.reshape(1, -1)

    for l in range(depth):
        proj = _inproj(l, x2, norm_g, w_in_p)
        ya = _lru(l, proj, bsz, seq, lru_conv_w, lru_conv_b, wx_bd, lru_bx, wa_bd, lru_ba, lru_lambda)
        gdn_ops = _gdn_prep(l, proj, bsz, seq, gdn_conv_w, gp)
        yb = _gdn_scan(l, *gdn_ops, proj, bsz, seq, gdn_norm_g)
        rwkv_ops = _rwkv_prep(l, proj, bsz, seq, mu_p, rwkv_w0, wup_p, rwkv_a0, aup_p,
                              rwkv_k_k, rwkv_k_a, rwkv_r_k)
        yc = _rwkv_scan(l, rwkv_ops, proj, bsz, seq, rwkv_gn_w, rwkv_gn_b)
        x2 = _outproj(l, ya, yb, yc, x2, w_out_b, final_g, final_norm=(l == depth - 1))
    return x2.reshape(bsz, seq, d)
```
